```python
import math
import numpy as np
import jax
import jax.numpy as jnp
from jax import lax

D_MODEL = 1024
BATCH = 1
SEQ = 16384
DEPTH = 2
DEC_BATCH = 128
DEC_SEQ = 1
PAST_LEN = 16384
PAGE_SIZE = 128

ROPE_THETA = 10000.0
NORM_EPS = 1e-6
Q_BLOCK = 128

MLA_HEADS = 8
MLA_Q_LORA = 256
MLA_KV_LORA = 128
MLA_NOPE = 64
MLA_ROPE = 32
MLA_V = 64
MLA_CACHE_W = MLA_KV_LORA + MLA_ROPE
MLA_SCALE = (MLA_NOPE + MLA_ROPE) ** -0.5

NSA_HEADS = 8
NSA_KV_HEADS = 1
NSA_HEAD_DIM = 64
NSA_GROUP = NSA_HEADS // NSA_KV_HEADS
NSA_BLOCK = 64
NSA_TOP_N = 16
NSA_WINDOW = 512
NSA_SCALE = NSA_HEAD_DIM ** -0.5
NSA_FORCED = 1.0e4

DIFF_HEADS = 8
DIFF_KV_HEADS = 2
DIFF_HEAD_DIM = 64
DIFF_GROUP = DIFF_HEADS // DIFF_KV_HEADS
DIFF_SCALE = DIFF_HEAD_DIM ** -0.5
DIFF_LAYER = 1
DIFF_LAMBDA_INIT = 0.8 - 0.6 * math.exp(-0.3 * DIFF_LAYER)

D_FF_RAW = -(-8 * D_MODEL // 3)
D_FF = -(-D_FF_RAW // 256) * 256

NSA_KV_W = NSA_KV_HEADS * NSA_HEAD_DIM
IN_A_SPLITS = (MLA_Q_LORA, MLA_KV_LORA, MLA_ROPE, NSA_HEADS * NSA_HEAD_DIM,
               NSA_KV_W, NSA_KV_W, NSA_KV_W, NSA_KV_W, NSA_KV_W, NSA_KV_W, 3 * NSA_HEADS)
IN_A = MLA_Q_LORA + MLA_KV_LORA + MLA_ROPE + NSA_HEADS * NSA_HEAD_DIM + 6 * NSA_KV_W + 3 * NSA_HEADS
MIX_A_OUT = MLA_HEADS * MLA_V + NSA_HEADS * NSA_HEAD_DIM
IN_C_SPLITS = (DIFF_HEADS * 2 * DIFF_HEAD_DIM, DIFF_KV_HEADS * 2 * DIFF_HEAD_DIM, DIFF_KV_HEADS * 2 * DIFF_HEAD_DIM)
IN_C = DIFF_HEADS * 2 * DIFF_HEAD_DIM + 4 * DIFF_KV_HEADS * DIFF_HEAD_DIM
MIX_C_OUT = DIFF_HEADS * 2 * DIFF_HEAD_DIM

kernel_name = 'hybrid_mla_nsa_diffattn_decoder_step'


def rms_norm(x, g):
    xf = x.astype(jnp.float32)
    y = xf * lax.rsqrt(jnp.mean(xf * xf, axis=-1, keepdims=True) + NORM_EPS)
    return (y * g.astype(jnp.float32)).astype(x.dtype)


def rope_cos_sin(pos, dim):
    inv = ROPE_THETA ** (-jnp.arange(0, dim, 2, dtype=jnp.float32) / dim)
    ang = pos.astype(jnp.float32)[:, None] * inv[None, :]
    ang = jnp.concatenate([ang, ang], axis=-1)
    return jnp.cos(ang), jnp.sin(ang)


def apply_rope(x, cos, sin):
    shp = (cos.shape[0],) + (1,) * (x.ndim - 3) + (cos.shape[1],)
    half = x.shape[-1] // 2
    rot = jnp.concatenate([-x[..., half:], x[..., :half]], axis=-1)
    return (x * cos.reshape(shp) + rot * sin.reshape(shp)).astype(x.dtype)


def split_cols(z, widths):
    offs = np.cumsum([0] + list(widths)).tolist()
    return [z[..., offs[i]:offs[i + 1]] for i in range(len(widths))]


def masked_softmax(s, mask):
    s = jnp.where(mask, s, -jnp.inf)
    m = jnp.max(s, axis=-1, keepdims=True)
    m = jnp.where(jnp.isfinite(m), m, 0.0)
    e = jnp.exp(s - m)
    den = jnp.sum(e, axis=-1, keepdims=True)
    return e / jnp.where(den > 0.0, den, 1.0)


def swiglu(h, w_gate, w_up, w_down):
    return (jax.nn.silu(h @ w_gate) * (h @ w_up)) @ w_down


def dense_causal_blocks(q, k, v, pos, scale):
    B, T, H, Dk = q.shape
    nq = T // Q_BLOCK
    qb = jnp.moveaxis(q.reshape(B, nq, Q_BLOCK, H, Dk), 1, 0)

    def block(a):
        qi, qp = a
        s = jnp.einsum('bqhd,bshd->bhqs', qi, k, preferred_element_type=jnp.float32) * scale
        p = masked_softmax(s, pos[None, :] <= qp[:, None])
        return jnp.einsum('bhqs,bshd->bqhd', p.astype(v.dtype), v)

    o = lax.map(block, (qb, pos.reshape(nq, Q_BLOCK)))
    return jnp.moveaxis(o, 0, 1).reshape(B, T, H, v.shape[-1])


def nsa_compress(rows, pe, w1, w2):
    B, S, G, D = rows.shape
    nb = S // NSA_BLOCK
    blk = rows.reshape(B, nb, NSA_BLOCK, G, D) + pe[:, None, :]
    flat = blk.transpose(0, 1, 3, 2, 4).reshape(B, nb, G, NSA_BLOCK * D)
    return jax.nn.silu(flat @ w1) @ w2


def block_view(rows):
    B, S, G, D = rows.shape
    return rows.reshape(B, S // NSA_BLOCK, NSA_BLOCK, G, D).transpose(0, 3, 1, 2, 4)


def nsa_cmp_branch(q, kc, vc, qpos):
    s = jnp.einsum('btgrd,bngd->bgrtn', q, kc, preferred_element_type=jnp.float32) * NSA_SCALE
    nb = kc.shape[1]
    vis = ((jnp.arange(nb) + 1) * NSA_BLOCK - 1)[None, :] <= qpos[:, None]
    p = masked_softmax(s, vis)
    o = jnp.einsum('bgrtn,bngd->btgrd', p.astype(vc.dtype), vc)
    return o, p


def nsa_select(imp, qpos, nb_sel):
    nbc = imp.shape[-1]
    imp = jnp.pad(imp, ((0, 0), (0, 0), (0, 0), (0, nb_sel - nbc)))
    blk = jnp.arange(nb_sel)[None, :]
    cur = (qpos // NSA_BLOCK)[:, None]
    forced = (blk == 0) | (blk == cur) | (blk == cur - 1)
    score = jnp.where(forced, NSA_FORCED, imp)
    score = jnp.where(blk <= cur, score, -jnp.inf)
    top, idx = lax.top_k(score, min(NSA_TOP_N, nb_sel))
    return idx.astype(jnp.int32), jnp.isfinite(top)


def nsa_sel_attend(q, kb, vb, idx, valid, qpos):
    B, Tq, G, R, D = q.shape
    n = idx.shape[-1]
    bi = jnp.arange(B)[:, None, None, None]
    gi = jnp.arange(G)[None, :, None, None]
    ksel = kb[bi, gi, idx].reshape(B, G, Tq, n * NSA_BLOCK, D)
    vsel = vb[bi, gi, idx].reshape(B, G, Tq, n * NSA_BLOCK, D)
    kpos = (idx[..., None] * NSA_BLOCK + jnp.arange(NSA_BLOCK)).reshape(B, G, Tq, n * NSA_BLOCK)
    mask = jnp.repeat(valid, NSA_BLOCK, axis=-1) & (kpos <= qpos[:, None])
    s = jnp.einsum('btgrd,bgtkd->bgrtk', q, ksel, preferred_element_type=jnp.float32) * NSA_SCALE
    p = masked_softmax(s, mask[:, :, None])
    return jnp.einsum('bgrtk,bgtkd->btgrd', p.astype(vsel.dtype), vsel)


def window_attend(q, kw, vw, qpos, kpos):
    s = jnp.einsum('btgrd,bsgd->bgrts', q, kw, preferred_element_type=jnp.float32) * NSA_SCALE
    d = qpos[:, None] - kpos[None, :]
    mask = (d >= 0) & (d < NSA_WINDOW) & (kpos[None, :] >= 0)
    p = masked_softmax(s, mask)
    return jnp.einsum('bgrts,bsgd->btgrd', p.astype(vw.dtype), vw)


def even_project(h, pos, pa):
    B, T, _ = h.shape
    c_q, c_kv, k_r, q_n, k_c, v_c, k_s, v_s, k_w, v_w, g = split_cols(h @ pa['w_in'], IN_A_SPLITS)
    cos_m, sin_m = rope_cos_sin(pos, MLA_ROPE)
    q = (rms_norm(c_q, pa['mla_q_norm']) @ pa['mla_uq']).reshape(B, T, MLA_HEADS, MLA_NOPE + MLA_ROPE)
    mla_rows = jnp.concatenate([rms_norm(c_kv, pa['mla_kv_norm']), apply_rope(k_r, cos_m, sin_m)], axis=-1)
    cos_n, sin_n = rope_cos_sin(pos, NSA_HEAD_DIM)

    def heads(a):
        return a.reshape(B, T, NSA_KV_HEADS, NSA_HEAD_DIM)

    q_n = q_n.reshape(B, T, NSA_KV_HEADS, NSA_GROUP, NSA_HEAD_DIM)
    return {
        'q_nope': q[..., :MLA_NOPE],
        'q_pe': apply_rope(q[..., MLA_NOPE:], cos_m, sin_m),
        'mla_rows': mla_rows,
        'q_cmp': q_n,
        'q_rot': apply_rope(q_n, cos_n, sin_n),
        'cmp_rows': jnp.stack([heads(k_c), heads(v_c)], axis=2),
        'slc_rows': jnp.stack([apply_rope(heads(k_s), cos_n, sin_n), heads(v_s)], axis=2),
        'win_rows': jnp.stack([apply_rope(heads(k_w), cos_n, sin_n), heads(v_w)], axis=2),
        'gates': jax.nn.sigmoid(g.astype(jnp.float32)).astype(h.dtype).reshape(B, T, 3, NSA_KV_HEADS, NSA_GROUP, 1),
    }


def even_merge(pj, o_mla, o_c, o_s, o_w, w_out):
    B, T = o_mla.shape[:2]
    g = pj['gates']
    o_nsa = g[:, :, 0] * o_c + g[:, :, 1] * o_s + g[:, :, 2] * o_w
    mixed = jnp.concatenate([o_mla.reshape(B, T, MLA_HEADS * MLA_V),
                             o_nsa.reshape(B, T, NSA_HEADS * NSA_HEAD_DIM)], axis=-1)
    return mixed @ w_out


def even_prompt(h, pos, pa):
    B, T, _ = h.shape
    pj = even_project(h, pos, pa)
    c_kv = pj['mla_rows'][..., :MLA_KV_LORA]
    k_pe = pj['mla_rows'][..., MLA_KV_LORA:]
    k = jnp.concatenate([jnp.einsum('btc,chd->bthd', c_kv, pa['mla_uk']),
                         jnp.broadcast_to(k_pe[:, :, None, :], (B, T, MLA_HEADS, MLA_ROPE))], axis=-1)
    v = jnp.einsum('btc,chd->bthd', c_kv, pa['mla_uv'])
    q = jnp.concatenate([pj['q_nope'], pj['q_pe']], axis=-1)
    o_mla = dense_causal_blocks(q, k, v, pos, MLA_SCALE)
    kc = nsa_compress(pj['cmp_rows'][:, :, 0], pa['pe_k'], pa['cmp_w1_k'], pa['cmp_w2_k'])
    vc = nsa_compress(pj['cmp_rows'][:, :, 1], pa['pe_v'], pa['cmp_w1_v'], pa['cmp_w2_v'])
    o_c, p_c = nsa_cmp_branch(pj['q_cmp'], kc, vc, pos)
    idx, valid = nsa_select(jnp.sum(p_c, axis=2), pos, T // NSA_BLOCK)
    kb = block_view(pj['slc_rows'][:, :, 0])
    vb = block_view(pj['slc_rows'][:, :, 1])
    nq = T // Q_BLOCK
    n_sel = idx.shape[-1]
    q_blocks = jnp.moveaxis(pj['q_rot'].reshape(B, nq, Q_BLOCK, NSA_KV_HEADS, NSA_GROUP, NSA_HEAD_DIM), 1, 0)
    idx_b = jnp.moveaxis(idx.reshape(B, NSA_KV_HEADS, nq, Q_BLOCK, n_sel), 2, 0)
    val_b = jnp.moveaxis(valid.reshape(B, NSA_KV_HEADS, nq, Q_BLOCK, n_sel), 2, 0)
    pos_b = pos.reshape(nq, Q_BLOCK)
    o_s = lax.map(lambda a: nsa_sel_attend(a[0], kb, vb, a[1], a[2], a[3]), (q_blocks, idx_b, val_b, pos_b))
    o_s = jnp.moveaxis(o_s, 0, 1).reshape(B, T, NSA_KV_HEADS, NSA_GROUP, NSA_HEAD_DIM)
    win = jnp.pad(pj['win_rows'], ((0, 0), (NSA_WINDOW, 0), (0, 0), (0, 0), (0, 0)))

    def win_block(a):
        qi, qp = a
        start = qp[0]
        kw = lax.dynamic_slice_in_dim(win, start, NSA_WINDOW + Q_BLOCK, axis=1)
        kpos = start - NSA_WINDOW + jnp.arange(NSA_WINDOW + Q_BLOCK, dtype=jnp.int32)
        return window_attend(qi, kw[:, :, 0], kw[:, :, 1], qp, kpos)

    o_w = lax.map(win_block, (q_blocks, pos_b))
    o_w = jnp.moveaxis(o_w, 0, 1).reshape(B, T, NSA_KV_HEADS, NSA_GROUP, NSA_HEAD_DIM)
    out = even_merge(pj, o_mla, o_c, o_s, o_w, pa['w_out'])
    win_state = pj['win_rows'][:, T - min(NSA_WINDOW, T):]
    return out, (pj['mla_rows'], pj['cmp_rows'], pj['slc_rows'], win_state)


def even_sample(h, pos, cache_mla, cache_nsa_cmp, cache_nsa_slc, state_nsa_win, page_table, pa):
    B, T, _ = h.shape
    past = page_table.shape[1] * PAGE_SIZE
    total = past + T
    pj = even_project(h, pos, pa)
    q_abs = jnp.concatenate([jnp.einsum('bthd,chd->bthc', pj['q_nope'], pa['mla_uk']), pj['q_pe']], axis=-1)
    kpos = jnp.arange(total, dtype=jnp.int32)

    def mla_one(a):
        qa, new_rows, pt = a
        rows = jnp.concatenate([cache_mla[pt].reshape(past, MLA_CACHE_W), new_rows], axis=0)
        s = jnp.einsum('thc,sc->hts', qa, rows, preferred_element_type=jnp.float32) * MLA_SCALE
        p = masked_softmax(s, kpos[None, :] <= pos[:, None])
        return jnp.einsum('hts,sc->thc', p.astype(rows.dtype), rows[:, :MLA_KV_LORA])

    o_lat = lax.map(mla_one, (q_abs, pj['mla_rows'], page_table))
    o_mla = jnp.einsum('bthc,chd->bthd', o_lat, pa['mla_uv'])
    nb_cmp = total // NSA_BLOCK
    nb_sel = -(-total // NSA_BLOCK)
    row_shape = (past,) + cache_nsa_cmp.shape[2:]

    def nsa_one(a):
        qc, qr, new_c, new_s, pt = a
        rows_c = jnp.concatenate([cache_nsa_cmp[pt].reshape(row_shape), new_c], axis=0)[None]
        rows_s = jnp.concatenate([cache_nsa_slc[pt].reshape(row_shape), new_s], axis=0)[None]
        kc = nsa_compress(rows_c[:, :nb_cmp * NSA_BLOCK, 0], pa['pe_k'], pa['cmp_w1_k'], pa['cmp_w2_k'])
        vc = nsa_compress(rows_c[:, :nb_cmp * NSA_BLOCK, 1], pa['pe_v'], pa['cmp_w1_v'], pa['cmp_w2_v'])
        o_c1, p_c = nsa_cmp_branch(qc[None], kc, vc, pos)
        idx, valid = nsa_select(jnp.sum(p_c, axis=2), pos, nb_sel)
        rows_s = jnp.pad(rows_s, ((0, 0), (0, nb_sel * NSA_BLOCK - total), (0, 0), (0, 0), (0, 0)))
        o_s1 = nsa_sel_attend(qr[None], block_view(rows_s[:, :, 0]), block_view(rows_s[:, :, 1]), idx, valid, pos)
        return o_c1[0], o_s1[0]

    o_c, o_s = lax.map(nsa_one, (pj['q_cmp'], pj['q_rot'], pj['cmp_rows'], pj['slc_rows'], page_table))
    wb = state_nsa_win.shape[1]
    win = jnp.concatenate([state_nsa_win, pj['win_rows']], axis=1)
    wpos = past - wb + jnp.arange(wb + T, dtype=jnp.int32)
    o_w = window_attend(pj['q_rot'], win[:, :, 0], win[:, :, 1], pos, wpos)
    out = even_merge(pj, o_mla, o_c, o_s, o_w, pa['w_out'])
    return out, (pj['mla_rows'], pj['cmp_rows'], pj['slc_rows'], win[:, T:])


def diff_lambda(pc):
    f = lambda a: a.astype(jnp.float32)
    return (jnp.exp(jnp.sum(f(pc['lq1']) * f(pc['lk1']))) - jnp.exp(jnp.sum(f(pc['lq2']) * f(pc['lk2'])))
            + DIFF_LAMBDA_INIT)


def diff_project(h, pos, w_in):
    B, T, _ = h.shape
    q, k, v = split_cols(h @ w_in, IN_C_SPLITS)
    cos, sin = rope_cos_sin(pos, DIFF_HEAD_DIM)
    q = apply_rope(q.reshape(B, T, DIFF_KV_HEADS, DIFF_GROUP, 2, DIFF_HEAD_DIM), cos, sin)
    k = apply_rope(k.reshape(B, T, DIFF_KV_HEADS, 2, DIFF_HEAD_DIM), cos, sin)
    rows = jnp.stack([k.reshape(B, T, DIFF_KV_HEADS, 2 * DIFF_HEAD_DIM),
                      v.reshape(B, T, DIFF_KV_HEADS, 2 * DIFF_HEAD_DIM)], axis=2)
    return q, rows


def diff_attend(q, rows, qpos, kpos, lam):
    B, S = rows.shape[:2]
    k = rows[:, :, 0].reshape(B, S, DIFF_KV_HEADS, 2, DIFF_HEAD_DIM)
    v = rows[:, :, 1]
    s = jnp.einsum('btgrcd,bsgcd->bgrcts', q, k, preferred_element_type=jnp.float32) * DIFF_SCALE
    a = masked_softmax(s, kpos[None, :] <= qpos[:, None])
    w = a[:, :, :, 0] - lam * a[:, :, :, 1]
    return jnp.einsum('bgrts,bsge->btgre', w.astype(v.dtype), v)


def diff_out(o, subln, w_out):
    B, T = o.shape[:2]
    o = rms_norm(o.reshape(B, T, DIFF_HEADS, 2 * DIFF_HEAD_DIM), subln) * (1.0 - DIFF_LAMBDA_INIT)
    return o.reshape(B, T, MIX_C_OUT) @ w_out


def odd_prompt(h, pos, pc):
    B, T, _ = h.shape
    q, rows = diff_project(h, pos, pc['w_in'])
    lam = diff_lambda(pc)
    nq = T // Q_BLOCK
    qb = jnp.moveaxis(q.reshape((B, nq, Q_BLOCK) + q.shape[2:]), 1, 0)
    o = lax.map(lambda a: diff_attend(a[0], rows, a[1], pos, lam), (qb, pos.reshape(nq, Q_BLOCK)))
    o = jnp.moveaxis(o, 0, 1).reshape(B, T, DIFF_KV_HEADS, DIFF_GROUP, 2 * DIFF_HEAD_DIM)
    return diff_out(o, pc['subln'], pc['w_out']), rows


def odd_sample(h, pos, cache_diff, page_table, pc):
    q, rows = diff_project(h, pos, pc['w_in'])
    lam = diff_lambda(pc)
    past = page_table.shape[1] * PAGE_SIZE
    row_shape = (past,) + cache_diff.shape[2:]
    kpos = jnp.arange(past + h.shape[1], dtype=jnp.int32)

    def one(a):
        qi, new_rows, pt = a
        allr = jnp.concatenate([cache_diff[pt].reshape(row_shape), new_rows], axis=0)[None]
        return diff_attend(qi[None], allr, pos, kpos, lam)[0]

    o = lax.map(one, (q, rows, page_table))
    return diff_out(o, pc['subln'], pc['w_out']), rows


def setup_inputs(seed: int = 0) -> dict:
    key = jax.random.key(seed)
    keys = iter(jax.random.split(key, 48))
    f32 = jnp.float32

    def normal(shape, scale=1.0):
        return jax.random.normal(next(keys), shape, f32) * scale

    def gain(shape):
        return 1.0 + 0.05 * jax.random.normal(next(keys), shape, f32)

    n_pages = PAST_LEN // PAGE_SIZE
    n_used = DEC_BATCH * n_pages
    n_pool = n_used + max(1, n_used // 4)
    win_buf = min(NSA_WINDOW, PAST_LEN)
    x_prompt = normal((BATCH, SEQ, D_MODEL))
    x_sample = normal((DEC_BATCH, DEC_SEQ, D_MODEL))
    cache_mla = normal((n_pool, PAGE_SIZE, MLA_CACHE_W))
    cache_nsa_cmp = normal((n_pool, PAGE_SIZE, 2, NSA_KV_HEADS, NSA_HEAD_DIM))
    cache_nsa_slc = normal((n_pool, PAGE_SIZE, 2, NSA_KV_HEADS, NSA_HEAD_DIM))
    state_nsa_win = normal((DEC_BATCH, win_buf, 2, NSA_KV_HEADS, NSA_HEAD_DIM))
    cache_diff = normal((n_pool, PAGE_SIZE, 2, DIFF_KV_HEADS, 2 * DIFF_HEAD_DIM))
    page_table = jax.random.permutation(next(keys), n_pool)[:n_used].reshape(DEC_BATCH, n_pages).astype(jnp.int32)
    return {
        'x_prompt': x_prompt,
        'x_sample': x_sample,
        'cache_mla': cache_mla,
        'cache_nsa_cmp': cache_nsa_cmp,
        'cache_nsa_slc': cache_nsa_slc,
        'state_nsa_win': state_nsa_win,
        'cache_diff': cache_diff,
        'page_table': page_table,
        'attn_norm': gain((DEPTH, D_MODEL)),
        'ffn_norm': gain((DEPTH, D_MODEL)),
        'final_norm': gain((D_MODEL,)),
        'w_in_a': normal((D_MODEL, IN_A), D_MODEL ** -0.5),
        'mla_q_norm': gain((MLA_Q_LORA,)),
        'mla_kv_norm': gain((MLA_KV_LORA,)),
        'w_mla_uq': normal((MLA_Q_LORA, MLA_HEADS * (MLA_NOPE + MLA_ROPE)), MLA_Q_LORA ** -0.5),
        'w_mla_uk': normal((MLA_KV_LORA, MLA_HEADS, MLA_NOPE), MLA_KV_LORA ** -0.5),
        'w_mla_uv': normal((MLA_KV_LORA, MLA_HEADS, MLA_V), MLA_KV_LORA ** -0.5),
        'nsa_pe_k': normal((NSA_BLOCK, NSA_HEAD_DIM), 0.5),
        'nsa_pe_v': normal((NSA_BLOCK, NSA_HEAD_DIM), 0.5),
        'nsa_cmp_w1_k': normal((NSA_BLOCK * NSA_HEAD_DIM, NSA_HEAD_DIM), (NSA_BLOCK * NSA_HEAD_DIM) ** -0.5),
        'nsa_cmp_w2_k': normal((NSA_HEAD_DIM, NSA_HEAD_DIM), NSA_HEAD_DIM ** -0.5),
        'nsa_cmp_w1_v': normal((NSA_BLOCK * NSA_HEAD_DIM, NSA_HEAD_DIM), (NSA_BLOCK * NSA_HEAD_DIM) ** -0.5),
        'nsa_cmp_w2_v': normal((NSA_HEAD_DIM, NSA_HEAD_DIM), NSA_HEAD_DIM ** -0.5),
        'w_out_a': normal((MIX_A_OUT, D_MODEL), MIX_A_OUT ** -0.5),
        'w_in_c': normal((D_MODEL, IN_C), D_MODEL ** -0.5),
        'diff_lambda_q1': normal((DIFF_HEAD_DIM,), 0.1),
        'diff_lambda_k1': normal((DIFF_HEAD_DIM,), 0.1),
        'diff_lambda_q2': normal((DIFF_HEAD_DIM,), 0.1),
        'diff_lambda_k2': normal((DIFF_HEAD_DIM,), 0.1),
        'diff_subln': gain((2 * DIFF_HEAD_DIM,)),
        'w_out_c': normal((MIX_C_OUT, D_MODEL), MIX_C_OUT ** -0.5),
        'w_ffn_gate': normal((DEPTH, D_MODEL, D_FF), D_MODEL ** -0.5),
        'w_ffn_up': normal((DEPTH, D_MODEL, D_FF), D_MODEL ** -0.5),
        'w_ffn_down': normal((DEPTH, D_FF, D_MODEL), D_FF ** -0.5),
    }


def reference(x_prompt, x_sample, cache_mla, cache_nsa_cmp, cache_nsa_slc, state_nsa_win, cache_diff, page_table,
              attn_norm, ffn_norm, final_norm,
              w_in_a, mla_q_norm, mla_kv_norm, w_mla_uq, w_mla_uk, w_mla_uv,
              nsa_pe_k, nsa_pe_v, nsa_cmp_w1_k, nsa_cmp_w2_k, nsa_cmp_w1_v, nsa_cmp_w2_v, w_out_a,
              w_in_c, diff_lambda_q1, diff_lambda_k1, diff_lambda_q2, diff_lambda_k2, diff_subln, w_out_c,
              w_ffn_gate, w_ffn_up, w_ffn_down):
    pa = {'w_in': w_in_a, 'mla_q_norm': mla_q_norm, 'mla_kv_norm': mla_kv_norm, 'mla_uq': w_mla_uq,
          'mla_uk': w_mla_uk, 'mla_uv': w_mla_uv, 'pe_k': nsa_pe_k, 'pe_v': nsa_pe_v,
          'cmp_w1_k': nsa_cmp_w1_k, 'cmp_w2_k': nsa_cmp_w2_k, 'cmp_w1_v': nsa_cmp_w1_v, 'cmp_w2_v': nsa_cmp_w2_v,
          'w_out': w_out_a}
    pc = {'w_in': w_in_c, 'lq1': diff_lambda_q1, 'lk1': diff_lambda_k1, 'lq2': diff_lambda_q2,
          'lk2': diff_lambda_k2, 'subln': diff_subln, 'w_out': w_out_c}
    pos_p = jnp.arange(x_prompt.shape[1], dtype=jnp.int32)
    pos_s = page_table.shape[1] * PAGE_SIZE + jnp.arange(x_sample.shape[1], dtype=jnp.int32)
    hp, hs = x_prompt, x_sample
    for layer in range(DEPTH):
        hp_in = rms_norm(hp, attn_norm[layer])
        hs_in = rms_norm(hs, attn_norm[layer])
        if layer % 2 == 0:
            mp, (p_mla, p_nsa_cmp, p_nsa_slc, p_nsa_win) = even_prompt(hp_in, pos_p, pa)
            ms, (s_mla, s_nsa_cmp, s_nsa_slc, s_nsa_win) = even_sample(
                hs_in, pos_s, cache_mla, cache_nsa_cmp, cache_nsa_slc, state_nsa_win, page_table, pa)
        else:
            mp, p_diff = odd_prompt(hp_in, pos_p, pc)
            ms, s_diff = odd_sample(hs_in, pos_s, cache_diff, page_table, pc)
        hp = hp + mp
        hs = hs + ms
        hp = hp + swiglu(rms_norm(hp, ffn_norm[layer]), w_ffn_gate[layer], w_ffn_up[layer], w_ffn_down[layer])
        hs = hs + swiglu(rms_norm(hs, ffn_norm[layer]), w_ffn_gate[layer], w_ffn_up[layer], w_ffn_down[layer])
    y_prompt = rms_norm(hp, final_norm)
    y_sample = rms_norm(hs, final_norm)
    return (y_prompt, y_sample, p_mla, p_nsa_cmp, p_nsa_slc, p_nsa_win, p_diff,
            s_mla, s_nsa_cmp, s_nsa_slc, s_nsa_win, s_diff)
```

```python
import functools
import math

import numpy as np
import jax
import jax.numpy as jnp
from jax import lax
from jax.experimental import pallas as pl
from jax.experimental.pallas import tpu as pltpu

F32 = jnp.float32
BF16 = jnp.bfloat16

LANES = 128
VMEM_LIMIT = 56 * 1024 * 1024

ROPE_THETA = 10000.0
NORM_EPS = 1e-6
PAGE = 128
MLA_HEADS = 8
MLA_Q_LORA = 256
MLA_KV_LORA = 128
MLA_NOPE = 64
MLA_ROPE = 32
MLA_V = 64
MLA_SCALE = (MLA_NOPE + MLA_ROPE) ** -0.5
NSA_HEADS = 8
NSA_D = 64
NSA_BLOCK = 64
NSA_TOP_N = 16
NSA_WINDOW = 512
NSA_SCALE = NSA_D ** -0.5
NSA_FORCED = 1.0e4
DIFF_HEADS = 8
DIFF_KVH = 2
DIFF_D = 64
DIFF_GROUP = DIFF_HEADS // DIFF_KVH
DIFF_SCALE = DIFF_D ** -0.5
DIFF_LAMBDA_INIT = 0.8 - 0.6 * math.exp(-0.3 * 1)

NEG_BIG = -1.0e30
SEL_NEG = -float(2 ** 40)


def _cparams(sem):
    return pltpu.CompilerParams(dimension_semantics=sem, vmem_limit_bytes=VMEM_LIMIT)


def _rms(x, g):
    return x * lax.rsqrt(jnp.mean(x * x, axis=-1, keepdims=True) + NORM_EPS) * g


def _dot(a, b):
    return jnp.dot(a, b, preferred_element_type=F32)


def _dot_nt(a, b):
    return lax.dot_general(a, b, (((1,), (1,)), ((), ())), preferred_element_type=F32)


def _lane(shape):
    return lax.broadcasted_iota(jnp.int32, shape, len(shape) - 1)


_S_CQ, _S_CKV, _S_KR, _S_KR_ROT, _S_QN, _S_QN_ROT = 0, 2, 3, 4, 5, 9
_S_CMP, _S_SLC, _S_SLC_ROT, _S_WIN, _S_WIN_ROT, _S_GATE, _N_SLOTS0 = 13, 14, 15, 16, 17, 18, 19


def _proj0_kernel(x_ref, g_ref, w_ref, qn_ref, kvn_ref, uq_ref, ukv_ref, c32_ref, s32_ref, cn_ref, sn_ref,
                  *out_refs, prompt, tm):
    if prompt:
        (mla_ref, cmp_ref, slc_ref, win_ref, qm_ref, qc_ref, qr_ref, gate_ref,
         km_ref, vm_ref, ksa_ref, vs_ref, kw_ref, vw_ref) = out_refs
    else:
        (mla_ref, cmp_ref, slc_ref, win_ref, qm_ref, qc_ref, qr_ref, gate_ref, qa_ref) = out_refs

    hn = _rms(x_ref[...], g_ref[...]).astype(BF16)
    z = _dot(hn, w_ref[...])

    def slot(s, n=1):
        return z[:, s * LANES:(s + n) * LANES]

    c32, s32, cn, sn = c32_ref[...], s32_ref[...], cn_ref[...], sn_ref[...]
    lane = _lane((tm, LANES))
    low = lane < NSA_D

    cqn = _rms(slot(_S_CQ, 2), qn_ref[...]).astype(BF16)
    zq = _dot(cqn, uq_ref[...])
    ckvn = _rms(slot(_S_CKV), kvn_ref[...])
    kpe_hi = slot(_S_KR) * c32 + slot(_S_KR_ROT) * s32
    mla_ref[:, 0:MLA_KV_LORA] = ckvn
    mla_ref[:, MLA_KV_LORA:MLA_KV_LORA + MLA_ROPE] = pltpu.roll(kpe_hi, 64, axis=1)[:, 0:MLA_ROPE]
    ckvb = ckvn.astype(BF16)
    q_cos = jnp.where(low, 1.0, c32)
    nh = MLA_HEADS
    for h in range(nh):
        qh = (zq[:, h * LANES:(h + 1) * LANES] * q_cos
              + zq[:, (nh + h) * LANES:(nh + h + 1) * LANES] * s32) * MLA_SCALE
        qm_ref[h] = qh.astype(BF16)
        if not prompt:
            qlat = _dot(qh.astype(BF16), ukv_ref[h])
            qa_ref[h, :, 0:LANES] = qlat.astype(BF16)
            qa_ref[h, :, LANES:2 * LANES] = jnp.where(lane < MLA_ROPE, pltpu.roll(qh, 64, axis=1), 0.0).astype(BF16)
    if prompt:
        zk = _dot(ckvb, ukv_ref[0])
        zv = _dot(ckvb, ukv_ref[1])
        for h in range(nh):
            km_ref[h] = (zk[:, h * LANES:(h + 1) * LANES] + kpe_hi).astype(BF16)
            vm_ref[h] = zv[:, h * LANES:(h + 1) * LANES].astype(BF16)

    for p in range(NSA_HEADS // 2):
        plain = slot(_S_QN + p) * NSA_SCALE
        roped = (slot(_S_QN + p) * cn + slot(_S_QN_ROT + p) * sn) * NSA_SCALE
        qc_ref[2 * p] = jnp.where(low, plain, 0.0).astype(BF16)
        qc_ref[2 * p + 1] = jnp.where(low, 0.0, plain).astype(BF16)
        qr_ref[2 * p] = jnp.where(low, roped, 0.0).astype(BF16)
        qr_ref[2 * p + 1] = jnp.where(low, 0.0, roped).astype(BF16)

    ck = jnp.where(low, cn, 1.0)
    cmp_ref[...] = slot(_S_CMP)
    slc = slot(_S_SLC) * ck + slot(_S_SLC_ROT) * sn
    win = slot(_S_WIN) * ck + slot(_S_WIN_ROT) * sn
    slc_ref[...] = slc
    win_ref[...] = win
    gate_ref[...] = jax.nn.sigmoid(slot(_S_GATE))
    if prompt:
        slc_sw = pltpu.roll(slc, 64, axis=1)
        win_sw = pltpu.roll(win, 64, axis=1)
        row = pl.program_id(0) * tm + lax.broadcasted_iota(jnp.int32, (tm, LANES), 0)
        onehot = ((row >> 6) & (LANES - 1)) == lane
        ksa_ref[:, 0:LANES] = jnp.where(onehot, 1.0, 0.0).astype(BF16)
        ksa_ref[:, LANES:2 * LANES] = jnp.where(low, slc, slc_sw).astype(BF16)
        vs_ref[...] = slc_sw.astype(BF16)
        kw_ref[...] = jnp.where(low, win, win_sw).astype(BF16)
        vw_ref[...] = win_sw.astype(BF16)


def _rot_cols(w, d):
    k = w.shape[0]
    w3 = w.reshape(k, -1, d)
    return jnp.concatenate([-w3[..., d // 2:], w3[..., :d // 2]], axis=-1).reshape(k, -1)


def _place(w, left, width=LANES):
    return jnp.pad(w, ((0, 0), (left, width - left - w.shape[1])))


def _prep_layer0(w_in_a, w_mla_uq, w_mla_uk, w_mla_uv):
    o = np.cumsum([0, 256, 128, 32, 512, 64, 64, 64, 64, 64, 64, 24]).tolist()
    c_q, c_kv, k_r, q_n = (w_in_a[:, o[i]:o[i + 1]] for i in range(4))
    kv_c = w_in_a[:, o[4]:o[6]]
    kv_s = w_in_a[:, o[6]:o[8]]
    kv_w = w_in_a[:, o[8]:o[10]]
    g = w_in_a[:, o[10]:o[11]]
    z64 = jnp.zeros((w_in_a.shape[0], 64), F32)
    cols = [c_q, c_kv, _place(k_r, 64), _place(_rot_cols(k_r, MLA_ROPE), 64), q_n, _rot_cols(q_n, NSA_D),
            kv_c,
            kv_s, jnp.concatenate([_rot_cols(kv_s[:, :64], 64), z64], axis=1),
            kv_w, jnp.concatenate([_rot_cols(kv_w[:, :64], 64), z64], axis=1),
            _place(g, 0)]
    w0 = jnp.concatenate(cols, axis=1).astype(BF16)
    uq3 = w_mla_uq.reshape(MLA_Q_LORA, MLA_HEADS, MLA_NOPE + MLA_ROPE)
    nope, pe = uq3[..., :MLA_NOPE], uq3[..., MLA_NOPE:]
    pe_rot = jnp.concatenate([-pe[..., MLA_ROPE // 2:], pe[..., :MLA_ROPE // 2]], axis=-1)
    pad32 = jnp.zeros(pe.shape, F32)
    plain = jnp.concatenate([nope, pe, pad32], axis=-1).reshape(MLA_Q_LORA, -1)
    rot = jnp.concatenate([jnp.zeros(nope.shape, F32), pe_rot, pad32], axis=-1).reshape(MLA_Q_LORA, -1)
    uq = jnp.concatenate([plain, rot], axis=1).astype(BF16)
    pad_k = jnp.zeros((MLA_KV_LORA, MLA_HEADS, LANES - MLA_NOPE), F32)
    uk_slots = jnp.concatenate([w_mla_uk, pad_k], axis=-1).reshape(MLA_KV_LORA, -1)
    uv_slots = jnp.concatenate([w_mla_uv, pad_k], axis=-1).reshape(MLA_KV_LORA, -1)
    ukv_prompt = jnp.stack([uk_slots, uv_slots]).astype(BF16)
    ukt = jnp.transpose(w_mla_uk, (1, 2, 0))
    ukt = jnp.concatenate([ukt, jnp.zeros_like(ukt)], axis=1).astype(BF16)
    uv_pad = jnp.transpose(jnp.concatenate([w_mla_uv, pad_k], axis=-1), (1, 0, 2)).astype(BF16)
    return w0, uq, ukv_prompt, ukt, uv_pad


def _rope_tables(pos):
    def cs(dim):
        inv = ROPE_THETA ** (-jnp.arange(0, dim, 2, dtype=F32) / dim)
        ang = pos.astype(F32)[:, None] * inv[None, :]
        ang = jnp.concatenate([ang, ang], axis=-1)
        return jnp.cos(ang), jnp.sin(ang)
    c32, s32 = cs(MLA_ROPE)
    cn, sn = cs(NSA_D)
    return (_place(c32, 64), _place(s32, 64), jnp.concatenate([cn, cn], axis=1), jnp.concatenate([sn, sn], axis=1))


def _proj0(x, gain, w0, qnorm, kvnorm, uq, ukv, tables, prompt):
    t, d = x.shape
    tm = min(t, 256)
    nh = MLA_HEADS
    row = lambda w: pl.BlockSpec((tm, w), lambda i: (i, 0))
    head = lambda w: pl.BlockSpec((nh, tm, w), lambda i: (0, i, 0))
    full = lambda a: pl.BlockSpec(a.shape, lambda i: (0,) * a.ndim)
    out_shape = [jax.ShapeDtypeStruct((t, MLA_KV_LORA + MLA_ROPE), F32)] + [jax.ShapeDtypeStruct((t, LANES), F32)] * 3
    out_specs = [row(MLA_KV_LORA + MLA_ROPE), row(LANES), row(LANES), row(LANES)]
    out_shape += [jax.ShapeDtypeStruct((nh, t, LANES), BF16)] * 3 + [jax.ShapeDtypeStruct((t, LANES), F32)]
    out_specs += [head(LANES)] * 3 + [row(LANES)]
    if prompt:
        out_shape += [jax.ShapeDtypeStruct((nh, t, LANES), BF16)] * 2
        out_specs += [head(LANES)] * 2
        out_shape += [jax.ShapeDtypeStruct((t, 2 * LANES), BF16)] + [jax.ShapeDtypeStruct((t, LANES), BF16)] * 3
        out_specs += [row(2 * LANES), row(LANES), row(LANES), row(LANES)]
    else:
        out_shape += [jax.ShapeDtypeStruct((nh, t, 2 * LANES), BF16)]
        out_specs += [head(2 * LANES)]
    return pl.pallas_call(
        functools.partial(_proj0_kernel, prompt=prompt, tm=tm),
        grid=(t // tm,),
        in_specs=[row(d), full(gain), full(w0), full(qnorm), full(kvnorm), full(uq), full(ukv)] + [row(LANES)] * 4,
        out_specs=out_specs, out_shape=out_shape,
        compiler_params=_cparams(("parallel",)),
        name="proj0_prompt" if prompt else "proj0_sample",
    )(x, gain, w0, qnorm, kvnorm, uq, ukv, *tables)


_CMP_PAIR = 2


def _compress_rows(src_ref, n_blocks, pe_ref, w1_ref):
    acc = None
    for r0 in range(0, NSA_BLOCK, _CMP_PAIR):
        xs = [src_ref[pl.ds(r0 + u, n_blocks, stride=NSA_BLOCK), :] + pe_ref[r0 + u:r0 + u + 1, :]
              for u in range(_CMP_PAIR)]
        part = _dot(jnp.concatenate(xs, axis=1).astype(BF16), w1_ref[r0 // _CMP_PAIR])
        acc = part if acc is None else acc + part
    return acc


def _compress_prompt_kernel(f_ref, pe_ref, w1_ref, w2_ref, o_ref, *, n_blocks):
    pre = _compress_rows(f_ref, n_blocks, pe_ref, w1_ref)
    o_ref[...] = _dot(jax.nn.silu(pre).astype(BF16), w2_ref[...])


def _compress_paged_kernel(pt_ref, *refs, pages, n_blocks):
    page_refs = refs[:pages]
    pe_ref, w1_ref, w2_ref, o_ref, stage, pre = refs[pages:]
    ch = pl.program_id(1)
    for p in range(pages):
        stage[PAGE * p:PAGE * (p + 1), :] = page_refs[p][0]
    nb = pages * PAGE // NSA_BLOCK
    part = _compress_rows(stage, nb, pe_ref, w1_ref)
    pre[pl.ds(pl.multiple_of(ch * nb, nb), nb), :] = part

    @pl.when(ch == pl.num_programs(1) - 1)
    def _():
        o_ref[0] = _dot(jax.nn.silu(pre[...]).astype(BF16), w2_ref[...])


def _prep_compress(pe_k, pe_v, w1_k, w1_v, w2_k, w2_v):
    pe = jnp.concatenate([pe_k, pe_v], axis=1)
    w1 = jnp.zeros((NSA_BLOCK, 2, NSA_D, 2 * NSA_D), F32)
    w1 = w1.at[:, 0, :, :NSA_D].set(w1_k.reshape(NSA_BLOCK, NSA_D, NSA_D))
    w1 = w1.at[:, 1, :, NSA_D:].set(w1_v.reshape(NSA_BLOCK, NSA_D, NSA_D))
    w1 = w1.reshape(NSA_BLOCK // _CMP_PAIR, _CMP_PAIR * LANES, 2 * NSA_D).astype(BF16)
    w2 = jnp.zeros((2 * NSA_D, 2 * NSA_D), F32)
    w2 = w2.at[:NSA_D, :NSA_D].set(w2_k).at[NSA_D:, NSA_D:].set(w2_v).astype(BF16)
    return pe, w1, w2


def _compress_prompt(cmp_rows, pe, w1, w2):
    t = cmp_rows.shape[0]
    nb = t // NSA_BLOCK
    full = lambda a: pl.BlockSpec(a.shape, lambda i: (0,) * a.ndim)
    return pl.pallas_call(
        functools.partial(_compress_prompt_kernel, n_blocks=nb),
        grid=(1,), in_specs=[full(cmp_rows), full(pe), full(w1), full(w2)],
        out_specs=pl.BlockSpec((nb, LANES), lambda i: (0, 0)),
        out_shape=jax.ShapeDtypeStruct((nb, LANES), F32),
        compiler_params=_cparams(("arbitrary",)), name="compress_prompt",
    )(cmp_rows, pe, w1, w2)


def _compress_paged(cache_cmp, page_table, pe, w1, w2, pages):
    n_pool = cache_cmp.shape[0]
    b, n_pages = page_table.shape
    nb = n_pages * PAGE // NSA_BLOCK
    view = cache_cmp.reshape(n_pool, PAGE, LANES)
    pages = min(pages, n_pages)
    page_specs = [pl.BlockSpec((1, PAGE, LANES),
                               functools.partial(lambda i, c, pt, p: (pt[i, c * pages + p], 0, 0), p=p))
                  for p in range(pages)]
    full = lambda a: pl.BlockSpec(a.shape, lambda i, c, pt: (0,) * a.ndim)
    return pl.pallas_call(
        functools.partial(_compress_paged_kernel, pages=pages, n_blocks=nb),
        grid_spec=pltpu.PrefetchScalarGridSpec(
            num_scalar_prefetch=1, grid=(b, n_pages // pages),
            in_specs=page_specs + [full(pe), full(w1), full(w2)],
            out_specs=pl.BlockSpec((1, nb, LANES), lambda i, c, pt: (i, 0, 0)),
            scratch_shapes=[pltpu.VMEM((PAGE * pages, LANES), F32), pltpu.VMEM((nb, LANES), F32)]),
        out_shape=jax.ShapeDtypeStruct((b, nb, LANES), F32),
        compiler_params=_cparams(("parallel", "arbitrary")), name="compress_paged",
    )(page_table, *([view] * pages), pe, w1, w2)


def _cmp_attn_kernel(kv_ref, q_ref, o_ref, imp_ref, *, rows_per_query, tq, pos0, pos_step, nbp):
    kv = kv_ref[...] if len(kv_ref.shape) == 2 else kv_ref[0]
    nb = kv.shape[0]
    sw = pltpu.roll(kv, 64, axis=1)
    kdup = jnp.where(_lane(kv.shape) < NSA_D, kv, sw).astype(BF16)
    vk = sw.astype(BF16)
    if rows_per_query:
        q = q_ref[0]
        qpos = jnp.full((q.shape[0], 1), pos0, jnp.int32)
    else:
        q = q_ref[...].reshape(NSA_HEADS * tq, LANES)
        r = lax.broadcasted_iota(jnp.int32, (NSA_HEADS * tq, 1), 0)
        qpos = pos0 + pos_step * (pl.program_id(0) * tq + (r & (tq - 1)))
    s = _dot_nt(q, kdup)
    last = (_lane(s.shape) + 1) * NSA_BLOCK - 1
    s = jnp.where(last <= qpos, s, -jnp.inf)
    m = jnp.max(s, axis=-1, keepdims=True)
    m = jnp.where(m == -jnp.inf, 0.0, m)
    e = jnp.exp(s - m)
    den = jnp.sum(e, axis=-1, keepdims=True)
    p = e / jnp.where(den > 0.0, den, 1.0)
    o = _dot(p.astype(BF16), vk)
    if rows_per_query:
        o_ref[0] = o
        imp = jnp.sum(p, axis=0, keepdims=True)
        if nbp > nb:
            imp = jnp.concatenate([imp, jnp.zeros((1, nbp - nb), F32)], axis=1)
        imp_ref[0] = imp
    else:
        o_ref[...] = o.reshape(NSA_HEADS, tq, LANES)
        imp = jnp.sum(p.reshape(NSA_HEADS, tq, nb), axis=0)
        if nbp > nb:
            imp = jnp.concatenate([imp, jnp.zeros((tq, nbp - nb), F32)], axis=1)
        imp_ref[...] = imp


def _cmp_attn_prompt(kcvc, q_cmp, nbp):
    nh, t, _ = q_cmp.shape
    nb = kcvc.shape[0]
    tq = min(t, 128)
    return pl.pallas_call(
        functools.partial(_cmp_attn_kernel, rows_per_query=False, tq=tq, pos0=0, pos_step=1, nbp=nbp),
        grid=(t // tq,),
        in_specs=[pl.BlockSpec((nb, LANES), lambda i: (0, 0)), pl.BlockSpec((nh, tq, LANES), lambda i: (0, i, 0))],
        out_specs=[pl.BlockSpec((nh, tq, LANES), lambda i: (0, i, 0)), pl.BlockSpec((tq, nbp), lambda i: (i, 0))],
        out_shape=[jax.ShapeDtypeStruct((nh, t, LANES), F32), jax.ShapeDtypeStruct((t, nbp), F32)],
        compiler_params=_cparams(("parallel",)), name="cmp_attn_prompt",
    )(kcvc, q_cmp)


def _cmp_attn_sample(kcvc, q_cmp, pos, nbp):
    b, nh, _ = q_cmp.shape
    nb = kcvc.shape[1]
    return pl.pallas_call(
        functools.partial(_cmp_attn_kernel, rows_per_query=True, tq=1, pos0=pos, pos_step=0, nbp=nbp),
        grid=(b,),
        in_specs=[pl.BlockSpec((1, nb, LANES), lambda i: (i, 0, 0)), pl.BlockSpec((1, nh, LANES), lambda i: (i, 0, 0))],
        out_specs=[pl.BlockSpec((1, nh, LANES), lambda i: (i, 0, 0)), pl.BlockSpec((1, 1, nbp), lambda i: (i, 0, 0))],
        out_shape=[jax.ShapeDtypeStruct((b, nh, LANES), F32), jax.ShapeDtypeStruct((b, 1, nbp), F32)],
        compiler_params=_cparams(("parallel",)), name="cmp_attn_sample",
    )(kcvc, q_cmp)


def _select_kernel(imp_ref, *refs, tq, pos0, pos_step, nb_sel, n_halves, with_query):
    if with_query:
        q_ref, qa_ref = refs
    else:
        idx_ref, val_ref = refs
    imp = imp_ref[...]
    nbp = imp.shape[1]
    lane = _lane(imp.shape)
    qpos = pos0 + pos_step * (pl.program_id(0) * tq + lax.broadcasted_iota(jnp.int32, (tq, 1), 0))
    cur = qpos >> 6
    forced = (lane == 0) | (lane == cur) | (lane == cur - 1)
    score = jnp.where(forced, NSA_FORCED, imp)
    work = jnp.where((lane <= cur) & (lane < nb_sel), score, -jnp.inf)
    lane_f = lane.astype(F32)
    neg = jnp.full(imp.shape, SEL_NEG, F32)
    lane16 = _lane((tq, LANES))
    idx_acc = jnp.zeros((tq, LANES), jnp.int32)
    val_acc = jnp.zeros((tq, LANES), jnp.int32)
    for n in range(min(NSA_TOP_N, nb_sel)):
        mx = jnp.max(work, axis=-1, keepdims=True)
        first = jnp.min(jnp.where(work == mx, lane_f, float(nbp)), axis=-1, keepdims=True)
        pick = lane_f == first
        valid_neg = jnp.where(mx > -jnp.inf, 0.0, SEL_NEG)
        neg = jnp.where(pick, jnp.maximum(neg, valid_neg), neg)
        work = jnp.where(pick, -jnp.inf, work)
        if not with_query:
            idx_acc = jnp.where(lane16 == n, first.astype(jnp.int32), idx_acc)
            val_acc = jnp.where(lane16 == n, jnp.where(mx > -jnp.inf, 1, 0), val_acc)
    if with_query:
        neg = neg.astype(BF16)
        for hf in range(n_halves):
            part = neg[:, hf * LANES:(hf + 1) * LANES]
            for h in range(NSA_HEADS):
                qa_ref[hf, h, :, 0:LANES] = part
                qa_ref[hf, h, :, LANES:2 * LANES] = q_ref[h]
    else:
        idx_ref[...] = idx_acc
        val_ref[...] = val_acc


def _select_prompt(imp, q_rot):
    t, nbp = imp.shape
    nh = q_rot.shape[0]
    tq = min(t, 128)
    n_halves = nbp // LANES
    return pl.pallas_call(
        functools.partial(_select_kernel, tq=tq, pos0=0, pos_step=1, nb_sel=t // NSA_BLOCK, n_halves=n_halves,
                          with_query=True),
        grid=(t // tq,),
        in_specs=[pl.BlockSpec((tq, nbp), lambda i: (i, 0)), pl.BlockSpec((nh, tq, LANES), lambda i: (0, i, 0))],
        out_specs=pl.BlockSpec((n_halves, nh, tq, 2 * LANES), lambda i: (0, 0, i, 0)),
        out_shape=jax.ShapeDtypeStruct((n_halves, nh, t, 2 * LANES), BF16),
        compiler_params=_cparams(("parallel",)), name="select_prompt",
    )(imp, q_rot)


def _select_sample(imp, pos, nb_sel):
    b, nbp = imp.shape
    return pl.pallas_call(
        functools.partial(_select_kernel, tq=b, pos0=pos, pos_step=0, nb_sel=nb_sel, n_halves=0, with_query=False),
        grid=(1,),
        in_specs=[pl.BlockSpec((b, nbp), lambda i: (0, 0))],
        out_specs=[pl.BlockSpec((b, LANES), lambda i: (0, 0))] * 2,
        out_shape=[jax.ShapeDtypeStruct((b, LANES), jnp.int32)] * 2,
        compiler_params=_cparams(("arbitrary",)), name="select_sample",
    )(imp)


def _flash_kernel(q_ref, k_ref, v_ref, o_ref, m_scr, l_scr, acc_scr, *, r, tq, tk, window, n_j):
    i, j = pl.program_id(1), pl.program_id(2)
    m_rows = r * tq

    @pl.when(j == 0)
    def _():
        m_scr[...] = jnp.full(m_scr.shape, NEG_BIG, F32)
        l_scr[...] = jnp.zeros(l_scr.shape, F32)
        acc_scr[...] = jnp.zeros(acc_scr.shape, F32)

    if window:
        kv_blk = i - (n_j - 1) + j
        active = kv_blk >= 0
        need_mask = None
    else:
        kv_blk = j
        active = j * tk <= i * tq + (tq - 1)
        need_mask = j * tk + (tk - 1) > i * tq

    def step(masked):
        q = q_ref[0].reshape(m_rows, q_ref.shape[-1])
        s = _dot_nt(q, k_ref[0])
        if masked:
            qpos = i * tq + (lax.broadcasted_iota(jnp.int32, (m_rows, 1), 0) & (tq - 1))
            kpos = kv_blk * tk + _lane((m_rows, tk))
            ok = kpos <= qpos
            if window:
                ok = ok & (qpos - kpos < window)
            s = jnp.where(ok, s, NEG_BIG)
        m_prev = m_scr[...]
        m_new = jnp.maximum(m_prev, jnp.max(s, axis=-1, keepdims=True))
        alpha = jnp.exp(m_prev - m_new)
        p = jnp.exp(s - m_new)
        l_scr[...] = alpha * l_scr[...] + jnp.sum(p, axis=-1, keepdims=True)
        acc_scr[...] = alpha * acc_scr[...] + _dot(p.astype(BF16), v_ref[0])
        m_scr[...] = m_new

    if window:
        pl.when(active)(lambda: step(True))
    else:
        pl.when(active & need_mask)(lambda: step(True))
        pl.when(active & jnp.logical_not(need_mask))(lambda: step(False))

    @pl.when(j == n_j - 1)
    def _():
        o = acc_scr[...] / l_scr[...]
        o_ref[0] = o.reshape(r, tq, o.shape[-1]).astype(o_ref.dtype)


def _flash(q, k, v, *, tq, tk, groups, q_index, kv_index, window=0, out_dtype=F32, name):
    r, t, dq = q.shape[1:]
    dv = v.shape[-1]
    tq, tk = min(tq, t), min(tk, t)
    if window:
        assert tq == tk and window % tk == 0
        n_j = window // tk + 1
        kv_blk = lambda i, j: jnp.maximum(i - (n_j - 1) + j, 0)
    else:
        n_j = t // tk
        kv_blk = lambda i, j: jnp.minimum(j, (i * tq + tq - 1) // tk)
    return pl.pallas_call(
        functools.partial(_flash_kernel, r=r, tq=tq, tk=tk, window=window, n_j=n_j),
        grid=(groups, t // tq, n_j),
        in_specs=[pl.BlockSpec((1, r, tq, dq), lambda g, i, j: (q_index(g, kv_blk(i, j) * tk), 0, i, 0)),
                  pl.BlockSpec((1, tk, dq), lambda g, i, j: (kv_index(g), kv_blk(i, j), 0)),
                  pl.BlockSpec((1, tk, dv), lambda g, i, j: (kv_index(g), kv_blk(i, j), 0))],
        out_specs=pl.BlockSpec((1, r, tq, dv), lambda g, i, j: (g, 0, i, 0)),
        out_shape=jax.ShapeDtypeStruct((groups, r, t, dv), out_dtype),
        scratch_shapes=[pltpu.VMEM((r * tq, 1), F32), pltpu.VMEM((r * tq, 1), F32), pltpu.VMEM((r * tq, dv), F32)],
        compiler_params=_cparams(("parallel", "parallel", "arbitrary")), name=name,
    )(q, k, v)


def _decode_kernel(pt_ref, q_ref, new_ref, *refs, pages, groups):
    page_refs = refs[:pages]
    o_ref, m_scr, l_scr, acc_scr = refs[pages:]
    c = pl.program_id(1)
    for gi, (koff, kw, voff, vw) in enumerate(groups):
        q = q_ref[0, gi]

        @pl.when(c == 0)
        def _():
            knew = new_ref[0, :, koff:koff + kw].astype(BF16).astype(F32)
            vnew = new_ref[0, :, voff:voff + vw].astype(BF16).astype(F32)
            m_scr[gi] = jnp.sum(q.astype(F32) * knew, axis=-1, keepdims=True)
            l_scr[gi] = jnp.ones(l_scr.shape[1:], F32)
            acc_scr[gi] = jnp.broadcast_to(vnew, acc_scr.shape[1:])

        kk = jnp.concatenate([pr[0, :, koff:koff + kw] for pr in page_refs], axis=0).astype(BF16)
        vv = jnp.concatenate([pr[0, :, voff:voff + vw] for pr in page_refs], axis=0).astype(BF16)
        s = _dot_nt(q, kk)
        m_prev = m_scr[gi]
        m_new = jnp.maximum(m_prev, jnp.max(s, axis=-1, keepdims=True))
        alpha = jnp.exp(m_prev - m_new)
        p = jnp.exp(s - m_new)
        l_scr[gi] = alpha * l_scr[gi] + jnp.sum(p, axis=-1, keepdims=True)
        acc_scr[gi] = alpha * acc_scr[gi] + _dot(p.astype(BF16), vv)
        m_scr[gi] = m_new

    @pl.when(c == pl.num_programs(1) - 1)
    def _():
        for gi in range(len(groups)):
            o_ref[0, gi] = acc_scr[gi] / l_scr[gi]


def _decode(q, new_rows, cache2d, page_table, groups, pages, name):
    b, ng, rows, kw = q.shape
    w = cache2d.shape[-1]
    n_pages = page_table.shape[1]
    pages = min(pages, n_pages)
    vw = groups[0][3]
    page_specs = [pl.BlockSpec((1, PAGE, w), functools.partial(lambda i, c, pt, p: (pt[i, c * pages + p], 0, 0), p=p))
                  for p in range(pages)]
    return pl.pallas_call(
        functools.partial(_decode_kernel, pages=pages, groups=groups),
        grid_spec=pltpu.PrefetchScalarGridSpec(
            num_scalar_prefetch=1, grid=(b, n_pages // pages),
            in_specs=[pl.BlockSpec((1, ng, rows, kw), lambda i, c, pt: (i, 0, 0, 0)),
                      pl.BlockSpec((1, 1, w), lambda i, c, pt: (i, 0, 0))] + page_specs,
            out_specs=pl.BlockSpec((1, ng, rows, vw), lambda i, c, pt: (i, 0, 0, 0)),
            scratch_shapes=[pltpu.VMEM((ng, rows, 1), F32), pltpu.VMEM((ng, rows, 1), F32),
                            pltpu.VMEM((ng, rows, vw), F32)]),
        out_shape=jax.ShapeDtypeStruct((b, ng, rows, vw), F32),
        compiler_params=_cparams(("parallel", "arbitrary")), name=name,
    )(page_table, q, new_rows, *([cache2d] * pages))


def _nsa_decode_kernel(idx_ref, val_ref, pt_ref, q_ref, news_ref, neww_ref, win_ref, *refs, n_sel, nb_cached, pos,
                       win_pos0):
    blk_refs = refs[:n_sel]
    os_ref, ow_ref = refs[n_sel:]
    b = pl.program_id(0)
    q = q_ref[0]
    qf = q.astype(F32)
    head_low = (lax.broadcasted_iota(jnp.int32, (NSA_HEADS, 1), 0) & 1) == 0

    def new_key_score(new_ref):
        row = new_ref[0].astype(BF16).astype(F32)
        kd = jnp.where(_lane(row.shape) < NSA_D, row, pltpu.roll(row, 64, axis=1))
        vnew = jnp.where(_lane(row.shape) < NSA_D, pltpu.roll(row, 64, axis=1), 0.0)
        return jnp.sum(qf * kd, axis=-1, keepdims=True), vnew

    def attend(s, ok, kv, s_new, new_ok, vnew):
        s = jnp.where(ok, s, -jnp.inf)
        if new_ok is not True:
            s_new = jnp.where(new_ok, s_new, -jnp.inf)
        m = jnp.maximum(jnp.max(s, axis=-1, keepdims=True), s_new)
        m = jnp.where(m == -jnp.inf, 0.0, m)
        e = jnp.exp(s - m)
        e_new = jnp.exp(s_new - m)
        den = jnp.sum(e, axis=-1, keepdims=True) + e_new
        den = jnp.where(den > 0.0, den, 1.0)
        o = _dot(e.astype(BF16), kv)
        return (pltpu.roll(o, 64, axis=1) + e_new * vnew) / den

    def dup_keys(kv):
        return jnp.where(_lane(kv.shape) < NSA_D, kv, pltpu.roll(kv, 64, axis=1)).astype(BF16)

    kv = jnp.concatenate([r[0] for r in blk_refs], axis=0)
    s = _dot_nt(q, dup_keys(kv))
    blk_of = _lane(s.shape) >> 6
    ok = jnp.zeros(s.shape, F32)
    new_ok = jnp.int32(0)
    for n in range(n_sel):
        valid = val_ref[b * n_sel + n] > 0
        blk = idx_ref[b * n_sel + n]
        ok = jnp.where(blk_of == n, jnp.where(valid & (blk < nb_cached), 1.0, 0.0), ok)
        new_ok = new_ok | jnp.where(valid & (blk == nb_cached), 1, 0)
    s_new, vnew = new_key_score(news_ref)
    os_ref[0] = attend(s, ok > 0.0, kv.astype(BF16), s_new, new_ok > 0, vnew)

    wkv = win_ref[0]
    sw = _dot_nt(q, dup_keys(wkv))
    wpos = win_pos0 + _lane(sw.shape)
    d = pos - wpos
    okw = (d >= 0) & (d < NSA_WINDOW) & (wpos >= 0)
    sw_new, vwnew = new_key_score(neww_ref)
    ow_ref[0] = attend(sw, okw, wkv.astype(BF16), sw_new, True, vwnew)


def _nsa_decode(idx, val, page_table, q_rot, new_slc, new_win, state_win, cache_slc, pos):
    b = q_rot.shape[0]
    n_pool = cache_slc.shape[0]
    n_pages = page_table.shape[1]
    nb_cached = n_pages * PAGE // NSA_BLOCK
    n_sel = min(NSA_TOP_N, nb_cached + 1)
    wb = state_win.shape[1]
    blocks = cache_slc.reshape(n_pool * 2, NSA_BLOCK, LANES)

    def blk_index(i, idx_r, val_r, pt_r, n):
        blk = jnp.minimum(idx_r[i * n_sel + n], nb_cached - 1)
        return (pt_r[i, blk >> 1] * 2 + (blk & 1), 0, 0)

    blk_specs = [pl.BlockSpec((1, NSA_BLOCK, LANES), functools.partial(blk_index, n=n)) for n in range(n_sel)]
    per_seq = lambda shp: pl.BlockSpec((1,) + shp, lambda i, a, c, d: (i, 0, 0))
    return pl.pallas_call(
        functools.partial(_nsa_decode_kernel, n_sel=n_sel, nb_cached=nb_cached, pos=pos, win_pos0=pos - wb),
        grid_spec=pltpu.PrefetchScalarGridSpec(
            num_scalar_prefetch=3, grid=(b,),
            in_specs=[per_seq((NSA_HEADS, LANES)), per_seq((1, LANES)), per_seq((1, LANES)), per_seq((wb, LANES))]
            + blk_specs,
            out_specs=[per_seq((NSA_HEADS, LANES))] * 2),
        out_shape=[jax.ShapeDtypeStruct((b, NSA_HEADS, LANES), F32)] * 2,
        compiler_params=_cparams(("parallel",)), name="nsa_decode",
    )(idx[:, :n_sel].reshape(-1), val[:, :n_sel].reshape(-1), page_table, q_rot, new_slc, new_win, state_win, *([blocks] * n_sel))


def _mix0_kernel(x_ref, om_ref, oc_ref, os_ref, ow_ref, g_ref, w_ref, *refs, absorbed):
    if absorbed:
        uv_ref, o_ref = refs
    else:
        (o_ref,) = refs
    gates = g_ref[...]
    acc = x_ref[...]
    for h in range(MLA_HEADS):
        om = om_ref[h]
        if absorbed:
            om = _dot(om.astype(BF16), uv_ref[h])
        acc = acc + _dot(om.astype(BF16), w_ref[h])
    for h in range(NSA_HEADS):
        gc, gs, gw = (gates[:, k * NSA_HEADS + h:k * NSA_HEADS + h + 1] for k in range(3))
        on = gc * oc_ref[h] + gs * os_ref[h] + gw * ow_ref[h]
        acc = acc + _dot(on.astype(BF16), w_ref[MLA_HEADS + h])
    o_ref[...] = acc


def _mix0(x, o_mla, o_c, o_s, o_w, gates, w_out, uv=None):
    t, d = x.shape
    tm = min(t, 512)
    row = lambda w: pl.BlockSpec((tm, w), lambda i: (i, 0))
    head = pl.BlockSpec((NSA_HEADS, tm, LANES), lambda i: (0, i, 0))
    full = lambda a: pl.BlockSpec(a.shape, lambda i: (0,) * a.ndim)
    ins = [x, o_mla, o_c, o_s, o_w, gates, w_out] + ([uv] if uv is not None else [])
    specs = [row(d), head, head, head, head, row(LANES), full(w_out)] + ([full(uv)] if uv is not None else [])
    return pl.pallas_call(
        functools.partial(_mix0_kernel, absorbed=uv is not None),
        grid=(t // tm,), in_specs=specs, out_specs=row(d), out_shape=jax.ShapeDtypeStruct((t, d), F32),
        compiler_params=_cparams(("parallel",)), name="mix0",
    )(*ins)


def _ffn_kernel(x_ref, g_ref, wg_ref, wu_ref, wd_ref, fg_ref, o_ref, hn_scr, acc_scr, *, final_norm):
    f = pl.program_id(1)

    @pl.when(f == 0)
    def _():
        hn_scr[...] = _rms(x_ref[...], g_ref[...]).astype(BF16)
        acc_scr[...] = x_ref[...]

    hn = hn_scr[...]
    act = jax.nn.silu(_dot(hn, wg_ref[...])) * _dot(hn, wu_ref[...])
    acc_scr[...] += _dot(act.astype(BF16), wd_ref[...])

    @pl.when(f == pl.num_programs(1) - 1)
    def _():
        y = acc_scr[...]
        o_ref[...] = _rms(y, fg_ref[...]) if final_norm else y


def _ffn(x, gain, wg, wu, wd, final_gain, final_norm):
    t, d = x.shape
    dff = wg.shape[1]
    tm = min(t, 512)
    tf = 256
    return pl.pallas_call(
        functools.partial(_ffn_kernel, final_norm=final_norm),
        grid=(t // tm, dff // tf),
        in_specs=[pl.BlockSpec((tm, d), lambda i, f: (i, 0)), pl.BlockSpec((1, d), lambda i, f: (0, 0)),
                  pl.BlockSpec((d, tf), lambda i, f: (0, f)), pl.BlockSpec((d, tf), lambda i, f: (0, f)),
                  pl.BlockSpec((tf, d), lambda i, f: (f, 0)), pl.BlockSpec((1, d), lambda i, f: (0, 0))],
        out_specs=pl.BlockSpec((tm, d), lambda i, f: (i, 0)),
        out_shape=jax.ShapeDtypeStruct((t, d), F32),
        scratch_shapes=[pltpu.VMEM((tm, d), BF16), pltpu.VMEM((tm, d), F32)],
        compiler_params=_cparams(("parallel", "arbitrary")), name="ffn",
    )(x, gain, wg, wu, wd, final_gain)


_N_QSLOT = DIFF_HEADS


def _proj1_kernel(x_ref, g_ref, w_ref, cn_ref, sn_ref, rows_ref, q_ref, k_ref, v_ref, *, tm):
    hn = _rms(x_ref[...], g_ref[...]).astype(BF16)
    z = _dot(hn, w_ref[...])
    cn, sn = cn_ref[...], sn_ref[...]
    low = _lane((tm, LANES)) < DIFF_D

    def slot(s):
        return z[:, s * LANES:(s + 1) * LANES]

    for g in range(DIFF_KVH):
        for r in range(DIFF_GROUP):
            h = g * DIFF_GROUP + r
            qp = (slot(h) * cn + slot(_N_QSLOT + h) * sn) * DIFF_SCALE
            q_ref[2 * g, r] = jnp.where(low, qp, 0.0).astype(BF16)
            q_ref[2 * g + 1, r] = jnp.where(low, 0.0, qp).astype(BF16)
        k = slot(2 * _N_QSLOT + g) * cn + slot(2 * _N_QSLOT + DIFF_KVH + g) * sn
        v = slot(2 * _N_QSLOT + 2 * DIFF_KVH + g)
        rows_ref[:, g * LANES:(g + 1) * LANES] = k
        rows_ref[:, (DIFF_KVH + g) * LANES:(DIFF_KVH + g + 1) * LANES] = v
        k_ref[g] = k.astype(BF16)
        v_ref[g] = v.astype(BF16)


def _prep_layer1(w_in_c):
    nq = DIFF_HEADS * 2 * DIFF_D
    nk = DIFF_KVH * 2 * DIFF_D
    wq, wk, wv = w_in_c[:, :nq], w_in_c[:, nq:nq + nk], w_in_c[:, nq + nk:]
    return jnp.concatenate([wq, _rot_cols(wq, DIFF_D), wk, _rot_cols(wk, DIFF_D), wv], axis=1).astype(BF16)


def _proj1(x, gain, w1, cn, sn):
    t, d = x.shape
    tm = min(t, 256)
    row = lambda w: pl.BlockSpec((tm, w), lambda i: (i, 0))
    full = lambda a: pl.BlockSpec(a.shape, lambda i: (0,) * a.ndim)
    ng = 2 * DIFF_KVH
    return pl.pallas_call(
        functools.partial(_proj1_kernel, tm=tm),
        grid=(t // tm,),
        in_specs=[row(d), full(gain), full(w1), row(LANES), row(LANES)],
        out_specs=[row(2 * DIFF_KVH * LANES),
                   pl.BlockSpec((ng, DIFF_GROUP, tm, LANES), lambda i: (0, 0, i, 0)),
                   pl.BlockSpec((DIFF_KVH, tm, LANES), lambda i: (0, i, 0)),
                   pl.BlockSpec((DIFF_KVH, tm, LANES), lambda i: (0, i, 0))],
        out_shape=[jax.ShapeDtypeStruct((t, 2 * DIFF_KVH * LANES), F32),
                   jax.ShapeDtypeStruct((ng, DIFF_GROUP, t, LANES), BF16),
                   jax.ShapeDtypeStruct((DIFF_KVH, t, LANES), BF16),
                   jax.ShapeDtypeStruct((DIFF_KVH, t, LANES), BF16)],
        compiler_params=_cparams(("parallel",)), name="proj1",
    )(x, gain, w1, cn, sn)


def _mix1_kernel(x_ref, o_ref_in, lam_ref, sub_ref, w_ref, o_ref):
    lv = lam_ref[...]
    lam = (jnp.exp(jnp.sum(lv[0:1] * lv[1:2], axis=-1, keepdims=True))
           - jnp.exp(jnp.sum(lv[2:3] * lv[3:4], axis=-1, keepdims=True)) + DIFF_LAMBDA_INIT)
    acc = x_ref[...]
    for g in range(DIFF_KVH):
        for r in range(DIFF_GROUP):
            o = o_ref_in[2 * g, r] - lam * o_ref_in[2 * g + 1, r]
            o = _rms(o, sub_ref[...]) * (1.0 - DIFF_LAMBDA_INIT)
            acc = acc + _dot(o.astype(BF16), w_ref[g * DIFF_GROUP + r])
    o_ref[...] = acc


def _mix1(x, o, lam_vecs, subln, w_out):
    t, d = x.shape
    tm = min(t, 512)
    full = lambda a: pl.BlockSpec(a.shape, lambda i: (0,) * a.ndim)
    return pl.pallas_call(
        _mix1_kernel, grid=(t // tm,),
        in_specs=[pl.BlockSpec((tm, d), lambda i: (i, 0)),
                  pl.BlockSpec((2 * DIFF_KVH, DIFF_GROUP, tm, LANES), lambda i: (0, 0, i, 0)),
                  full(lam_vecs), full(subln), full(w_out)],
        out_specs=pl.BlockSpec((tm, d), lambda i: (i, 0)), out_shape=jax.ShapeDtypeStruct((t, d), F32),
        compiler_params=_cparams(("parallel",)), name="mix1",
    )(x, o, lam_vecs, subln, w_out)


def kernel(x_prompt, x_sample, cache_mla, cache_nsa_cmp, cache_nsa_slc, state_nsa_win, cache_diff, page_table,
           attn_norm, ffn_norm, final_norm,
           w_in_a, mla_q_norm, mla_kv_norm, w_mla_uq, w_mla_uk, w_mla_uv,
           nsa_pe_k, nsa_pe_v, nsa_cmp_w1_k, nsa_cmp_w2_k, nsa_cmp_w1_v, nsa_cmp_w2_v, w_out_a,
           w_in_c, diff_lambda_q1, diff_lambda_k1, diff_lambda_q2, diff_lambda_k2, diff_subln, w_out_c,
           w_ffn_gate, w_ffn_up, w_ffn_down):
    bp, t, d = x_prompt.shape
    b, ts, _ = x_sample.shape
    assert bp == 1 and ts == 1 and t % NSA_BLOCK == 0
    n_pool = cache_mla.shape[0]
    past = page_table.shape[1] * PAGE
    assert past % NSA_BLOCK == 0
    nh = MLA_HEADS

    w0, uq, ukv_prompt, ukt, uv_pad = _prep_layer0(w_in_a, w_mla_uq, w_mla_uk, w_mla_uv)
    pe, w1c, w2c = _prep_compress(nsa_pe_k, nsa_pe_v, nsa_cmp_w1_k, nsa_cmp_w1_v, nsa_cmp_w2_k, nsa_cmp_w2_v)
    w1 = _prep_layer1(w_in_c)
    wo_a = w_out_a.reshape(2 * nh, NSA_D, d)
    wo_a = jnp.concatenate([wo_a, jnp.zeros_like(wo_a)], axis=1).astype(BF16)
    wo_c = w_out_c.reshape(DIFF_HEADS, 2 * DIFF_D, d).astype(BF16)
    wg, wu, wd = (w.astype(BF16) for w in (w_ffn_gate, w_ffn_up, w_ffn_down))
    lam_vecs = jnp.stack([diff_lambda_q1, diff_lambda_k1, diff_lambda_q2, diff_lambda_k2])
    row2 = lambda v: v.reshape(1, -1)

    xp = x_prompt[0]
    xs = x_sample[:, 0]
    tab_p = _rope_tables(jnp.arange(t, dtype=jnp.int32))
    tab_s = _rope_tables(jnp.full((b,), past, jnp.int32))

    (mla_p, cmp_p, slc_p, win_p, qm_p, qc_p, qr_p, gate_p, km_p, vm_p, ksa_p, vs_p, kw_p, vw_p) = _proj0(
        xp, row2(attn_norm[0]), w0, row2(mla_q_norm), row2(mla_kv_norm), uq, ukv_prompt, tab_p, True)
    o_mla_p = _flash(qm_p.reshape(nh, 1, t, LANES), km_p, vm_p, tq=1024, tk=512, groups=nh,
                     q_index=lambda g, k0: g, kv_index=lambda g: g, out_dtype=BF16, name="flash_mla")[:, 0]
    nb_p = t // NSA_BLOCK
    nbp_p = -(-nb_p // LANES) * LANES
    kcvc_p = _compress_prompt(cmp_p, pe, w1c, w2c)
    o_c_p, imp_p = _cmp_attn_prompt(kcvc_p, qc_p, nbp_p)
    q_aug = _select_prompt(imp_p, qr_p)
    keys_per_half = LANES * NSA_BLOCK
    o_s_p = _flash(q_aug, ksa_p[None], vs_p[None], tq=128, tk=512, groups=1,
                   q_index=lambda g, k0: k0 // keys_per_half, kv_index=lambda g: 0, name="flash_slc")[0]
    o_w_p = _flash(qr_p[None], kw_p[None], vw_p[None], tq=128, tk=128, groups=1, window=NSA_WINDOW,
                   q_index=lambda g, k0: 0, kv_index=lambda g: 0, name="flash_win")[0]
    hp = _mix0(xp, o_mla_p, o_c_p, o_s_p, o_w_p, gate_p, wo_a)
    hp = _ffn(hp, row2(ffn_norm[0]), wg[0], wu[0], wd[0], row2(final_norm), False)

    (mla_s, cmp_s, slc_s, win_s, qm_s, qc_s, qr_s, gate_s, qa_s) = _proj0(
        xs, row2(attn_norm[0]), w0, row2(mla_q_norm), row2(mla_kv_norm), uq, ukt, tab_s, False)
    q_abs = jnp.transpose(qa_s[:, :, :MLA_KV_LORA + MLA_ROPE], (1, 0, 2))[:, None]
    o_lat = _decode(q_abs, mla_s[:, None, :], cache_mla, page_table,
                    ((0, MLA_KV_LORA + MLA_ROPE, 0, MLA_KV_LORA),), 16, "decode_mla")[:, 0]
    nb_s = past // NSA_BLOCK
    nb_sel_s = -(-(past + ts) // NSA_BLOCK)
    nbp_s = -(-nb_sel_s // LANES) * LANES
    kcvc_s = _compress_paged(cache_nsa_cmp, page_table, pe, w1c, w2c, 32)
    o_c_s, imp_s = _cmp_attn_sample(kcvc_s, jnp.transpose(qc_s, (1, 0, 2)), past, nbp_s)
    idx_s, val_s = _select_sample(imp_s[:, 0], past, nb_sel_s)
    o_s_s, o_w_s = _nsa_decode(idx_s, val_s, page_table, jnp.transpose(qr_s, (1, 0, 2)), slc_s[:, None, :],
                               win_s[:, None, :], state_nsa_win.reshape(b, -1, LANES),
                               cache_nsa_slc.reshape(n_pool, PAGE, LANES), past)
    tr = lambda a: jnp.transpose(a, (1, 0, 2))
    hs = _mix0(xs, tr(o_lat), tr(o_c_s), tr(o_s_s), tr(o_w_s), gate_s, wo_a, uv_pad)
    hs = _ffn(hs, row2(ffn_norm[0]), wg[0], wu[0], wd[0], row2(final_norm), False)

    rows_p, qd_p, kd_p, vd_p = _proj1(hp, row2(attn_norm[1]), w1, tab_p[2], tab_p[3])
    o_d_p = _flash(qd_p, kd_p, vd_p, tq=256, tk=512, groups=2 * DIFF_KVH,
                   q_index=lambda g, k0: g, kv_index=lambda g: g // 2, name="flash_diff")
    hp = _mix1(hp, o_d_p, lam_vecs, row2(diff_subln), wo_c)
    y_p = _ffn(hp, row2(ffn_norm[1]), wg[1], wu[1], wd[1], row2(final_norm), True)

    rows_s, qd_s, kd_s, vd_s = _proj1(hs, row2(attn_norm[1]), w1, tab_s[2], tab_s[3])
    qd = jnp.transpose(qd_s.reshape(DIFF_KVH, 2 * DIFF_GROUP, b, LANES), (2, 0, 1, 3))
    groups_d = tuple((g * LANES, LANES, (DIFF_KVH + g) * LANES, LANES) for g in range(DIFF_KVH))
    o_d_s = _decode(qd, rows_s[:, None, :], cache_diff.reshape(n_pool, PAGE, -1), page_table, groups_d, 8,
                    "decode_diff")
    o_d_s = jnp.transpose(o_d_s.reshape(b, 2 * DIFF_KVH, DIFF_GROUP, LANES), (1, 2, 0, 3))
    hs = _mix1(hs, o_d_s, lam_vecs, row2(diff_subln), wo_c)
    y_s = _ffn(hs, row2(ffn_norm[1]), wg[1], wu[1], wd[1], row2(final_norm), True)

    wb_p = min(NSA_WINDOW, t)
    pair = lambda a: a.reshape(a.shape[0], 2, 1, NSA_D)
    new_win_s = jnp.concatenate([state_nsa_win, win_s.reshape(b, 1, 2, 1, NSA_D)], axis=1)[:, ts:]
    return (y_p[None], y_s[:, None], mla_p[None], pair(cmp_p)[None], pair(slc_p)[None], pair(win_p[t - wb_p:])[None],
            rows_p.reshape(1, t, 2, DIFF_KVH, 2 * DIFF_D),
            mla_s[:, None], pair(cmp_s)[:, None], pair(slc_s)[:, None], new_win_s,
            rows_s.reshape(b, 1, 2, DIFF_KVH, 2 * DIFF_D))
```

```python
import functools
import math

import numpy as np
import jax
import jax.numpy as jnp
from jax import lax
from jax.experimental import pallas as pl
from jax.experimental.pallas import tpu as pltpu

F32 = jnp.float32
BF16 = jnp.bfloat16

LANES = 128
VMEM_LIMIT = 56 * 1024 * 1024

ROPE_THETA = 10000.0
NORM_EPS = 1e-6
PAGE = 128
MLA_HEADS = 8
MLA_Q_LORA = 256
MLA_KV_LORA = 128
MLA_NOPE = 64
MLA_ROPE = 32
MLA_V = 64
MLA_SCALE = (MLA_NOPE + MLA_ROPE) ** -0.5
NSA_HEADS = 8
NSA_D = 64
NSA_BLOCK = 64
NSA_TOP_N = 16
NSA_WINDOW = 512
NSA_SCALE = NSA_D ** -0.5
NSA_FORCED = 1.0e4
DIFF_HEADS = 8
DIFF_KVH = 2
DIFF_D = 64
DIFF_GROUP = DIFF_HEADS // DIFF_KVH
DIFF_SCALE = DIFF_D ** -0.5
DIFF_LAMBDA_INIT = 0.8 - 0.6 * math.exp(-0.3 * 1)

NEG_BIG = -1.0e30
SEL_NEG = -float(2 ** 40)


def _cparams(sem):
    return pltpu.CompilerParams(dimension_semantics=sem, vmem_limit_bytes=VMEM_LIMIT)


def _rms(x, g):
    return x * lax.rsqrt(jnp.mean(x * x, axis=-1, keepdims=True) + NORM_EPS) * g


def _dot(a, b):
    return jnp.dot(a, b, preferred_element_type=F32)


def _dot_nt(a, b):
    return lax.dot_general(a, b, (((1,), (1,)), ((), ())), preferred_element_type=F32)


def _lane(shape):
    return lax.broadcasted_iota(jnp.int32, shape, len(shape) - 1)


_S_CQ, _S_CKV, _S_KR, _S_KR_ROT, _S_QN, _S_QN_ROT = 0, 2, 3, 4, 5, 9
_S_CMP, _S_SLC, _S_SLC_ROT, _S_WIN, _S_WIN_ROT, _S_GATE, _N_SLOTS0 = 13, 14, 15, 16, 17, 18, 19


def _proj0_kernel(x_ref, g_ref, w_ref, qn_ref, kvn_ref, uq_ref, ukv_ref, c32_ref, s32_ref, cn_ref, sn_ref,
                  *out_refs, prompt, tm):
    if prompt:
        (mla_ref, cmp_ref, slc_ref, win_ref, qm_ref, qc_ref, qr_ref, gate_ref,
         km_ref, vm_ref, ksa_ref, vs_ref, kw_ref, vw_ref) = out_refs
    else:
        (mla_ref, cmp_ref, slc_ref, win_ref, qm_ref, qc_ref, qr_ref, gate_ref, qa_ref) = out_refs

    hn = _rms(x_ref[...], g_ref[...]).astype(BF16)
    z = _dot(hn, w_ref[...])

    def slot(s, n=1):
        return z[:, s * LANES:(s + n) * LANES]

    c32, s32, cn, sn = c32_ref[...], s32_ref[...], cn_ref[...], sn_ref[...]
    lane = _lane((tm, LANES))
    low = lane < NSA_D

    cqn = _rms(slot(_S_CQ, 2), qn_ref[...]).astype(BF16)
    zq = _dot(cqn, uq_ref[...])
    ckvn = _rms(slot(_S_CKV), kvn_ref[...])
    kpe_hi = slot(_S_KR) * c32 + slot(_S_KR_ROT) * s32
    mla_ref[:, 0:MLA_KV_LORA] = ckvn
    mla_ref[:, MLA_KV_LORA:MLA_KV_LORA + MLA_ROPE] = pltpu.roll(kpe_hi, 64, axis=1)[:, 0:MLA_ROPE]
    ckvb = ckvn.astype(BF16)
    q_cos = jnp.where(low, 1.0, c32)
    nh = MLA_HEADS
    for h in range(nh):
        qh = (zq[:, h * LANES:(h + 1) * LANES] * q_cos
              + zq[:, (nh + h) * LANES:(nh + h + 1) * LANES] * s32) * MLA_SCALE
        qm_ref[h] = qh.astype(BF16)
        if not prompt:
            qlat = _dot(qh.astype(BF16), ukv_ref[h])
            qa_ref[h, :, 0:LANES] = qlat.astype(BF16)
            qa_ref[h, :, LANES:2 * LANES] = jnp.where(lane < MLA_ROPE, pltpu.roll(qh, 64, axis=1), 0.0).astype(BF16)
    if prompt:
        zk = _dot(ckvb, ukv_ref[0])
        zv = _dot(ckvb, ukv_ref[1])
        for h in range(nh):
            km_ref[h] = (zk[:, h * LANES:(h + 1) * LANES] + kpe_hi).astype(BF16)
            vm_ref[h] = zv[:, h * LANES:(h + 1) * LANES].T.astype(BF16)

    for p in range(NSA_HEADS // 2):
        plain = slot(_S_QN + p) * NSA_SCALE
        roped = (slot(_S_QN + p) * cn + slot(_S_QN_ROT + p) * sn) * NSA_SCALE
        qc_ref[2 * p] = jnp.where(low, plain, 0.0).astype(BF16)
        qc_ref[2 * p + 1] = jnp.where(low, 0.0, plain).astype(BF16)
        qr_ref[2 * p] = jnp.where(low, roped, 0.0).astype(BF16)
        qr_ref[2 * p + 1] = jnp.where(low, 0.0, roped).astype(BF16)

    ck = jnp.where(low, cn, 1.0)
    cmp_ref[...] = slot(_S_CMP)
    slc = slot(_S_SLC) * ck + slot(_S_SLC_ROT) * sn
    win = slot(_S_WIN) * ck + slot(_S_WIN_ROT) * sn
    slc_ref[...] = slc
    win_ref[...] = win
    gate_ref[...] = jax.nn.sigmoid(slot(_S_GATE))
    if prompt:
        slc_sw = pltpu.roll(slc, 64, axis=1)
        win_sw = pltpu.roll(win, 64, axis=1)
        row = pl.program_id(0) * tm + lax.broadcasted_iota(jnp.int32, (tm, LANES), 0)
        onehot = ((row >> 6) & (LANES - 1)) == lane
        ksa_ref[:, 0:LANES] = jnp.where(onehot, 1.0, 0.0).astype(BF16)
        ksa_ref[:, LANES:2 * LANES] = jnp.where(low, slc, slc_sw).astype(BF16)
        vs_ref[...] = slc_sw.T.astype(BF16)
        kw_ref[...] = jnp.where(low, win, win_sw).astype(BF16)
        vw_ref[...] = win_sw.T.astype(BF16)


def _rot_cols(w, d):
    k = w.shape[0]
    w3 = w.reshape(k, -1, d)
    return jnp.concatenate([-w3[..., d // 2:], w3[..., :d // 2]], axis=-1).reshape(k, -1)


def _place(w, left, width=LANES):
    return jnp.pad(w, ((0, 0), (left, width - left - w.shape[1])))


def _prep_layer0(w_in_a, w_mla_uq, w_mla_uk, w_mla_uv):
    o = np.cumsum([0, 256, 128, 32, 512, 64, 64, 64, 64, 64, 64, 24]).tolist()
    c_q, c_kv, k_r, q_n = (w_in_a[:, o[i]:o[i + 1]] for i in range(4))
    kv_c = w_in_a[:, o[4]:o[6]]
    kv_s = w_in_a[:, o[6]:o[8]]
    kv_w = w_in_a[:, o[8]:o[10]]
    g = w_in_a[:, o[10]:o[11]]
    z64 = jnp.zeros((w_in_a.shape[0], 64), F32)
    cols = [c_q, c_kv, _place(k_r, 64), _place(_rot_cols(k_r, MLA_ROPE), 64), q_n, _rot_cols(q_n, NSA_D),
            kv_c,
            kv_s, jnp.concatenate([_rot_cols(kv_s[:, :64], 64), z64], axis=1),
            kv_w, jnp.concatenate([_rot_cols(kv_w[:, :64], 64), z64], axis=1),
            _place(g, 0)]
    w0 = jnp.concatenate(cols, axis=1).astype(BF16)
    uq3 = w_mla_uq.reshape(MLA_Q_LORA, MLA_HEADS, MLA_NOPE + MLA_ROPE)
    nope, pe = uq3[..., :MLA_NOPE], uq3[..., MLA_NOPE:]
    pe_rot = jnp.concatenate([-pe[..., MLA_ROPE // 2:], pe[..., :MLA_ROPE // 2]], axis=-1)
    pad32 = jnp.zeros(pe.shape, F32)
    plain = jnp.concatenate([nope, pe, pad32], axis=-1).reshape(MLA_Q_LORA, -1)
    rot = jnp.concatenate([jnp.zeros(nope.shape, F32), pe_rot, pad32], axis=-1).reshape(MLA_Q_LORA, -1)
    uq = jnp.concatenate([plain, rot], axis=1).astype(BF16)
    pad_k = jnp.zeros((MLA_KV_LORA, MLA_HEADS, LANES - MLA_NOPE), F32)
    uk_slots = jnp.concatenate([w_mla_uk, pad_k], axis=-1).reshape(MLA_KV_LORA, -1)
    uv_slots = jnp.concatenate([w_mla_uv, pad_k], axis=-1).reshape(MLA_KV_LORA, -1)
    ukv_prompt = jnp.stack([uk_slots, uv_slots]).astype(BF16)
    ukt = jnp.transpose(w_mla_uk, (1, 2, 0))
    ukt = jnp.concatenate([ukt, jnp.zeros_like(ukt)], axis=1).astype(BF16)
    uv_pad = jnp.transpose(jnp.concatenate([w_mla_uv, pad_k], axis=-1), (1, 0, 2)).astype(BF16)
    return w0, uq, ukv_prompt, ukt, uv_pad


def _rope_tables(pos):
    def cs(dim):
        inv = ROPE_THETA ** (-jnp.arange(0, dim, 2, dtype=F32) / dim)
        ang = pos.astype(F32)[:, None] * inv[None, :]
        ang = jnp.concatenate([ang, ang], axis=-1)
        return jnp.cos(ang), jnp.sin(ang)
    c32, s32 = cs(MLA_ROPE)
    cn, sn = cs(NSA_D)
    return (_place(c32, 64), _place(s32, 64), jnp.concatenate([cn, cn], axis=1), jnp.concatenate([sn, sn], axis=1))


def _proj0(x, gain, w0, qnorm, kvnorm, uq, ukv, tables, prompt):
    t, d = x.shape
    tm = min(t, 256)
    nh = MLA_HEADS
    row = lambda w: pl.BlockSpec((tm, w), lambda i: (i, 0))
    head = lambda w: pl.BlockSpec((nh, tm, w), lambda i: (0, i, 0))
    full = lambda a: pl.BlockSpec(a.shape, lambda i: (0,) * a.ndim)
    out_shape = [jax.ShapeDtypeStruct((t, MLA_KV_LORA + MLA_ROPE), F32)] + [jax.ShapeDtypeStruct((t, LANES), F32)] * 3
    out_specs = [row(MLA_KV_LORA + MLA_ROPE), row(LANES), row(LANES), row(LANES)]
    out_shape += [jax.ShapeDtypeStruct((nh, t, LANES), BF16)] * 3 + [jax.ShapeDtypeStruct((t, LANES), F32)]
    out_specs += [head(LANES)] * 3 + [row(LANES)]
    if prompt:
        col = pl.BlockSpec((LANES, tm), lambda i: (0, i))
        out_shape += [jax.ShapeDtypeStruct((nh, t, LANES), BF16), jax.ShapeDtypeStruct((nh, LANES, t), BF16)]
        out_specs += [head(LANES), pl.BlockSpec((nh, LANES, tm), lambda i: (0, 0, i))]
        out_shape += [jax.ShapeDtypeStruct((t, 2 * LANES), BF16), jax.ShapeDtypeStruct((LANES, t), BF16),
                      jax.ShapeDtypeStruct((t, LANES), BF16), jax.ShapeDtypeStruct((LANES, t), BF16)]
        out_specs += [row(2 * LANES), col, row(LANES), col]
    else:
        out_shape += [jax.ShapeDtypeStruct((nh, t, 2 * LANES), BF16)]
        out_specs += [head(2 * LANES)]
    return pl.pallas_call(
        functools.partial(_proj0_kernel, prompt=prompt, tm=tm),
        grid=(t // tm,),
        in_specs=[row(d), full(gain), full(w0), full(qnorm), full(kvnorm), full(uq), full(ukv)] + [row(LANES)] * 4,
        out_specs=out_specs, out_shape=out_shape,
        compiler_params=_cparams(("parallel",)),
        name="proj0_prompt" if prompt else "proj0_sample",
    )(x, gain, w0, qnorm, kvnorm, uq, ukv, *tables)


_CMP_PAIR = 2


def _compress_rows(src_ref, n_blocks, pe_ref, w1_ref):
    acc = None
    for r0 in range(0, NSA_BLOCK, _CMP_PAIR):
        xs = [src_ref[pl.ds(r0 + u, n_blocks, stride=NSA_BLOCK), :] + pe_ref[r0 + u:r0 + u + 1, :]
              for u in range(_CMP_PAIR)]
        part = _dot(jnp.concatenate(xs, axis=1).astype(BF16), w1_ref[r0 // _CMP_PAIR])
        acc = part if acc is None else acc + part
    return acc


def _compress_prompt_kernel(f_ref, pe_ref, w1_ref, w2_ref, o_ref, *, n_blocks):
    pre = _compress_rows(f_ref, n_blocks, pe_ref, w1_ref)
    o_ref[...] = _dot(jax.nn.silu(pre).astype(BF16), w2_ref[...])


def _compress_paged_kernel(pt_ref, *refs, pages):
    page_refs = refs[:pages]
    pe_ref, w1_ref, w2_ref, o_ref, stage, pre_k, pre_v = refs[pages:]
    ch = pl.program_id(1)
    for p in range(pages):
        stage[PAGE * p:PAGE * (p + 1), :] = page_refs[p][0]
    for kv, pre in enumerate((pre_k, pre_v)):
        acc = None
        for f0 in range(0, NSA_D, _CMP_PAIR):
            xs = [stage[pl.ds(kv * NSA_D + f0 + u, pages, stride=PAGE), :]
                  + pe_ref[kv * NSA_D + f0 + u:kv * NSA_D + f0 + u + 1, :] for u in range(_CMP_PAIR)]
            part = _dot(jnp.concatenate(xs, axis=1).astype(BF16), w1_ref[kv, f0 // _CMP_PAIR])
            acc = part if acc is None else acc + part
        pre[pl.ds(pl.multiple_of(ch * pages, pages), pages), :] = acc

    @pl.when(ch == pl.num_programs(1) - 1)
    def _():
        pk, pv = pre_k[...], pre_v[...]
        low = _lane(pk.shape) < NSA_D
        even = jnp.where(low, pk, pltpu.roll(pv, 64, axis=1))
        odd = jnp.where(low, pltpu.roll(pk, 64, axis=1), pv)
        both = jnp.concatenate([even, odd], axis=0)
        o_ref[0] = _dot(jax.nn.silu(both).astype(BF16), w2_ref[...])


def _prep_compress(pe_k, pe_v, w1_k, w1_v, w2_k, w2_v):
    pe = jnp.concatenate([pe_k, pe_v], axis=1)
    w1k3 = w1_k.reshape(NSA_BLOCK, NSA_D, NSA_D)
    w1v3 = w1_v.reshape(NSA_BLOCK, NSA_D, NSA_D)
    w1 = jnp.zeros((NSA_BLOCK, 2, NSA_D, 2 * NSA_D), F32)
    w1 = w1.at[:, 0, :, :NSA_D].set(w1k3).at[:, 1, :, NSA_D:].set(w1v3)
    w1 = w1.reshape(NSA_BLOCK // _CMP_PAIR, _CMP_PAIR * LANES, 2 * NSA_D).astype(BF16)
    w2 = jnp.zeros((2 * NSA_D, 2 * NSA_D), F32)
    w2 = w2.at[:NSA_D, :NSA_D].set(w2_k).at[NSA_D:, NSA_D:].set(w2_v).astype(BF16)
    pe_t = jnp.tile(jnp.concatenate([pe_k.T, pe_v.T], axis=0), (1, 2))
    def per_dim(w3):
        wd = jnp.transpose(w3, (1, 0, 2))
        z = jnp.zeros_like(wd)
        top = jnp.concatenate([wd, z], axis=2)
        bot = jnp.concatenate([z, wd], axis=2)
        return jnp.concatenate([top, bot], axis=1)
    w1_t = jnp.stack([per_dim(w1k3), per_dim(w1v3)])
    w1_t = w1_t.reshape(2, NSA_D // _CMP_PAIR, _CMP_PAIR * LANES, LANES).astype(BF16)
    return pe, w1, w2, pe_t, w1_t


def _compress_prompt(cmp_rows, pe, w1, w2):
    t = cmp_rows.shape[0]
    nb = t // NSA_BLOCK
    full = lambda a: pl.BlockSpec(a.shape, lambda i: (0,) * a.ndim)
    return pl.pallas_call(
        functools.partial(_compress_prompt_kernel, n_blocks=nb),
        grid=(1,), in_specs=[full(cmp_rows), full(pe), full(w1), full(w2)],
        out_specs=pl.BlockSpec((nb, LANES), lambda i: (0, 0)),
        out_shape=jax.ShapeDtypeStruct((nb, LANES), F32),
        compiler_params=_cparams(("arbitrary",)), name="compress_prompt",
    )(cmp_rows, pe, w1, w2)


def _feature_major(cache):
    n, rows = cache.shape[:2]
    return jnp.transpose(cache, (0, 2, 3, 4, 1)).reshape(n, 2 * NSA_D, rows)


def _compress_paged(cache_cmp, page_table, pe_t, w1_t, w2, pages):
    b, n_pages = page_table.shape
    nb = n_pages * PAGE // NSA_BLOCK
    view = _feature_major(cache_cmp)
    pages = min(pages, n_pages)
    page_specs = [pl.BlockSpec((1, 2 * NSA_D, PAGE),
                               functools.partial(lambda i, c, pt, p: (pt[i, c * pages + p], 0, 0), p=p))
                  for p in range(pages)]
    full = lambda a: pl.BlockSpec(a.shape, lambda i, c, pt: (0,) * a.ndim)
    return pl.pallas_call(
        functools.partial(_compress_paged_kernel, pages=pages),
        grid_spec=pltpu.PrefetchScalarGridSpec(
            num_scalar_prefetch=1, grid=(b, n_pages // pages),
            in_specs=page_specs + [full(pe_t), full(w1_t), full(w2)],
            out_specs=pl.BlockSpec((1, nb, LANES), lambda i, c, pt: (i, 0, 0)),
            scratch_shapes=[pltpu.VMEM((PAGE * pages, LANES), F32), pltpu.VMEM((n_pages, LANES), F32),
                            pltpu.VMEM((n_pages, LANES), F32)]),
        out_shape=jax.ShapeDtypeStruct((b, nb, LANES), F32),
        compiler_params=_cparams(("parallel", "arbitrary")), name="compress_paged",
    )(page_table, *([view] * pages), pe_t, w1_t, w2)


def _block_id(lane, perm_pages):
    if not perm_pages:
        return lane
    assert perm_pages & (perm_pages - 1) == 0
    shift = perm_pages.bit_length() - 1
    return jnp.where(lane < 2 * perm_pages, 2 * (lane & (perm_pages - 1)) + (lane >> shift), lane)


def _cmp_attn_kernel(kv_ref, q_ref, o_ref, imp_ref, *, rows_per_query, tq, pos0, pos_step, nbp, perm_pages):
    kv = kv_ref[...] if len(kv_ref.shape) == 2 else kv_ref[0]
    nb = kv.shape[0]
    sw = pltpu.roll(kv, 64, axis=1)
    kdup = jnp.where(_lane(kv.shape) < NSA_D, kv, sw).astype(BF16)
    vk = sw.astype(BF16)
    if rows_per_query:
        q = q_ref[0]
        qpos = jnp.full((q.shape[0], 1), pos0, jnp.int32)
    else:
        q = q_ref[...].reshape(NSA_HEADS * tq, LANES)
        r = lax.broadcasted_iota(jnp.int32, (NSA_HEADS * tq, 1), 0)
        qpos = pos0 + pos_step * (pl.program_id(0) * tq + (r & (tq - 1)))
    s = _dot_nt(q, kdup)
    last = (_block_id(_lane(s.shape), perm_pages) + 1) * NSA_BLOCK - 1
    s = jnp.where(last <= qpos, s, -jnp.inf)
    m = jnp.max(s, axis=-1, keepdims=True)
    m = jnp.where(m == -jnp.inf, 0.0, m)
    e = jnp.exp(s - m)
    den = jnp.sum(e, axis=-1, keepdims=True)
    p = e / jnp.where(den > 0.0, den, 1.0)
    o = _dot(p.astype(BF16), vk)
    if rows_per_query:
        o_ref[0] = o
        imp = jnp.sum(p, axis=0, keepdims=True)
        if nbp > nb:
            imp = jnp.concatenate([imp, jnp.zeros((1, nbp - nb), F32)], axis=1)
        imp_ref[0] = imp
    else:
        o_ref[...] = o.reshape(NSA_HEADS, tq, LANES)
        imp = jnp.sum(p.reshape(NSA_HEADS, tq, nb), axis=0)
        if nbp > nb:
            imp = jnp.concatenate([imp, jnp.zeros((tq, nbp - nb), F32)], axis=1)
        imp_ref[...] = imp


def _cmp_attn_prompt(kcvc, q_cmp, nbp):
    nh, t, _ = q_cmp.shape
    nb = kcvc.shape[0]
    tq = min(t, 128)
    return pl.pallas_call(
        functools.partial(_cmp_attn_kernel, rows_per_query=False, tq=tq, pos0=0, pos_step=1, nbp=nbp, perm_pages=0),
        grid=(t // tq,),
        in_specs=[pl.BlockSpec((nb, LANES), lambda i: (0, 0)), pl.BlockSpec((nh, tq, LANES), lambda i: (0, i, 0))],
        out_specs=[pl.BlockSpec((nh, tq, LANES), lambda i: (0, i, 0)), pl.BlockSpec((tq, nbp), lambda i: (i, 0))],
        out_shape=[jax.ShapeDtypeStruct((nh, t, LANES), F32), jax.ShapeDtypeStruct((t, nbp), F32)],
        compiler_params=_cparams(("parallel",)), name="cmp_attn_prompt",
    )(kcvc, q_cmp)


def _cmp_attn_sample(kcvc, q_cmp, pos, nbp, perm_pages):
    b, nh, _ = q_cmp.shape
    nb = kcvc.shape[1]
    return pl.pallas_call(
        functools.partial(_cmp_attn_kernel, rows_per_query=True, tq=1, pos0=pos, pos_step=0, nbp=nbp,
                          perm_pages=perm_pages),
        grid=(b,),
        in_specs=[pl.BlockSpec((1, nb, LANES), lambda i: (i, 0, 0)), pl.BlockSpec((1, nh, LANES), lambda i: (i, 0, 0))],
        out_specs=[pl.BlockSpec((1, nh, LANES), lambda i: (i, 0, 0)), pl.BlockSpec((1, 1, nbp), lambda i: (i, 0, 0))],
        out_shape=[jax.ShapeDtypeStruct((b, nh, LANES), F32), jax.ShapeDtypeStruct((b, 1, nbp), F32)],
        compiler_params=_cparams(("parallel",)), name="cmp_attn_sample",
    )(kcvc, q_cmp)


def _select_kernel(imp_ref, *refs, tq, pos0, pos_step, nb_sel, n_halves, with_query, perm_pages):
    if with_query:
        q_ref, qa_ref = refs
    else:
        idx_ref, val_ref = refs
    imp = imp_ref[...]
    nbp = imp.shape[1]
    lane = _block_id(_lane(imp.shape), perm_pages)
    qpos = pos0 + pos_step * (pl.program_id(0) * tq + lax.broadcasted_iota(jnp.int32, (tq, 1), 0))
    cur = qpos >> 6
    forced = (lane == 0) | (lane == cur) | (lane == cur - 1)
    score = jnp.where(forced, NSA_FORCED, imp)
    work = jnp.where((lane <= cur) & (lane < nb_sel), score, -jnp.inf)
    lane_f = lane.astype(F32)
    neg = jnp.full(imp.shape, SEL_NEG, F32)
    lane16 = _lane((tq, LANES))
    idx_acc = jnp.zeros((tq, LANES), jnp.int32)
    val_acc = jnp.zeros((tq, LANES), jnp.int32)
    for n in range(min(NSA_TOP_N, nb_sel)):
        mx = jnp.max(work, axis=-1, keepdims=True)
        first = jnp.min(jnp.where(work == mx, lane_f, float(nbp)), axis=-1, keepdims=True)
        pick = lane_f == first
        valid_neg = jnp.where(mx > -jnp.inf, 0.0, SEL_NEG)
        neg = jnp.where(pick, jnp.maximum(neg, valid_neg), neg)
        work = jnp.where(pick, -jnp.inf, work)
        if not with_query:
            idx_acc = jnp.where(lane16 == n, first.astype(jnp.int32), idx_acc)
            val_acc = jnp.where(lane16 == n, jnp.where(mx > -jnp.inf, 1, 0), val_acc)
    if with_query:
        neg = neg.astype(BF16)
        for hf in range(n_halves):
            part = neg[:, hf * LANES:(hf + 1) * LANES]
            for h in range(NSA_HEADS):
                qa_ref[hf, h, :, 0:LANES] = part
                qa_ref[hf, h, :, LANES:2 * LANES] = q_ref[h]
    else:
        idx_ref[...] = idx_acc
        val_ref[...] = val_acc


def _select_prompt(imp, q_rot):
    t, nbp = imp.shape
    nh = q_rot.shape[0]
    tq = min(t, 128)
    n_halves = nbp // LANES
    return pl.pallas_call(
        functools.partial(_select_kernel, tq=tq, pos0=0, pos_step=1, nb_sel=t // NSA_BLOCK, n_halves=n_halves,
                          with_query=True, perm_pages=0),
        grid=(t // tq,),
        in_specs=[pl.BlockSpec((tq, nbp), lambda i: (i, 0)), pl.BlockSpec((nh, tq, LANES), lambda i: (0, i, 0))],
        out_specs=pl.BlockSpec((n_halves, nh, tq, 2 * LANES), lambda i: (0, 0, i, 0)),
        out_shape=jax.ShapeDtypeStruct((n_halves, nh, t, 2 * LANES), BF16),
        compiler_params=_cparams(("parallel",)), name="select_prompt",
    )(imp, q_rot)


def _select_sample(imp, pos, nb_sel, perm_pages):
    b, nbp = imp.shape
    return pl.pallas_call(
        functools.partial(_select_kernel, tq=b, pos0=pos, pos_step=0, nb_sel=nb_sel, n_halves=0, with_query=False,
                          perm_pages=perm_pages),
        grid=(1,),
        in_specs=[pl.BlockSpec((b, nbp), lambda i: (0, 0))],
        out_specs=[pl.BlockSpec((b, LANES), lambda i: (0, 0))] * 2,
        out_shape=[jax.ShapeDtypeStruct((b, LANES), jnp.int32)] * 2,
        compiler_params=_cparams(("arbitrary",)), name="select_sample",
    )(imp)


def _flash_kernel(q_ref, k_ref, vt_ref, o_ref, m_scr, l_scr, acc_scr, *, r, tq, tk, window, n_j):
    i, j = pl.program_id(1), pl.program_id(2)
    m_rows = r * tq

    @pl.when(j == 0)
    def _():
        m_scr[...] = jnp.full(m_scr.shape, NEG_BIG, F32)
        l_scr[...] = jnp.zeros(l_scr.shape, F32)
        acc_scr[...] = jnp.zeros(acc_scr.shape, F32)

    if window:
        kv_blk = i - (n_j - 1) + j
        active = kv_blk >= 0
        need_mask = None
    else:
        kv_blk = j
        active = j * tk <= i * tq + (tq - 1)
        need_mask = j * tk + (tk - 1) > i * tq

    def step(masked):
        q = q_ref[0].reshape(m_rows, q_ref.shape[-1])
        s = _dot_nt(k_ref[0], q)
        if masked:
            qpos = i * tq + (_lane((tk, m_rows)) & (tq - 1))
            kpos = kv_blk * tk + lax.broadcasted_iota(jnp.int32, (tk, m_rows), 0)
            ok = kpos <= qpos
            if window:
                ok = ok & (qpos - kpos < window)
            s = jnp.where(ok, s, NEG_BIG)
        m_prev = m_scr[...]
        m_new = jnp.maximum(m_prev, jnp.max(s, axis=0, keepdims=True))
        alpha = jnp.exp(m_prev - m_new)
        p = jnp.exp(s - m_new)
        l_scr[...] = alpha * l_scr[...] + jnp.sum(p, axis=0, keepdims=True)
        acc_scr[...] = alpha * acc_scr[...] + _dot(vt_ref[0], p.astype(BF16))
        m_scr[...] = m_new

    if window:
        pl.when(active)(lambda: step(True))
    else:
        pl.when(active & need_mask)(lambda: step(True))
        pl.when(active & jnp.logical_not(need_mask))(lambda: step(False))

    @pl.when(j == n_j - 1)
    def _():
        o = (acc_scr[...] / l_scr[...]).T
        o_ref[0] = o.reshape(r, tq, o.shape[-1]).astype(o_ref.dtype)


def _flash(q, k, vt, *, tq, tk, groups, q_index, kv_index, window=0, out_dtype=F32, name):
    r, t, dq = q.shape[1:]
    dv = vt.shape[1]
    tq, tk = min(tq, t), min(tk, t)
    if window:
        assert tq == tk and window % tk == 0
        n_j = window // tk + 1
        kv_blk = lambda i, j: jnp.maximum(i - (n_j - 1) + j, 0)
    else:
        n_j = t // tk
        kv_blk = lambda i, j: jnp.minimum(j, (i * tq + tq - 1) // tk)
    return pl.pallas_call(
        functools.partial(_flash_kernel, r=r, tq=tq, tk=tk, window=window, n_j=n_j),
        grid=(groups, t // tq, n_j),
        in_specs=[pl.BlockSpec((1, r, tq, dq), lambda g, i, j: (q_index(g, kv_blk(i, j) * tk), 0, i, 0)),
                  pl.BlockSpec((1, tk, dq), lambda g, i, j: (kv_index(g), kv_blk(i, j), 0)),
                  pl.BlockSpec((1, dv, tk), lambda g, i, j: (kv_index(g), 0, kv_blk(i, j)))],
        out_specs=pl.BlockSpec((1, r, tq, dv), lambda g, i, j: (g, 0, i, 0)),
        out_shape=jax.ShapeDtypeStruct((groups, r, t, dv), out_dtype),
        scratch_shapes=[pltpu.VMEM((1, r * tq), F32), pltpu.VMEM((1, r * tq), F32), pltpu.VMEM((dv, r * tq), F32)],
        compiler_params=_cparams(("parallel", "parallel", "arbitrary")), name=name,
    )(q, k, vt)


def _online_softmax_step(s, m_scr, l_scr):
    m_prev = m_scr[...]
    m_new = jnp.maximum(m_prev, jnp.max(s, axis=-1, keepdims=True))
    alpha = jnp.exp(m_prev - m_new)
    p = jnp.exp(s - m_new)
    l_scr[...] = alpha * l_scr[...] + jnp.sum(p, axis=-1, keepdims=True)
    m_scr[...] = m_new
    return alpha, p


def _decode_mla_kernel(pt_ref, q_ref, new_ref, *refs, pages):
    page_refs = refs[:pages]
    o_ref, m_scr, l_scr, acc_scr = refs[pages:]
    c = pl.program_id(1)
    q = q_ref[0]

    @pl.when(c == 0)
    def _():
        new = new_ref[0].astype(BF16).astype(F32)
        m_scr[...] = jnp.sum(q.astype(F32) * new, axis=-1, keepdims=True)
        l_scr[...] = jnp.ones(l_scr.shape, F32)
        acc_scr[...] = jnp.broadcast_to(new[:, 0:MLA_KV_LORA], acc_scr.shape)

    kt = jnp.concatenate([pr[0] for pr in page_refs], axis=1).astype(BF16)
    alpha, p = _online_softmax_step(_dot(q, kt), m_scr, l_scr)
    acc_scr[...] = alpha * acc_scr[...] + _dot_nt(p.astype(BF16), kt[0:MLA_KV_LORA])

    @pl.when(c == pl.num_programs(1) - 1)
    def _():
        o_ref[0] = acc_scr[...] / l_scr[...]


def _decode_mla(q, new_rows, cache_mla, page_table, pages):
    b, rows, kw = q.shape
    n_pages = page_table.shape[1]
    pages = min(pages, n_pages)
    view = jnp.transpose(cache_mla, (0, 2, 1))
    page_specs = [pl.BlockSpec((1, kw, PAGE), functools.partial(lambda i, c, pt, p: (pt[i, c * pages + p], 0, 0), p=p))
                  for p in range(pages)]
    return pl.pallas_call(
        functools.partial(_decode_mla_kernel, pages=pages),
        grid_spec=pltpu.PrefetchScalarGridSpec(
            num_scalar_prefetch=1, grid=(b, n_pages // pages),
            in_specs=[pl.BlockSpec((1, rows, kw), lambda i, c, pt: (i, 0, 0)),
                      pl.BlockSpec((1, 1, kw), lambda i, c, pt: (i, 0, 0))] + page_specs,
            out_specs=pl.BlockSpec((1, rows, MLA_KV_LORA), lambda i, c, pt: (i, 0, 0)),
            scratch_shapes=[pltpu.VMEM((rows, 1), F32), pltpu.VMEM((rows, 1), F32),
                            pltpu.VMEM((rows, MLA_KV_LORA), F32)]),
        out_shape=jax.ShapeDtypeStruct((b, rows, MLA_KV_LORA), F32),
        compiler_params=_cparams(("parallel", "arbitrary")), name="decode_mla",
    )(page_table, q, new_rows, *([view] * pages))


_DIFF_ROWS = 2 * DIFF_KVH * DIFF_GROUP
_DIFF_COLS = 2 * DIFF_KVH


def _decode_diff_kernel(pt_ref, q_ref, new_ref, *refs, pages):
    page_refs = refs[:pages]
    o_ref, m_scr, l_scr, acc_scr = refs[pages:]
    c = pl.program_id(1)
    q = q_ref[0]
    head = lax.broadcasted_iota(jnp.int32, (_DIFF_ROWS, 1), 0) >> 3

    @pl.when(c == 0)
    def _():
        new = new_ref[0].astype(BF16).astype(F32)
        knew = jnp.where(head == 0, new[:, 0:LANES], new[:, LANES:2 * LANES])
        vnew = jnp.where(head == 0, new[:, 2 * LANES:3 * LANES], new[:, 3 * LANES:4 * LANES])
        m_scr[...] = jnp.sum(q.astype(F32) * knew, axis=-1, keepdims=True)
        l_scr[...] = jnp.ones(l_scr.shape, F32)
        acc_scr[...] = vnew

    rows = jnp.concatenate([pr[0] for pr in page_refs], axis=0).astype(BF16)
    s = _dot_nt(q, rows)
    s = jnp.where((_lane(s.shape) & (_DIFF_COLS - 1)) == head, s, NEG_BIG)
    alpha, p = _online_softmax_step(s, m_scr, l_scr)
    p = pltpu.roll(p, DIFF_KVH, axis=1)
    acc_scr[...] = alpha * acc_scr[...] + _dot(p.astype(BF16), rows)

    @pl.when(c == pl.num_programs(1) - 1)
    def _():
        o_ref[0] = acc_scr[...] / l_scr[...]


def _decode_diff(q, new_rows, cache_diff, page_table, pages):
    b, rows, _ = q.shape
    n_pool = cache_diff.shape[0]
    n_pages = page_table.shape[1]
    pages = min(pages, n_pages)
    view = cache_diff.reshape(n_pool, PAGE * _DIFF_COLS, LANES)
    page_specs = [pl.BlockSpec((1, PAGE * _DIFF_COLS, LANES),
                               functools.partial(lambda i, c, pt, p: (pt[i, c * pages + p], 0, 0), p=p))
                  for p in range(pages)]
    return pl.pallas_call(
        functools.partial(_decode_diff_kernel, pages=pages),
        grid_spec=pltpu.PrefetchScalarGridSpec(
            num_scalar_prefetch=1, grid=(b, n_pages // pages),
            in_specs=[pl.BlockSpec((1, rows, LANES), lambda i, c, pt: (i, 0, 0)),
                      pl.BlockSpec((1, 1, _DIFF_COLS * LANES), lambda i, c, pt: (i, 0, 0))] + page_specs,
            out_specs=pl.BlockSpec((1, rows, LANES), lambda i, c, pt: (i, 0, 0)),
            scratch_shapes=[pltpu.VMEM((rows, 1), F32), pltpu.VMEM((rows, 1), F32), pltpu.VMEM((rows, LANES), F32)]),
        out_shape=jax.ShapeDtypeStruct((b, rows, LANES), F32),
        compiler_params=_cparams(("parallel", "arbitrary")), name="decode_diff",
    )(page_table, q, new_rows, *([view] * pages))


def _nsa_decode_kernel(idx_ref, val_ref, pt_ref, q_ref, news_ref, neww_ref, win_ref, *refs, n_sel, nb_cached, pos,
                       win_pos0):
    blk_refs = refs[:n_sel]
    os_ref, ow_ref = refs[n_sel:]
    b = pl.program_id(0)
    q = q_ref[0]
    qf = q.astype(F32)

    def new_key_score(new_ref):
        row = new_ref[0].astype(BF16).astype(F32)
        kd = jnp.where(_lane(row.shape) < NSA_D, row, pltpu.roll(row, 64, axis=1))
        vnew = jnp.where(_lane(row.shape) < NSA_D, pltpu.roll(row, 64, axis=1), 0.0)
        return jnp.sum(qf * kd, axis=-1, keepdims=True), vnew

    def attend(ktv, ok, s_new, new_ok, vnew):
        ktv = ktv.astype(BF16)
        s = _dot(q, jnp.concatenate([ktv[0:NSA_D], ktv[0:NSA_D]], axis=0))
        s = jnp.where(ok, s, -jnp.inf)
        if new_ok is not True:
            s_new = jnp.where(new_ok, s_new, -jnp.inf)
        m = jnp.maximum(jnp.max(s, axis=-1, keepdims=True), s_new)
        m = jnp.where(m == -jnp.inf, 0.0, m)
        e = jnp.exp(s - m)
        e_new = jnp.exp(s_new - m)
        den = jnp.sum(e, axis=-1, keepdims=True) + e_new
        den = jnp.where(den > 0.0, den, 1.0)
        o = _dot_nt(e.astype(BF16), ktv)
        return (pltpu.roll(o, 64, axis=1) + e_new * vnew) / den

    ktv = jnp.concatenate([r[0] for r in blk_refs], axis=1)
    lane = _lane((NSA_HEADS, n_sel * PAGE))
    page_of, half_of = lane >> 7, (lane >> 6) & 1
    ok = jnp.zeros(lane.shape, F32)
    new_ok = jnp.int32(0)
    for n in range(n_sel):
        valid = val_ref[b * n_sel + n] > 0
        blk = idx_ref[b * n_sel + n]
        use = jnp.where(valid & (blk < nb_cached), 1.0, 0.0)
        ok = jnp.where(page_of == n, jnp.where(half_of == (blk & 1), use, 0.0), ok)
        new_ok = new_ok | jnp.where(valid & (blk == nb_cached), 1, 0)
    s_new, vnew = new_key_score(news_ref)
    os_ref[0] = attend(ktv, ok > 0.0, s_new, new_ok > 0, vnew)

    wpos = win_pos0 + _lane((NSA_HEADS, win_ref.shape[-1]))
    d = pos - wpos
    okw = (d >= 0) & (d < NSA_WINDOW) & (wpos >= 0)
    sw_new, vwnew = new_key_score(neww_ref)
    ow_ref[0] = attend(win_ref[0], okw, sw_new, True, vwnew)


def _nsa_decode(idx, val, page_table, q_rot, new_slc, new_win, state_win, cache_slc, pos):
    b = q_rot.shape[0]
    n_pages = page_table.shape[1]
    nb_cached = n_pages * PAGE // NSA_BLOCK
    n_sel = min(NSA_TOP_N, nb_cached + 1)
    wb = state_win.shape[1]
    blocks = _feature_major(cache_slc)
    state_win = _feature_major(state_win)

    def blk_index(i, idx_r, val_r, pt_r, n):
        blk = jnp.minimum(idx_r[i * n_sel + n], nb_cached - 1)
        return (pt_r[i, blk >> 1], 0, 0)

    blk_specs = [pl.BlockSpec((1, 2 * NSA_D, PAGE), functools.partial(blk_index, n=n)) for n in range(n_sel)]
    per_seq = lambda shp: pl.BlockSpec((1,) + shp, lambda i, a, c, d: (i, 0, 0))
    return pl.pallas_call(
        functools.partial(_nsa_decode_kernel, n_sel=n_sel, nb_cached=nb_cached, pos=pos, win_pos0=pos - wb),
        grid_spec=pltpu.PrefetchScalarGridSpec(
            num_scalar_prefetch=3, grid=(b,),
            in_specs=[per_seq((NSA_HEADS, LANES)), per_seq((1, LANES)), per_seq((1, LANES)), per_seq((2 * NSA_D, wb))]
            + blk_specs,
            out_specs=[per_seq((NSA_HEADS, LANES))] * 2),
        out_shape=[jax.ShapeDtypeStruct((b, NSA_HEADS, LANES), F32)] * 2,
        compiler_params=_cparams(("parallel",)), name="nsa_decode",
    )(idx[:, :n_sel].reshape(-1), val[:, :n_sel].reshape(-1), page_table, q_rot, new_slc, new_win, state_win, *([blocks] * n_sel))


def _mix0_kernel(x_ref, om_ref, oc_ref, os_ref, ow_ref, g_ref, w_ref, *refs, absorbed):
    if absorbed:
        uv_ref, o_ref = refs
    else:
        (o_ref,) = refs
    gates = g_ref[...]
    acc = x_ref[...]
    for h in range(MLA_HEADS):
        om = om_ref[h]
        if absorbed:
            om = _dot(om.astype(BF16), uv_ref[h])
        acc = acc + _dot(om.astype(BF16), w_ref[h])
    for h in range(NSA_HEADS):
        gc, gs, gw = (gates[:, k * NSA_HEADS + h:k * NSA_HEADS + h + 1] for k in range(3))
        on = gc * oc_ref[h] + gs * os_ref[h] + gw * ow_ref[h]
        acc = acc + _dot(on.astype(BF16), w_ref[MLA_HEADS + h])
    o_ref[...] = acc


def _mix0(x, o_mla, o_c, o_s, o_w, gates, w_out, uv=None):
    t, d = x.shape
    tm = min(t, 512)
    row = lambda w: pl.BlockSpec((tm, w), lambda i: (i, 0))
    head = pl.BlockSpec((NSA_HEADS, tm, LANES), lambda i: (0, i, 0))
    full = lambda a: pl.BlockSpec(a.shape, lambda i: (0,) * a.ndim)
    ins = [x, o_mla, o_c, o_s, o_w, gates, w_out] + ([uv] if uv is not None else [])
    specs = [row(d), head, head, head, head, row(LANES), full(w_out)] + ([full(uv)] if uv is not None else [])
    return pl.pallas_call(
        functools.partial(_mix0_kernel, absorbed=uv is not None),
        grid=(t // tm,), in_specs=specs, out_specs=row(d), out_shape=jax.ShapeDtypeStruct((t, d), F32),
        compiler_params=_cparams(("parallel",)), name="mix0",
    )(*ins)


def _ffn_kernel(x_ref, g_ref, wg_ref, wu_ref, wd_ref, fg_ref, o_ref, hn_scr, acc_scr, *, final_norm):
    f = pl.program_id(1)

    @pl.when(f == 0)
    def _():
        hn_scr[...] = _rms(x_ref[...], g_ref[...]).astype(BF16)
        acc_scr[...] = x_ref[...]

    hn = hn_scr[...]
    act = jax.nn.silu(_dot(hn, wg_ref[...])) * _dot(hn, wu_ref[...])
    acc_scr[...] += _dot(act.astype(BF16), wd_ref[...])

    @pl.when(f == pl.num_programs(1) - 1)
    def _():
        y = acc_scr[...]
        o_ref[...] = _rms(y, fg_ref[...]) if final_norm else y


def _ffn(x, gain, wg, wu, wd, final_gain, final_norm):
    t, d = x.shape
    dff = wg.shape[1]
    tm = min(t, 512)
    tf = 256
    return pl.pallas_call(
        functools.partial(_ffn_kernel, final_norm=final_norm),
        grid=(t // tm, dff // tf),
        in_specs=[pl.BlockSpec((tm, d), lambda i, f: (i, 0)), pl.BlockSpec((1, d), lambda i, f: (0, 0)),
                  pl.BlockSpec((d, tf), lambda i, f: (0, f)), pl.BlockSpec((d, tf), lambda i, f: (0, f)),
                  pl.BlockSpec((tf, d), lambda i, f: (f, 0)), pl.BlockSpec((1, d), lambda i, f: (0, 0))],
        out_specs=pl.BlockSpec((tm, d), lambda i, f: (i, 0)),
        out_shape=jax.ShapeDtypeStruct((t, d), F32),
        scratch_shapes=[pltpu.VMEM((tm, d), BF16), pltpu.VMEM((tm, d), F32)],
        compiler_params=_cparams(("parallel", "arbitrary")), name="ffn",
    )(x, gain, wg, wu, wd, final_gain)


_N_QSLOT = DIFF_HEADS


def _proj1_kernel(x_ref, g_ref, w_ref, cn_ref, sn_ref, rows_ref, q_ref, k_ref, v_ref, *, tm):
    hn = _rms(x_ref[...], g_ref[...]).astype(BF16)
    z = _dot(hn, w_ref[...])
    cn, sn = cn_ref[...], sn_ref[...]
    low = _lane((tm, LANES)) < DIFF_D

    def slot(s):
        return z[:, s * LANES:(s + 1) * LANES]

    for g in range(DIFF_KVH):
        for r in range(DIFF_GROUP):
            h = g * DIFF_GROUP + r
            qp = (slot(h) * cn + slot(_N_QSLOT + h) * sn) * DIFF_SCALE
            q_ref[2 * g, r] = jnp.where(low, qp, 0.0).astype(BF16)
            q_ref[2 * g + 1, r] = jnp.where(low, 0.0, qp).astype(BF16)
        k = slot(2 * _N_QSLOT + g) * cn + slot(2 * _N_QSLOT + DIFF_KVH + g) * sn
        v = slot(2 * _N_QSLOT + 2 * DIFF_KVH + g)
        rows_ref[:, g * LANES:(g + 1) * LANES] = k
        rows_ref[:, (DIFF_KVH + g) * LANES:(DIFF_KVH + g + 1) * LANES] = v
        k_ref[g] = k.astype(BF16)
        v_ref[g] = v.T.astype(BF16)


def _prep_layer1(w_in_c):
    nq = DIFF_HEADS * 2 * DIFF_D
    nk = DIFF_KVH * 2 * DIFF_D
    wq, wk, wv = w_in_c[:, :nq], w_in_c[:, nq:nq + nk], w_in_c[:, nq + nk:]
    return jnp.concatenate([wq, _rot_cols(wq, DIFF_D), wk, _rot_cols(wk, DIFF_D), wv], axis=1).astype(BF16)


def _proj1(x, gain, w1, cn, sn):
    t, d = x.shape
    tm = min(t, 256)
    row = lambda w: pl.BlockSpec((tm, w), lambda i: (i, 0))
    full = lambda a: pl.BlockSpec(a.shape, lambda i: (0,) * a.ndim)
    ng = 2 * DIFF_KVH
    return pl.pallas_call(
        functools.partial(_proj1_kernel, tm=tm),
        grid=(t // tm,),
        in_specs=[row(d), full(gain), full(w1), row(LANES), row(LANES)],
        out_specs=[row(2 * DIFF_KVH * LANES),
                   pl.BlockSpec((ng, DIFF_GROUP, tm, LANES), lambda i: (0, 0, i, 0)),
                   pl.BlockSpec((DIFF_KVH, tm, LANES), lambda i: (0, i, 0)),
                   pl.BlockSpec((DIFF_KVH, LANES, tm), lambda i: (0, 0, i))],
        out_shape=[jax.ShapeDtypeStruct((t, 2 * DIFF_KVH * LANES), F32),
                   jax.ShapeDtypeStruct((ng, DIFF_GROUP, t, LANES), BF16),
                   jax.ShapeDtypeStruct((DIFF_KVH, t, LANES), BF16),
                   jax.ShapeDtypeStruct((DIFF_KVH, LANES, t), BF16)],
        compiler_params=_cparams(("parallel",)), name="proj1",
    )(x, gain, w1, cn, sn)


def _mix1_kernel(x_ref, o_ref_in, lam_ref, sub_ref, w_ref, o_ref):
    lv = lam_ref[...]
    lam = (jnp.exp(jnp.sum(lv[0:1] * lv[1:2], axis=-1, keepdims=True))
           - jnp.exp(jnp.sum(lv[2:3] * lv[3:4], axis=-1, keepdims=True)) + DIFF_LAMBDA_INIT)
    acc = x_ref[...]
    for g in range(DIFF_KVH):
        for r in range(DIFF_GROUP):
            o = o_ref_in[2 * g, r] - lam * o_ref_in[2 * g + 1, r]
            o = _rms(o, sub_ref[...]) * (1.0 - DIFF_LAMBDA_INIT)
            acc = acc + _dot(o.astype(BF16), w_ref[g * DIFF_GROUP + r])
    o_ref[...] = acc


def _mix1(x, o, lam_vecs, subln, w_out):
    t, d = x.shape
    tm = min(t, 512)
    full = lambda a: pl.BlockSpec(a.shape, lambda i: (0,) * a.ndim)
    return pl.pallas_call(
        _mix1_kernel, grid=(t // tm,),
        in_specs=[pl.BlockSpec((tm, d), lambda i: (i, 0)),
                  pl.BlockSpec((2 * DIFF_KVH, DIFF_GROUP, tm, LANES), lambda i: (0, 0, i, 0)),
                  full(lam_vecs), full(subln), full(w_out)],
        out_specs=pl.BlockSpec((tm, d), lambda i: (i, 0)), out_shape=jax.ShapeDtypeStruct((t, d), F32),
        compiler_params=_cparams(("parallel",)), name="mix1",
    )(x, o, lam_vecs, subln, w_out)


def kernel(x_prompt, x_sample, cache_mla, cache_nsa_cmp, cache_nsa_slc, state_nsa_win, cache_diff, page_table,
           attn_norm, ffn_norm, final_norm,
           w_in_a, mla_q_norm, mla_kv_norm, w_mla_uq, w_mla_uk, w_mla_uv,
           nsa_pe_k, nsa_pe_v, nsa_cmp_w1_k, nsa_cmp_w2_k, nsa_cmp_w1_v, nsa_cmp_w2_v, w_out_a,
           w_in_c, diff_lambda_q1, diff_lambda_k1, diff_lambda_q2, diff_lambda_k2, diff_subln, w_out_c,
           w_ffn_gate, w_ffn_up, w_ffn_down):
    bp, t, d = x_prompt.shape
    b, ts, _ = x_sample.shape
    assert bp == 1 and ts == 1 and t % NSA_BLOCK == 0
    n_pool = cache_mla.shape[0]
    past = page_table.shape[1] * PAGE
    assert past % NSA_BLOCK == 0
    nh = MLA_HEADS

    w0, uq, ukv_prompt, ukt, uv_pad = _prep_layer0(w_in_a, w_mla_uq, w_mla_uk, w_mla_uv)
    pe, w1c, w2c, pe_t, w1c_t = _prep_compress(nsa_pe_k, nsa_pe_v, nsa_cmp_w1_k, nsa_cmp_w1_v, nsa_cmp_w2_k,
                                               nsa_cmp_w2_v)
    w1 = _prep_layer1(w_in_c)
    wo_a = w_out_a.reshape(2 * nh, NSA_D, d)
    wo_a = jnp.concatenate([wo_a, jnp.zeros_like(wo_a)], axis=1).astype(BF16)
    wo_c = w_out_c.reshape(DIFF_HEADS, 2 * DIFF_D, d).astype(BF16)
    wg, wu, wd = (w.astype(BF16) for w in (w_ffn_gate, w_ffn_up, w_ffn_down))
    lam_vecs = jnp.stack([diff_lambda_q1, diff_lambda_k1, diff_lambda_q2, diff_lambda_k2])
    row2 = lambda v: v.reshape(1, -1)

    xp = x_prompt[0]
    xs = x_sample[:, 0]
    tab_p = _rope_tables(jnp.arange(t, dtype=jnp.int32))
    tab_s = _rope_tables(jnp.full((b,), past, jnp.int32))

    (mla_p, cmp_p, slc_p, win_p, qm_p, qc_p, qr_p, gate_p, km_p, vm_p, ksa_p, vs_p, kw_p, vw_p) = _proj0(
        xp, row2(attn_norm[0]), w0, row2(mla_q_norm), row2(mla_kv_norm), uq, ukv_prompt, tab_p, True)
    o_mla_p = _flash(qm_p.reshape(nh, 1, t, LANES), km_p, vm_p, tq=1024, tk=512, groups=nh,
                     q_index=lambda g, k0: g, kv_index=lambda g: g, out_dtype=BF16, name="flash_mla")[:, 0]
    nb_p = t // NSA_BLOCK
    nbp_p = -(-nb_p // LANES) * LANES
    kcvc_p = _compress_prompt(cmp_p, pe, w1c, w2c)
    o_c_p, imp_p = _cmp_attn_prompt(kcvc_p, qc_p, nbp_p)
    q_aug = _select_prompt(imp_p, qr_p)
    keys_per_half = LANES * NSA_BLOCK
    o_s_p = _flash(q_aug, ksa_p[None], vs_p[None], tq=128, tk=512, groups=1,
                   q_index=lambda g, k0: k0 // keys_per_half, kv_index=lambda g: 0, name="flash_slc")[0]
    o_w_p = _flash(qr_p[None], kw_p[None], vw_p[None], tq=128, tk=128, groups=1, window=NSA_WINDOW,
                   q_index=lambda g, k0: 0, kv_index=lambda g: 0, name="flash_win")[0]
    hp = _mix0(xp, o_mla_p, o_c_p, o_s_p, o_w_p, gate_p, wo_a)
    hp = _ffn(hp, row2(ffn_norm[0]), wg[0], wu[0], wd[0], row2(final_norm), False)

    (mla_s, cmp_s, slc_s, win_s, qm_s, qc_s, qr_s, gate_s, qa_s) = _proj0(
        xs, row2(attn_norm[0]), w0, row2(mla_q_norm), row2(mla_kv_norm), uq, ukt, tab_s, False)
    q_abs = jnp.transpose(qa_s[:, :, :MLA_KV_LORA + MLA_ROPE], (1, 0, 2))
    o_lat = _decode_mla(q_abs, mla_s[:, None, :], cache_mla, page_table, 16)
    n_pages = page_table.shape[1]
    nb_sel_s = -(-(past + ts) // NSA_BLOCK)
    nbp_s = -(-nb_sel_s // LANES) * LANES
    kcvc_s = _compress_paged(cache_nsa_cmp, page_table, pe_t, w1c_t, w2c, 32)
    o_c_s, imp_s = _cmp_attn_sample(kcvc_s, jnp.transpose(qc_s, (1, 0, 2)), past, nbp_s, n_pages)
    idx_s, val_s = _select_sample(imp_s[:, 0], past, nb_sel_s, n_pages)
    o_s_s, o_w_s = _nsa_decode(idx_s, val_s, page_table, jnp.transpose(qr_s, (1, 0, 2)), slc_s[:, None, :],
                               win_s[:, None, :], state_nsa_win, cache_nsa_slc, past)
    tr = lambda a: jnp.transpose(a, (1, 0, 2))
    hs = _mix0(xs, tr(o_lat), tr(o_c_s), tr(o_s_s), tr(o_w_s), gate_s, wo_a, uv_pad)
    hs = _ffn(hs, row2(ffn_norm[0]), wg[0], wu[0], wd[0], row2(final_norm), False)

    rows_p, qd_p, kd_p, vd_p = _proj1(hp, row2(attn_norm[1]), w1, tab_p[2], tab_p[3])
    o_d_p = _flash(qd_p, kd_p, vd_p, tq=256, tk=512, groups=2 * DIFF_KVH,
                   q_index=lambda g, k0: g, kv_index=lambda g: g // 2, name="flash_diff")
    hp = _mix1(hp, o_d_p, lam_vecs, row2(diff_subln), wo_c)
    y_p = _ffn(hp, row2(ffn_norm[1]), wg[1], wu[1], wd[1], row2(final_norm), True)

    rows_s, qd_s, kd_s, vd_s = _proj1(hs, row2(attn_norm[1]), w1, tab_s[2], tab_s[3])
    qd = jnp.transpose(qd_s.reshape(_DIFF_ROWS, b, LANES), (1, 0, 2))
    o_d_s = _decode_diff(qd, rows_s[:, None, :], cache_diff, page_table, 8)
    o_d_s = jnp.transpose(o_d_s, (1, 0, 2)).reshape(2 * DIFF_KVH, DIFF_GROUP, b, LANES)
    hs = _mix1(hs, o_d_s, lam_vecs, row2(diff_subln), wo_c)
    y_s = _ffn(hs, row2(ffn_norm[1]), wg[1], wu[1], wd[1], row2(final_norm), True)

    wb_p = min(NSA_WINDOW, t)
    pair = lambda a: a.reshape(a.shape[0], 2, 1, NSA_D)
    new_win_s = jnp.concatenate([state_nsa_win, win_s.reshape(b, 1, 2, 1, NSA_D)], axis=1)[:, ts:]
    return (y_p[None], y_s[:, None], mla_p[None], pair(cmp_p)[None], pair(slc_p)[None], pair(win_p[t - wb_p:])[None],
            rows_p.reshape(1, t, 2, DIFF_KVH, 2 * DIFF_D),
            mla_s[:, None], pair(cmp_s)[:, None], pair(slc_s)[:, None], new_win_s,
            rows_s.reshape(b, 1, 2, DIFF_KVH, 2 * DIFF_D))
```

```python
import functools
import math

import numpy as np
import jax
import jax.numpy as jnp
from jax import lax
from jax.experimental import pallas as pl
from jax.experimental.pallas import tpu as pltpu

F32 = jnp.float32
BF16 = jnp.bfloat16

LANES = 128
VMEM_LIMIT = 56 * 1024 * 1024

ROPE_THETA = 10000.0
NORM_EPS = 1e-6
PAGE = 128
MLA_HEADS = 8
MLA_Q_LORA = 256
MLA_KV_LORA = 128
MLA_NOPE = 64
MLA_ROPE = 32
MLA_V = 64
MLA_SCALE = (MLA_NOPE + MLA_ROPE) ** -0.5
NSA_HEADS = 8
NSA_D = 64
NSA_BLOCK = 64
NSA_TOP_N = 16
NSA_WINDOW = 512
NSA_SCALE = NSA_D ** -0.5
NSA_FORCED = 1.0e4
DIFF_HEADS = 8
DIFF_KVH = 2
DIFF_D = 64
DIFF_GROUP = DIFF_HEADS // DIFF_KVH
DIFF_SCALE = DIFF_D ** -0.5
DIFF_LAMBDA_INIT = 0.8 - 0.6 * math.exp(-0.3 * 1)

NEG_BIG = -1.0e30
SEL_NEG = -float(2 ** 40)


def _cparams(sem):
    return pltpu.CompilerParams(dimension_semantics=sem, vmem_limit_bytes=VMEM_LIMIT)


def _rms(x, g):
    return x * lax.rsqrt(jnp.mean(x * x, axis=-1, keepdims=True) + NORM_EPS) * g


def _dot(a, b):
    return jnp.dot(a, b, preferred_element_type=F32)


def _dot_nt(a, b):
    return lax.dot_general(a, b, (((1,), (1,)), ((), ())), preferred_element_type=F32)


def _lane(shape):
    return lax.broadcasted_iota(jnp.int32, shape, len(shape) - 1)


_S_CQ, _S_CKV, _S_KR, _S_KR_ROT, _S_QN, _S_QN_ROT = 0, 2, 3, 4, 5, 9
_S_CMP, _S_SLC, _S_SLC_ROT, _S_WIN, _S_WIN_ROT, _S_GATE, _N_SLOTS0 = 13, 14, 15, 16, 17, 18, 19


def _proj0_kernel(x_ref, g_ref, w_ref, qn_ref, kvn_ref, uq_ref, ukv_ref, c32_ref, s32_ref, cn_ref, sn_ref,
                  *out_refs, prompt, tm):
    if prompt:
        (mla_ref, cmp_ref, slc_ref, win_ref, qm_ref, qc_ref, qr_ref, gate_ref,
         km_ref, vm_ref, ksa_ref, vs_ref, kw_ref, vw_ref) = out_refs
    else:
        (mla_ref, cmp_ref, slc_ref, win_ref, qm_ref, qc_ref, qr_ref, gate_ref, qa_ref) = out_refs

    hn = _rms(x_ref[...], g_ref[...]).astype(BF16)
    z = _dot(hn, w_ref[...])

    def slot(s, n=1):
        return z[:, s * LANES:(s + n) * LANES]

    c32, s32, cn, sn = c32_ref[...], s32_ref[...], cn_ref[...], sn_ref[...]
    lane = _lane((tm, LANES))
    low = lane < NSA_D

    cqn = _rms(slot(_S_CQ, 2), qn_ref[...]).astype(BF16)
    zq = _dot(cqn, uq_ref[...])
    ckvn = _rms(slot(_S_CKV), kvn_ref[...])
    kpe_hi = slot(_S_KR) * c32 + slot(_S_KR_ROT) * s32
    mla_ref[:, 0:MLA_KV_LORA] = ckvn
    mla_ref[:, MLA_KV_LORA:MLA_KV_LORA + MLA_ROPE] = pltpu.roll(kpe_hi, 64, axis=1)[:, 0:MLA_ROPE]
    ckvb = ckvn.astype(BF16)
    q_cos = jnp.where(low, 1.0, c32)
    nh = MLA_HEADS
    for h in range(nh):
        qh = (zq[:, h * LANES:(h + 1) * LANES] * q_cos
              + zq[:, (nh + h) * LANES:(nh + h + 1) * LANES] * s32) * MLA_SCALE
        qm_ref[h] = qh.astype(BF16)
        if not prompt:
            qlat = _dot(qh.astype(BF16), ukv_ref[h])
            qa_ref[h, :, 0:LANES] = qlat.astype(BF16)
            qa_ref[h, :, LANES:2 * LANES] = jnp.where(lane < MLA_ROPE, pltpu.roll(qh, 64, axis=1), 0.0).astype(BF16)
    if prompt:
        zk = _dot(ckvb, ukv_ref[0])
        zv = _dot(ckvb, ukv_ref[1])
        for h in range(nh):
            km_ref[h] = (zk[:, h * LANES:(h + 1) * LANES] + kpe_hi).astype(BF16)
            vm_ref[h] = zv[:, h * LANES:(h + 1) * LANES].T.astype(BF16)

    for p in range(NSA_HEADS // 2):
        plain = slot(_S_QN + p) * NSA_SCALE
        roped = (slot(_S_QN + p) * cn + slot(_S_QN_ROT + p) * sn) * NSA_SCALE
        qc_ref[2 * p] = jnp.where(low, plain, 0.0).astype(BF16)
        qc_ref[2 * p + 1] = jnp.where(low, 0.0, plain).astype(BF16)
        qr_ref[2 * p] = jnp.where(low, roped, 0.0).astype(BF16)
        qr_ref[2 * p + 1] = jnp.where(low, 0.0, roped).astype(BF16)

    ck = jnp.where(low, cn, 1.0)
    cmp_ref[...] = slot(_S_CMP)
    slc = slot(_S_SLC) * ck + slot(_S_SLC_ROT) * sn
    win = slot(_S_WIN) * ck + slot(_S_WIN_ROT) * sn
    slc_ref[...] = slc
    win_ref[...] = win
    gate_ref[...] = jax.nn.sigmoid(slot(_S_GATE))
    if prompt:
        slc_sw = pltpu.roll(slc, 64, axis=1)
        win_sw = pltpu.roll(win, 64, axis=1)
        row = pl.program_id(0) * tm + lax.broadcasted_iota(jnp.int32, (tm, LANES), 0)
        onehot = ((row >> 6) & (LANES - 1)) == lane
        ksa_ref[:, 0:LANES] = jnp.where(onehot, 1.0, 0.0).astype(BF16)
        ksa_ref[:, LANES:2 * LANES] = jnp.where(low, slc, slc_sw).astype(BF16)
        vs_ref[...] = slc_sw.T.astype(BF16)
        kw_ref[...] = jnp.where(low, win, win_sw).astype(BF16)
        vw_ref[...] = win_sw.T.astype(BF16)


def _rot_cols(w, d):
    k = w.shape[0]
    w3 = w.reshape(k, -1, d)
    return jnp.concatenate([-w3[..., d // 2:], w3[..., :d // 2]], axis=-1).reshape(k, -1)


def _place(w, left, width=LANES):
    return jnp.pad(w, ((0, 0), (left, width - left - w.shape[1])))


def _prep_layer0(w_in_a, w_mla_uq, w_mla_uk, w_mla_uv):
    o = np.cumsum([0, 256, 128, 32, 512, 64, 64, 64, 64, 64, 64, 24]).tolist()
    c_q, c_kv, k_r, q_n = (w_in_a[:, o[i]:o[i + 1]] for i in range(4))
    kv_c = w_in_a[:, o[4]:o[6]]
    kv_s = w_in_a[:, o[6]:o[8]]
    kv_w = w_in_a[:, o[8]:o[10]]
    g = w_in_a[:, o[10]:o[11]]
    z64 = jnp.zeros((w_in_a.shape[0], 64), F32)
    cols = [c_q, c_kv, _place(k_r, 64), _place(_rot_cols(k_r, MLA_ROPE), 64), q_n, _rot_cols(q_n, NSA_D),
            kv_c,
            kv_s, jnp.concatenate([_rot_cols(kv_s[:, :64], 64), z64], axis=1),
            kv_w, jnp.concatenate([_rot_cols(kv_w[:, :64], 64), z64], axis=1),
            _place(g, 0)]
    w0 = jnp.concatenate(cols, axis=1).astype(BF16)
    uq3 = w_mla_uq.reshape(MLA_Q_LORA, MLA_HEADS, MLA_NOPE + MLA_ROPE)
    nope, pe = uq3[..., :MLA_NOPE], uq3[..., MLA_NOPE:]
    pe_rot = jnp.concatenate([-pe[..., MLA_ROPE // 2:], pe[..., :MLA_ROPE // 2]], axis=-1)
    pad32 = jnp.zeros(pe.shape, F32)
    plain = jnp.concatenate([nope, pe, pad32], axis=-1).reshape(MLA_Q_LORA, -1)
    rot = jnp.concatenate([jnp.zeros(nope.shape, F32), pe_rot, pad32], axis=-1).reshape(MLA_Q_LORA, -1)
    uq = jnp.concatenate([plain, rot], axis=1).astype(BF16)
    pad_k = jnp.zeros((MLA_KV_LORA, MLA_HEADS, LANES - MLA_NOPE), F32)
    uk_slots = jnp.concatenate([w_mla_uk, pad_k], axis=-1).reshape(MLA_KV_LORA, -1)
    uv_slots = jnp.concatenate([w_mla_uv, pad_k], axis=-1).reshape(MLA_KV_LORA, -1)
    ukv_prompt = jnp.stack([uk_slots, uv_slots]).astype(BF16)
    ukt = jnp.transpose(w_mla_uk, (1, 2, 0))
    ukt = jnp.concatenate([ukt, jnp.zeros_like(ukt)], axis=1).astype(BF16)
    uv_pad = jnp.transpose(jnp.concatenate([w_mla_uv, pad_k], axis=-1), (1, 0, 2)).astype(BF16)
    return w0, uq, ukv_prompt, ukt, uv_pad


def _rope_tables(pos):
    def cs(dim):
        inv = ROPE_THETA ** (-jnp.arange(0, dim, 2, dtype=F32) / dim)
        ang = pos.astype(F32)[:, None] * inv[None, :]
        ang = jnp.concatenate([ang, ang], axis=-1)
        return jnp.cos(ang), jnp.sin(ang)
    c32, s32 = cs(MLA_ROPE)
    cn, sn = cs(NSA_D)
    return (_place(c32, 64), _place(s32, 64), jnp.concatenate([cn, cn], axis=1), jnp.concatenate([sn, sn], axis=1))


def _proj0(x, gain, w0, qnorm, kvnorm, uq, ukv, tables, prompt):
    t, d = x.shape
    tm = min(t, 256)
    nh = MLA_HEADS
    row = lambda w: pl.BlockSpec((tm, w), lambda i: (i, 0))
    head = lambda w: pl.BlockSpec((nh, tm, w), lambda i: (0, i, 0))
    full = lambda a: pl.BlockSpec(a.shape, lambda i: (0,) * a.ndim)
    out_shape = [jax.ShapeDtypeStruct((t, MLA_KV_LORA + MLA_ROPE), F32)] + [jax.ShapeDtypeStruct((t, LANES), F32)] * 3
    out_specs = [row(MLA_KV_LORA + MLA_ROPE), row(LANES), row(LANES), row(LANES)]
    out_shape += [jax.ShapeDtypeStruct((nh, t, LANES), BF16)] * 3 + [jax.ShapeDtypeStruct((t, LANES), F32)]
    out_specs += [head(LANES)] * 3 + [row(LANES)]
    if prompt:
        col = pl.BlockSpec((LANES, tm), lambda i: (0, i))
        out_shape += [jax.ShapeDtypeStruct((nh, t, LANES), BF16), jax.ShapeDtypeStruct((nh, LANES, t), BF16)]
        out_specs += [head(LANES), pl.BlockSpec((nh, LANES, tm), lambda i: (0, 0, i))]
        out_shape += [jax.ShapeDtypeStruct((t, 2 * LANES), BF16), jax.ShapeDtypeStruct((LANES, t), BF16),
                      jax.ShapeDtypeStruct((t, LANES), BF16), jax.ShapeDtypeStruct((LANES, t), BF16)]
        out_specs += [row(2 * LANES), col, row(LANES), col]
    else:
        out_shape += [jax.ShapeDtypeStruct((nh, t, 2 * LANES), BF16)]
        out_specs += [head(2 * LANES)]
    return pl.pallas_call(
        functools.partial(_proj0_kernel, prompt=prompt, tm=tm),
        grid=(t // tm,),
        in_specs=[row(d), full(gain), full(w0), full(qnorm), full(kvnorm), full(uq), full(ukv)] + [row(LANES)] * 4,
        out_specs=out_specs, out_shape=out_shape,
        compiler_params=_cparams(("parallel",)),
        name="proj0_prompt" if prompt else "proj0_sample",
    )(x, gain, w0, qnorm, kvnorm, uq, ukv, *tables)


_CMP_PAIR = 2


def _compress_rows(src_ref, n_blocks, pe_ref, w1_ref):
    acc = None
    for r0 in range(0, NSA_BLOCK, _CMP_PAIR):
        xs = [src_ref[pl.ds(r0 + u, n_blocks, stride=NSA_BLOCK), :] + pe_ref[r0 + u:r0 + u + 1, :]
              for u in range(_CMP_PAIR)]
        part = _dot(jnp.concatenate(xs, axis=1).astype(BF16), w1_ref[r0 // _CMP_PAIR])
        acc = part if acc is None else acc + part
    return acc


def _compress_prompt_kernel(f_ref, pe_ref, w1_ref, w2_ref, o_ref, *, n_blocks):
    pre = _compress_rows(f_ref, n_blocks, pe_ref, w1_ref)
    o_ref[...] = _dot(jax.nn.silu(pre).astype(BF16), w2_ref[...])


def _compress_paged_kernel(pt_ref, *refs, pages):
    page_refs = refs[:pages]
    pe_ref, w1_ref, w2_ref, o_ref, stage, pre_k, pre_v = refs[pages:]
    ch = pl.program_id(1)
    for p in range(pages):
        stage[PAGE * p:PAGE * (p + 1), :] = page_refs[p][0]
    for kv, pre in enumerate((pre_k, pre_v)):
        acc = None
        for f0 in range(0, NSA_D, _CMP_PAIR):
            xs = [stage[pl.ds(kv * NSA_D + f0 + u, pages, stride=PAGE), :]
                  + pe_ref[kv * NSA_D + f0 + u:kv * NSA_D + f0 + u + 1, :] for u in range(_CMP_PAIR)]
            part = _dot(jnp.concatenate(xs, axis=1).astype(BF16), w1_ref[kv, f0 // _CMP_PAIR])
            acc = part if acc is None else acc + part
        pre[pl.ds(pl.multiple_of(ch * pages, pages), pages), :] = acc

    @pl.when(ch == pl.num_programs(1) - 1)
    def _():
        pk, pv = pre_k[...], pre_v[...]
        low = _lane(pk.shape) < NSA_D
        even = jnp.where(low, pk, pltpu.roll(pv, 64, axis=1))
        odd = jnp.where(low, pltpu.roll(pk, 64, axis=1), pv)
        both = jnp.concatenate([even, odd], axis=0)
        o_ref[0] = _dot(jax.nn.silu(both).astype(BF16), w2_ref[...])


def _prep_compress(pe_k, pe_v, w1_k, w1_v, w2_k, w2_v):
    pe = jnp.concatenate([pe_k, pe_v], axis=1)
    w1k3 = w1_k.reshape(NSA_BLOCK, NSA_D, NSA_D)
    w1v3 = w1_v.reshape(NSA_BLOCK, NSA_D, NSA_D)
    w1 = jnp.zeros((NSA_BLOCK, 2, NSA_D, 2 * NSA_D), F32)
    w1 = w1.at[:, 0, :, :NSA_D].set(w1k3).at[:, 1, :, NSA_D:].set(w1v3)
    w1 = w1.reshape(NSA_BLOCK // _CMP_PAIR, _CMP_PAIR * LANES, 2 * NSA_D).astype(BF16)
    w2 = jnp.zeros((2 * NSA_D, 2 * NSA_D), F32)
    w2 = w2.at[:NSA_D, :NSA_D].set(w2_k).at[NSA_D:, NSA_D:].set(w2_v).astype(BF16)
    pe_t = jnp.tile(jnp.concatenate([pe_k.T, pe_v.T], axis=0), (1, 2))
    def per_dim(w3):
        wd = jnp.transpose(w3, (1, 0, 2))
        z = jnp.zeros_like(wd)
        top = jnp.concatenate([wd, z], axis=2)
        bot = jnp.concatenate([z, wd], axis=2)
        return jnp.concatenate([top, bot], axis=1)
    w1_t = jnp.stack([per_dim(w1k3), per_dim(w1v3)])
    w1_t = w1_t.reshape(2, NSA_D // _CMP_PAIR, _CMP_PAIR * LANES, LANES).astype(BF16)
    return pe, w1, w2, pe_t, w1_t


def _compress_prompt(cmp_rows, pe, w1, w2):
    t = cmp_rows.shape[0]
    nb = t // NSA_BLOCK
    full = lambda a: pl.BlockSpec(a.shape, lambda i: (0,) * a.ndim)
    return pl.pallas_call(
        functools.partial(_compress_prompt_kernel, n_blocks=nb),
        grid=(1,), in_specs=[full(cmp_rows), full(pe), full(w1), full(w2)],
        out_specs=pl.BlockSpec((nb, LANES), lambda i: (0, 0)),
        out_shape=jax.ShapeDtypeStruct((nb, LANES), F32),
        compiler_params=_cparams(("arbitrary",)), name="compress_prompt",
    )(cmp_rows, pe, w1, w2)


def _feature_major(cache):
    n, rows = cache.shape[:2]
    return jnp.transpose(cache, (0, 2, 3, 4, 1)).reshape(n, 2 * NSA_D, rows)


def _compress_paged(cache_cmp, page_table, pe_t, w1_t, w2, pages):
    b, n_pages = page_table.shape
    nb = n_pages * PAGE // NSA_BLOCK
    view = _feature_major(cache_cmp)
    pages = min(pages, n_pages)
    page_specs = [pl.BlockSpec((1, 2 * NSA_D, PAGE),
                               functools.partial(lambda i, c, pt, p: (pt[i, c * pages + p], 0, 0), p=p))
                  for p in range(pages)]
    full = lambda a: pl.BlockSpec(a.shape, lambda i, c, pt: (0,) * a.ndim)
    return pl.pallas_call(
        functools.partial(_compress_paged_kernel, pages=pages),
        grid_spec=pltpu.PrefetchScalarGridSpec(
            num_scalar_prefetch=1, grid=(b, n_pages // pages),
            in_specs=page_specs + [full(pe_t), full(w1_t), full(w2)],
            out_specs=pl.BlockSpec((1, nb, LANES), lambda i, c, pt: (i, 0, 0)),
            scratch_shapes=[pltpu.VMEM((PAGE * pages, LANES), F32), pltpu.VMEM((n_pages, LANES), F32),
                            pltpu.VMEM((n_pages, LANES), F32)]),
        out_shape=jax.ShapeDtypeStruct((b, nb, LANES), F32),
        compiler_params=_cparams(("parallel", "arbitrary")), name="compress_paged",
    )(page_table, *([view] * pages), pe_t, w1_t, w2)


def _block_id(lane, perm_pages):
    if not perm_pages:
        return lane
    assert perm_pages & (perm_pages - 1) == 0
    shift = perm_pages.bit_length() - 1
    return jnp.where(lane < 2 * perm_pages, 2 * (lane & (perm_pages - 1)) + (lane >> shift), lane)


def _cmp_attn_kernel(kv_ref, q_ref, o_ref, imp_ref, *, rows_per_query, tq, pos0, pos_step, nbp, perm_pages):
    kv = kv_ref[...] if len(kv_ref.shape) == 2 else kv_ref[0]
    nb = kv.shape[0]
    sw = pltpu.roll(kv, 64, axis=1)
    kdup = jnp.where(_lane(kv.shape) < NSA_D, kv, sw).astype(BF16)
    vk = sw.astype(BF16)
    if rows_per_query:
        q = q_ref[0]
        qpos = jnp.full((q.shape[0], 1), pos0, jnp.int32)
    else:
        q = q_ref[...].reshape(NSA_HEADS * tq, LANES)
        r = lax.broadcasted_iota(jnp.int32, (NSA_HEADS * tq, 1), 0)
        qpos = pos0 + pos_step * (pl.program_id(0) * tq + (r & (tq - 1)))
    s = _dot_nt(q, kdup)
    last = (_block_id(_lane(s.shape), perm_pages) + 1) * NSA_BLOCK - 1
    s = jnp.where(last <= qpos, s, -jnp.inf)
    m = jnp.max(s, axis=-1, keepdims=True)
    m = jnp.where(m == -jnp.inf, 0.0, m)
    e = jnp.exp(s - m)
    den = jnp.sum(e, axis=-1, keepdims=True)
    p = e / jnp.where(den > 0.0, den, 1.0)
    o = _dot(p.astype(BF16), vk)
    if rows_per_query:
        o_ref[0] = o
        imp = jnp.sum(p, axis=0, keepdims=True)
        if nbp > nb:
            imp = jnp.concatenate([imp, jnp.zeros((1, nbp - nb), F32)], axis=1)
        imp_ref[0] = imp
    else:
        o_ref[...] = o.reshape(NSA_HEADS, tq, LANES)
        imp = jnp.sum(p.reshape(NSA_HEADS, tq, nb), axis=0)
        if nbp > nb:
            imp = jnp.concatenate([imp, jnp.zeros((tq, nbp - nb), F32)], axis=1)
        imp_ref[...] = imp


def _cmp_attn_prompt(kcvc, q_cmp, nbp):
    nh, t, _ = q_cmp.shape
    nb = kcvc.shape[0]
    tq = min(t, 128)
    return pl.pallas_call(
        functools.partial(_cmp_attn_kernel, rows_per_query=False, tq=tq, pos0=0, pos_step=1, nbp=nbp, perm_pages=0),
        grid=(t // tq,),
        in_specs=[pl.BlockSpec((nb, LANES), lambda i: (0, 0)), pl.BlockSpec((nh, tq, LANES), lambda i: (0, i, 0))],
        out_specs=[pl.BlockSpec((nh, tq, LANES), lambda i: (0, i, 0)), pl.BlockSpec((tq, nbp), lambda i: (i, 0))],
        out_shape=[jax.ShapeDtypeStruct((nh, t, LANES), F32), jax.ShapeDtypeStruct((t, nbp), F32)],
        compiler_params=_cparams(("parallel",)), name="cmp_attn_prompt",
    )(kcvc, q_cmp)


def _cmp_attn_sample(kcvc, q_cmp, pos, nbp, perm_pages):
    b, nh, _ = q_cmp.shape
    nb = kcvc.shape[1]
    return pl.pallas_call(
        functools.partial(_cmp_attn_kernel, rows_per_query=True, tq=1, pos0=pos, pos_step=0, nbp=nbp,
                          perm_pages=perm_pages),
        grid=(b,),
        in_specs=[pl.BlockSpec((1, nb, LANES), lambda i: (i, 0, 0)), pl.BlockSpec((1, nh, LANES), lambda i: (i, 0, 0))],
        out_specs=[pl.BlockSpec((1, nh, LANES), lambda i: (i, 0, 0)), pl.BlockSpec((1, 1, nbp), lambda i: (i, 0, 0))],
        out_shape=[jax.ShapeDtypeStruct((b, nh, LANES), F32), jax.ShapeDtypeStruct((b, 1, nbp), F32)],
        compiler_params=_cparams(("parallel",)), name="cmp_attn_sample",
    )(kcvc, q_cmp)


def _select_kernel(imp_ref, *refs, tq, pos0, pos_step, nb_sel, n_halves, with_query, perm_pages):
    if with_query:
        q_ref, qa_ref = refs
    else:
        idx_ref, val_ref = refs
    imp = imp_ref[...]
    nbp = imp.shape[1]
    lane = _block_id(_lane(imp.shape), perm_pages)
    qpos = pos0 + pos_step * (pl.program_id(0) * tq + lax.broadcasted_iota(jnp.int32, (tq, 1), 0))
    cur = qpos >> 6
    forced = (lane == 0) | (lane == cur) | (lane == cur - 1)
    score = jnp.where(forced, NSA_FORCED, imp)
    work = jnp.where((lane <= cur) & (lane < nb_sel), score, -jnp.inf)
    lane_f = lane.astype(F32)
    neg = jnp.full(imp.shape, SEL_NEG, F32)
    lane16 = _lane((tq, LANES))
    idx_acc = jnp.zeros((tq, LANES), jnp.int32)
    val_acc = jnp.zeros((tq, LANES), jnp.int32)
    for n in range(min(NSA_TOP_N, nb_sel)):
        mx = jnp.max(work, axis=-1, keepdims=True)
        first = jnp.min(jnp.where(work == mx, lane_f, float(nbp)), axis=-1, keepdims=True)
        pick = lane_f == first
        valid_neg = jnp.where(mx > -jnp.inf, 0.0, SEL_NEG)
        neg = jnp.where(pick, jnp.maximum(neg, valid_neg), neg)
        work = jnp.where(pick, -jnp.inf, work)
        if not with_query:
            idx_acc = jnp.where(lane16 == n, first.astype(jnp.int32), idx_acc)
            val_acc = jnp.where(lane16 == n, jnp.where(mx > -jnp.inf, 1, 0), val_acc)
    if with_query:
        neg = neg.astype(BF16)
        for hf in range(n_halves):
            part = neg[:, hf * LANES:(hf + 1) * LANES]
            for h in range(NSA_HEADS):
                qa_ref[hf, h, :, 0:LANES] = part
                qa_ref[hf, h, :, LANES:2 * LANES] = q_ref[h]
    else:
        idx_ref[...] = idx_acc
        val_ref[...] = val_acc


def _select_prompt(imp, q_rot):
    t, nbp = imp.shape
    nh = q_rot.shape[0]
    tq = min(t, 128)
    n_halves = nbp // LANES
    return pl.pallas_call(
        functools.partial(_select_kernel, tq=tq, pos0=0, pos_step=1, nb_sel=t // NSA_BLOCK, n_halves=n_halves,
                          with_query=True, perm_pages=0),
        grid=(t // tq,),
        in_specs=[pl.BlockSpec((tq, nbp), lambda i: (i, 0)), pl.BlockSpec((nh, tq, LANES), lambda i: (0, i, 0))],
        out_specs=pl.BlockSpec((n_halves, nh, tq, 2 * LANES), lambda i: (0, 0, i, 0)),
        out_shape=jax.ShapeDtypeStruct((n_halves, nh, t, 2 * LANES), BF16),
        compiler_params=_cparams(("parallel",)), name="select_prompt",
    )(imp, q_rot)


def _select_sample(imp, pos, nb_sel, perm_pages):
    b, nbp = imp.shape
    return pl.pallas_call(
        functools.partial(_select_kernel, tq=b, pos0=pos, pos_step=0, nb_sel=nb_sel, n_halves=0, with_query=False,
                          perm_pages=perm_pages),
        grid=(1,),
        in_specs=[pl.BlockSpec((b, nbp), lambda i: (0, 0))],
        out_specs=[pl.BlockSpec((b, LANES), lambda i: (0, 0))] * 2,
        out_shape=[jax.ShapeDtypeStruct((b, LANES), jnp.int32)] * 2,
        compiler_params=_cparams(("arbitrary",)), name="select_sample",
    )(imp)


_FLASH_CQ = 256


def _flash_kernel(q_ref, k_ref, vt_ref, o_ref, m_scr, l_scr, acc_scr, *, r, tq, tk, window, n_j):
    i, j = pl.program_id(1), pl.program_id(2)
    m_rows = r * tq
    cq = min(_FLASH_CQ, m_rows)

    @pl.when(j == 0)
    def _():
        m_scr[...] = jnp.full(m_scr.shape, NEG_BIG, F32)
        l_scr[...] = jnp.zeros(l_scr.shape, F32)
        acc_scr[...] = jnp.zeros(acc_scr.shape, F32)

    if window:
        kv_blk = i - (n_j - 1) + j
        active = kv_blk >= 0
        need_mask = None
    else:
        kv_blk = j
        active = j * tk <= i * tq + (tq - 1)
        need_mask = j * tk + (tk - 1) > i * tq

    def q_chunk(c):
        if tq >= cq:
            per_head = tq // cq
            return q_ref[0, c // per_head, (c % per_head) * cq:(c % per_head + 1) * cq, :]
        heads = cq // tq
        return q_ref[0, c * heads:(c + 1) * heads].reshape(cq, q_ref.shape[-1])

    def scores(c, masked):
        s = _dot_nt(k_ref[0], q_chunk(c))
        if masked:
            qpos = i * tq + ((c * cq + _lane((tk, cq))) & (tq - 1))
            kpos = kv_blk * tk + lax.broadcasted_iota(jnp.int32, (tk, cq), 0)
            ok = kpos <= qpos
            if window:
                ok = ok & (qpos - kpos < window)
            s = jnp.where(ok, s, NEG_BIG)
        return s

    def step(masked):
        m_all, l_all = m_scr[...], l_scr[...]
        n_chunks = m_rows // cq
        s_next = scores(0, masked)
        for c in range(n_chunks):
            cols = slice(c * cq, (c + 1) * cq)
            s = s_next
            if c + 1 < n_chunks:
                s_next = scores(c + 1, masked)
            m_prev = m_all[:, cols]
            m_new = jnp.maximum(m_prev, jnp.max(s, axis=0, keepdims=True))
            alpha = jnp.exp(m_prev - m_new)
            p = jnp.exp(s - m_new)
            l_scr[:, cols] = alpha * l_all[:, cols] + jnp.sum(p, axis=0, keepdims=True)
            acc_scr[:, cols] = alpha * acc_scr[:, cols] + _dot(vt_ref[0], p.astype(BF16))
            m_scr[:, cols] = m_new

    if window:
        pl.when(active)(lambda: step(True))
    else:
        pl.when(active & need_mask)(lambda: step(True))
        pl.when(active & jnp.logical_not(need_mask))(lambda: step(False))

    @pl.when(j == n_j - 1)
    def _():
        o = (acc_scr[...] / l_scr[...]).T
        o_ref[0] = o.reshape(r, tq, o.shape[-1]).astype(o_ref.dtype)


def _flash(q, k, vt, *, tq, tk, groups, q_index, kv_index, window=0, out_dtype=F32, name):
    r, t, dq = q.shape[1:]
    dv = vt.shape[1]
    tq, tk = min(tq, t), min(tk, t)
    if window:
        assert tq == tk and window % tk == 0
        n_j = window // tk + 1
        kv_blk = lambda i, j: jnp.maximum(i - (n_j - 1) + j, 0)
    else:
        n_j = t // tk
        kv_blk = lambda i, j: jnp.minimum(j, (i * tq + tq - 1) // tk)
    return pl.pallas_call(
        functools.partial(_flash_kernel, r=r, tq=tq, tk=tk, window=window, n_j=n_j),
        grid=(groups, t // tq, n_j),
        in_specs=[pl.BlockSpec((1, r, tq, dq), lambda g, i, j: (q_index(g, kv_blk(i, j) * tk), 0, i, 0)),
                  pl.BlockSpec((1, tk, dq), lambda g, i, j: (kv_index(g), kv_blk(i, j), 0)),
                  pl.BlockSpec((1, dv, tk), lambda g, i, j: (kv_index(g), 0, kv_blk(i, j)))],
        out_specs=pl.BlockSpec((1, r, tq, dv), lambda g, i, j: (g, 0, i, 0)),
        out_shape=jax.ShapeDtypeStruct((groups, r, t, dv), out_dtype),
        scratch_shapes=[pltpu.VMEM((1, r * tq), F32), pltpu.VMEM((1, r * tq), F32), pltpu.VMEM((dv, r * tq), F32)],
        compiler_params=_cparams(("parallel", "parallel", "arbitrary")), name=name,
    )(q, k, vt)


def _online_softmax_step(s, m_scr, l_scr):
    m_prev = m_scr[...]
    m_new = jnp.maximum(m_prev, jnp.max(s, axis=-1, keepdims=True))
    alpha = jnp.exp(m_prev - m_new)
    p = jnp.exp(s - m_new)
    l_scr[...] = alpha * l_scr[...] + jnp.sum(p, axis=-1, keepdims=True)
    m_scr[...] = m_new
    return alpha, p


def _decode_mla_kernel(pt_ref, q_ref, new_ref, *refs, pages):
    page_refs = refs[:pages]
    o_ref, m_scr, l_scr, acc_scr = refs[pages:]
    c = pl.program_id(1)
    q = q_ref[0]

    @pl.when(c == 0)
    def _():
        new = new_ref[0].astype(BF16).astype(F32)
        m_scr[...] = jnp.sum(q.astype(F32) * new, axis=-1, keepdims=True)
        l_scr[...] = jnp.ones(l_scr.shape, F32)
        acc_scr[...] = jnp.broadcast_to(new[:, 0:MLA_KV_LORA], acc_scr.shape)

    kt = jnp.concatenate([pr[0] for pr in page_refs], axis=1).astype(BF16)
    alpha, p = _online_softmax_step(_dot(q, kt), m_scr, l_scr)
    acc_scr[...] = alpha * acc_scr[...] + _dot_nt(p.astype(BF16), kt[0:MLA_KV_LORA])

    @pl.when(c == pl.num_programs(1) - 1)
    def _():
        o_ref[0] = acc_scr[...] / l_scr[...]


def _decode_mla(q, new_rows, cache_mla, page_table, pages):
    b, rows, kw = q.shape
    n_pages = page_table.shape[1]
    pages = min(pages, n_pages)
    view = jnp.transpose(cache_mla, (0, 2, 1))
    page_specs = [pl.BlockSpec((1, kw, PAGE), functools.partial(lambda i, c, pt, p: (pt[i, c * pages + p], 0, 0), p=p))
                  for p in range(pages)]
    return pl.pallas_call(
        functools.partial(_decode_mla_kernel, pages=pages),
        grid_spec=pltpu.PrefetchScalarGridSpec(
            num_scalar_prefetch=1, grid=(b, n_pages // pages),
            in_specs=[pl.BlockSpec((1, rows, kw), lambda i, c, pt: (i, 0, 0)),
                      pl.BlockSpec((1, 1, kw), lambda i, c, pt: (i, 0, 0))] + page_specs,
            out_specs=pl.BlockSpec((1, rows, MLA_KV_LORA), lambda i, c, pt: (i, 0, 0)),
            scratch_shapes=[pltpu.VMEM((rows, 1), F32), pltpu.VMEM((rows, 1), F32),
                            pltpu.VMEM((rows, MLA_KV_LORA), F32)]),
        out_shape=jax.ShapeDtypeStruct((b, rows, MLA_KV_LORA), F32),
        compiler_params=_cparams(("parallel", "arbitrary")), name="decode_mla",
    )(page_table, q, new_rows, *([view] * pages))


_DIFF_ROWS = 2 * DIFF_KVH * DIFF_GROUP
_DIFF_COLS = 2 * DIFF_KVH


def _decode_diff_kernel(pt_ref, q_ref, new_ref, *refs, pages):
    page_refs = refs[:pages]
    o_ref, m_scr, l_scr, acc_scr = refs[pages:]
    c = pl.program_id(1)
    q = q_ref[0]
    head = lax.broadcasted_iota(jnp.int32, (_DIFF_ROWS, 1), 0) >> 3

    @pl.when(c == 0)
    def _():
        new = new_ref[0].astype(BF16).astype(F32)
        knew = jnp.where(head == 0, new[:, 0:LANES], new[:, LANES:2 * LANES])
        vnew = jnp.where(head == 0, new[:, 2 * LANES:3 * LANES], new[:, 3 * LANES:4 * LANES])
        m_scr[...] = jnp.sum(q.astype(F32) * knew, axis=-1, keepdims=True)
        l_scr[...] = jnp.ones(l_scr.shape, F32)
        acc_scr[...] = vnew

    def cache_rows(c):
        return jnp.concatenate([pr[0, pl.ds(c, PAGE, stride=_DIFF_COLS), :] for pr in page_refs],
                               axis=0).astype(BF16)

    rows_q = _DIFF_ROWS // DIFF_KVH
    qf = q.astype(F32)
    s = jnp.concatenate([_dot_nt(qf[g * rows_q:(g + 1) * rows_q].astype(BF16), cache_rows(g))
                         for g in range(DIFF_KVH)], axis=0)
    alpha, p = _online_softmax_step(s, m_scr, l_scr)
    pv = jnp.concatenate([_dot(p[g * rows_q:(g + 1) * rows_q].astype(BF16), cache_rows(DIFF_KVH + g))
                          for g in range(DIFF_KVH)], axis=0)
    acc_scr[...] = alpha * acc_scr[...] + pv

    @pl.when(c == pl.num_programs(1) - 1)
    def _():
        o_ref[0] = acc_scr[...] / l_scr[...]


def _decode_diff(q, new_rows, cache_diff, page_table, pages):
    b, rows, _ = q.shape
    n_pool = cache_diff.shape[0]
    n_pages = page_table.shape[1]
    pages = min(pages, n_pages)
    view = cache_diff.reshape(n_pool, PAGE * _DIFF_COLS, LANES)
    page_specs = [pl.BlockSpec((1, PAGE * _DIFF_COLS, LANES),
                               functools.partial(lambda i, c, pt, p: (pt[i, c * pages + p], 0, 0), p=p))
                  for p in range(pages)]
    return pl.pallas_call(
        functools.partial(_decode_diff_kernel, pages=pages),
        grid_spec=pltpu.PrefetchScalarGridSpec(
            num_scalar_prefetch=1, grid=(b, n_pages // pages),
            in_specs=[pl.BlockSpec((1, rows, LANES), lambda i, c, pt: (i, 0, 0)),
                      pl.BlockSpec((1, 1, _DIFF_COLS * LANES), lambda i, c, pt: (i, 0, 0))] + page_specs,
            out_specs=pl.BlockSpec((1, rows, LANES), lambda i, c, pt: (i, 0, 0)),
            scratch_shapes=[pltpu.VMEM((rows, 1), F32), pltpu.VMEM((rows, 1), F32), pltpu.VMEM((rows, LANES), F32)]),
        out_shape=jax.ShapeDtypeStruct((b, rows, LANES), F32),
        compiler_params=_cparams(("parallel", "arbitrary")), name="decode_diff",
    )(page_table, q, new_rows, *([view] * pages))


def _nsa_decode_kernel(idx_ref, val_ref, pt_ref, q_ref, news_ref, neww_ref, win_ref, *refs, n_sel, nb_cached, pos,
                       win_pos0):
    blk_refs = refs[:n_sel]
    os_ref, ow_ref = refs[n_sel:]
    b = pl.program_id(0)
    q = q_ref[0]
    qf = q.astype(F32)

    def new_key_score(new_ref):
        row = new_ref[0].astype(BF16).astype(F32)
        kd = jnp.where(_lane(row.shape) < NSA_D, row, pltpu.roll(row, 64, axis=1))
        vnew = jnp.where(_lane(row.shape) < NSA_D, pltpu.roll(row, 64, axis=1), 0.0)
        return jnp.sum(qf * kd, axis=-1, keepdims=True), vnew

    def attend(ktv, ok, s_new, new_ok, vnew):
        ktv = ktv.astype(BF16)
        s = _dot(q, jnp.concatenate([ktv[0:NSA_D], ktv[0:NSA_D]], axis=0))
        s = jnp.where(ok, s, -jnp.inf)
        if new_ok is not True:
            s_new = jnp.where(new_ok, s_new, -jnp.inf)
        m = jnp.maximum(jnp.max(s, axis=-1, keepdims=True), s_new)
        m = jnp.where(m == -jnp.inf, 0.0, m)
        e = jnp.exp(s - m)
        e_new = jnp.exp(s_new - m)
        den = jnp.sum(e, axis=-1, keepdims=True) + e_new
        den = jnp.where(den > 0.0, den, 1.0)
        o = _dot_nt(e.astype(BF16), ktv)
        return (pltpu.roll(o, 64, axis=1) + e_new * vnew) / den

    ktv = jnp.concatenate([r[0] for r in blk_refs], axis=1)
    lane = _lane((NSA_HEADS, n_sel * PAGE))
    page_of, half_of = lane >> 7, (lane >> 6) & 1
    ok = jnp.zeros(lane.shape, F32)
    new_ok = jnp.int32(0)
    for n in range(n_sel):
        valid = val_ref[b * n_sel + n] > 0
        blk = idx_ref[b * n_sel + n]
        use = jnp.where(valid & (blk < nb_cached), 1.0, 0.0)
        ok = jnp.where(page_of == n, jnp.where(half_of == (blk & 1), use, 0.0), ok)
        new_ok = new_ok | jnp.where(valid & (blk == nb_cached), 1, 0)
    s_new, vnew = new_key_score(news_ref)
    os_ref[0] = attend(ktv, ok > 0.0, s_new, new_ok > 0, vnew)

    wpos = win_pos0 + _lane((NSA_HEADS, win_ref.shape[-1]))
    d = pos - wpos
    okw = (d >= 0) & (d < NSA_WINDOW) & (wpos >= 0)
    sw_new, vwnew = new_key_score(neww_ref)
    ow_ref[0] = attend(win_ref[0], okw, sw_new, True, vwnew)


def _nsa_decode(idx, val, page_table, q_rot, new_slc, new_win, state_win, cache_slc, pos):
    b = q_rot.shape[0]
    n_pages = page_table.shape[1]
    nb_cached = n_pages * PAGE // NSA_BLOCK
    n_sel = min(NSA_TOP_N, nb_cached + 1)
    wb = state_win.shape[1]
    blocks = _feature_major(cache_slc)
    state_win = _feature_major(state_win)

    def blk_index(i, idx_r, val_r, pt_r, n):
        blk = jnp.minimum(idx_r[i * n_sel + n], nb_cached - 1)
        return (pt_r[i, blk >> 1], 0, 0)

    blk_specs = [pl.BlockSpec((1, 2 * NSA_D, PAGE), functools.partial(blk_index, n=n)) for n in range(n_sel)]
    per_seq = lambda shp: pl.BlockSpec((1,) + shp, lambda i, a, c, d: (i, 0, 0))
    return pl.pallas_call(
        functools.partial(_nsa_decode_kernel, n_sel=n_sel, nb_cached=nb_cached, pos=pos, win_pos0=pos - wb),
        grid_spec=pltpu.PrefetchScalarGridSpec(
            num_scalar_prefetch=3, grid=(b,),
            in_specs=[per_seq((NSA_HEADS, LANES)), per_seq((1, LANES)), per_seq((1, LANES)), per_seq((2 * NSA_D, wb))]
            + blk_specs,
            out_specs=[per_seq((NSA_HEADS, LANES))] * 2),
        out_shape=[jax.ShapeDtypeStruct((b, NSA_HEADS, LANES), F32)] * 2,
        compiler_params=_cparams(("parallel",)), name="nsa_decode",
    )(idx[:, :n_sel].reshape(-1), val[:, :n_sel].reshape(-1), page_table, q_rot, new_slc, new_win, state_win, *([blocks] * n_sel))


def _mix0_kernel(x_ref, om_ref, oc_ref, os_ref, ow_ref, g_ref, w_ref, *refs, absorbed):
    if absorbed:
        uv_ref, o_ref = refs
    else:
        (o_ref,) = refs
    gates = g_ref[...]
    acc = x_ref[...]
    for h in range(MLA_HEADS):
        om = om_ref[h]
        if absorbed:
            om = _dot(om.astype(BF16), uv_ref[h])
        acc = acc + _dot(om.astype(BF16), w_ref[h])
    for h in range(NSA_HEADS):
        gc, gs, gw = (gates[:, k * NSA_HEADS + h:k * NSA_HEADS + h + 1] for k in range(3))
        on = gc * oc_ref[h] + gs * os_ref[h] + gw * ow_ref[h]
        acc = acc + _dot(on.astype(BF16), w_ref[MLA_HEADS + h])
    o_ref[...] = acc


def _mix0(x, o_mla, o_c, o_s, o_w, gates, w_out, uv=None):
    t, d = x.shape
    tm = min(t, 512)
    row = lambda w: pl.BlockSpec((tm, w), lambda i: (i, 0))
    head = pl.BlockSpec((NSA_HEADS, tm, LANES), lambda i: (0, i, 0))
    full = lambda a: pl.BlockSpec(a.shape, lambda i: (0,) * a.ndim)
    ins = [x, o_mla, o_c, o_s, o_w, gates, w_out] + ([uv] if uv is not None else [])
    specs = [row(d), head, head, head, head, row(LANES), full(w_out)] + ([full(uv)] if uv is not None else [])
    return pl.pallas_call(
        functools.partial(_mix0_kernel, absorbed=uv is not None),
        grid=(t // tm,), in_specs=specs, out_specs=row(d), out_shape=jax.ShapeDtypeStruct((t, d), F32),
        compiler_params=_cparams(("parallel",)), name="mix0",
    )(*ins)


def _ffn_kernel(x_ref, g_ref, wg_ref, wu_ref, wd_ref, fg_ref, o_ref, hn_scr, acc_scr, *, final_norm):
    f = pl.program_id(1)

    @pl.when(f == 0)
    def _():
        hn_scr[...] = _rms(x_ref[...], g_ref[...]).astype(BF16)
        acc_scr[...] = x_ref[...]

    hn = hn_scr[...]
    act = jax.nn.silu(_dot(hn, wg_ref[...])) * _dot(hn, wu_ref[...])
    acc_scr[...] += _dot(act.astype(BF16), wd_ref[...])

    @pl.when(f == pl.num_programs(1) - 1)
    def _():
        y = acc_scr[...]
        o_ref[...] = _rms(y, fg_ref[...]) if final_norm else y


def _ffn(x, gain, wg, wu, wd, final_gain, final_norm):
    t, d = x.shape
    dff = wg.shape[1]
    tm = min(t, 512)
    tf = dff // 2 if (dff // 2) % LANES == 0 else dff
    return pl.pallas_call(
        functools.partial(_ffn_kernel, final_norm=final_norm),
        grid=(t // tm, dff // tf),
        in_specs=[pl.BlockSpec((tm, d), lambda i, f: (i, 0)), pl.BlockSpec((1, d), lambda i, f: (0, 0)),
                  pl.BlockSpec((d, tf), lambda i, f: (0, f)), pl.BlockSpec((d, tf), lambda i, f: (0, f)),
                  pl.BlockSpec((tf, d), lambda i, f: (f, 0)), pl.BlockSpec((1, d), lambda i, f: (0, 0))],
        out_specs=pl.BlockSpec((tm, d), lambda i, f: (i, 0)),
        out_shape=jax.ShapeDtypeStruct((t, d), F32),
        scratch_shapes=[pltpu.VMEM((tm, d), BF16), pltpu.VMEM((tm, d), F32)],
        compiler_params=_cparams(("parallel", "arbitrary")), name="ffn",
    )(x, gain, wg, wu, wd, final_gain)


_N_QSLOT = DIFF_HEADS


def _proj1_kernel(x_ref, g_ref, w_ref, cn_ref, sn_ref, rows_ref, q_ref, k_ref, v_ref, *, tm):
    hn = _rms(x_ref[...], g_ref[...]).astype(BF16)
    z = _dot(hn, w_ref[...])
    cn, sn = cn_ref[...], sn_ref[...]
    low = _lane((tm, LANES)) < DIFF_D

    def slot(s):
        return z[:, s * LANES:(s + 1) * LANES]

    for g in range(DIFF_KVH):
        for r in range(DIFF_GROUP):
            h = g * DIFF_GROUP + r
            qp = (slot(h) * cn + slot(_N_QSLOT + h) * sn) * DIFF_SCALE
            q_ref[2 * g, r] = jnp.where(low, qp, 0.0).astype(BF16)
            q_ref[2 * g + 1, r] = jnp.where(low, 0.0, qp).astype(BF16)
        k = slot(2 * _N_QSLOT + g) * cn + slot(2 * _N_QSLOT + DIFF_KVH + g) * sn
        v = slot(2 * _N_QSLOT + 2 * DIFF_KVH + g)
        rows_ref[:, g * LANES:(g + 1) * LANES] = k
        rows_ref[:, (DIFF_KVH + g) * LANES:(DIFF_KVH + g + 1) * LANES] = v
        k_ref[g] = k.astype(BF16)
        v_ref[g] = v.T.astype(BF16)


def _prep_layer1(w_in_c):
    nq = DIFF_HEADS * 2 * DIFF_D
    nk = DIFF_KVH * 2 * DIFF_D
    wq, wk, wv = w_in_c[:, :nq], w_in_c[:, nq:nq + nk], w_in_c[:, nq + nk:]
    return jnp.concatenate([wq, _rot_cols(wq, DIFF_D), wk, _rot_cols(wk, DIFF_D), wv], axis=1).astype(BF16)


def _proj1(x, gain, w1, cn, sn):
    t, d = x.shape
    tm = min(t, 256)
    row = lambda w: pl.BlockSpec((tm, w), lambda i: (i, 0))
    full = lambda a: pl.BlockSpec(a.shape, lambda i: (0,) * a.ndim)
    ng = 2 * DIFF_KVH
    return pl.pallas_call(
        functools.partial(_proj1_kernel, tm=tm),
        grid=(t // tm,),
        in_specs=[row(d), full(gain), full(w1), row(LANES), row(LANES)],
        out_specs=[row(2 * DIFF_KVH * LANES),
                   pl.BlockSpec((ng, DIFF_GROUP, tm, LANES), lambda i: (0, 0, i, 0)),
                   pl.BlockSpec((DIFF_KVH, tm, LANES), lambda i: (0, i, 0)),
                   pl.BlockSpec((DIFF_KVH, LANES, tm), lambda i: (0, 0, i))],
        out_shape=[jax.ShapeDtypeStruct((t, 2 * DIFF_KVH * LANES), F32),
                   jax.ShapeDtypeStruct((ng, DIFF_GROUP, t, LANES), BF16),
                   jax.ShapeDtypeStruct((DIFF_KVH, t, LANES), BF16),
                   jax.ShapeDtypeStruct((DIFF_KVH, LANES, t), BF16)],
        compiler_params=_cparams(("parallel",)), name="proj1",
    )(x, gain, w1, cn, sn)


def _mix1_kernel(x_ref, o_ref_in, lam_ref, sub_ref, w_ref, o_ref):
    lv = lam_ref[...]
    lam = (jnp.exp(jnp.sum(lv[0:1] * lv[1:2], axis=-1, keepdims=True))
           - jnp.exp(jnp.sum(lv[2:3] * lv[3:4], axis=-1, keepdims=True)) + DIFF_LAMBDA_INIT)
    acc = x_ref[...]
    for g in range(DIFF_KVH):
        for r in range(DIFF_GROUP):
            o = o_ref_in[2 * g, r] - lam * o_ref_in[2 * g + 1, r]
            o = _rms(o, sub_ref[...]) * (1.0 - DIFF_LAMBDA_INIT)
            acc = acc + _dot(o.astype(BF16), w_ref[g * DIFF_GROUP + r])
    o_ref[...] = acc


def _mix1(x, o, lam_vecs, subln, w_out):
    t, d = x.shape
    tm = min(t, 512)
    full = lambda a: pl.BlockSpec(a.shape, lambda i: (0,) * a.ndim)
    return pl.pallas_call(
        _mix1_kernel, grid=(t // tm,),
        in_specs=[pl.BlockSpec((tm, d), lambda i: (i, 0)),
                  pl.BlockSpec((2 * DIFF_KVH, DIFF_GROUP, tm, LANES), lambda i: (0, 0, i, 0)),
                  full(lam_vecs), full(subln), full(w_out)],
        out_specs=pl.BlockSpec((tm, d), lambda i: (i, 0)), out_shape=jax.ShapeDtypeStruct((t, d), F32),
        compiler_params=_cparams(("parallel",)), name="mix1",
    )(x, o, lam_vecs, subln, w_out)


def kernel(x_prompt, x_sample, cache_mla, cache_nsa_cmp, cache_nsa_slc, state_nsa_win, cache_diff, page_table,
           attn_norm, ffn_norm, final_norm,
           w_in_a, mla_q_norm, mla_kv_norm, w_mla_uq, w_mla_uk, w_mla_uv,
           nsa_pe_k, nsa_pe_v, nsa_cmp_w1_k, nsa_cmp_w2_k, nsa_cmp_w1_v, nsa_cmp_w2_v, w_out_a,
           w_in_c, diff_lambda_q1, diff_lambda_k1, diff_lambda_q2, diff_lambda_k2, diff_subln, w_out_c,
           w_ffn_gate, w_ffn_up, w_ffn_down):
    bp, t, d = x_prompt.shape
    b, ts, _ = x_sample.shape
    assert bp == 1 and ts == 1 and t % NSA_BLOCK == 0
    n_pool = cache_mla.shape[0]
    past = page_table.shape[1] * PAGE
    assert past % NSA_BLOCK == 0
    nh = MLA_HEADS

    w0, uq, ukv_prompt, ukt, uv_pad = _prep_layer0(w_in_a, w_mla_uq, w_mla_uk, w_mla_uv)
    pe, w1c, w2c, pe_t, w1c_t = _prep_compress(nsa_pe_k, nsa_pe_v, nsa_cmp_w1_k, nsa_cmp_w1_v, nsa_cmp_w2_k,
                                               nsa_cmp_w2_v)
    w1 = _prep_layer1(w_in_c)
    wo_a = w_out_a.reshape(2 * nh, NSA_D, d)
    wo_a = jnp.concatenate([wo_a, jnp.zeros_like(wo_a)], axis=1).astype(BF16)
    wo_c = w_out_c.reshape(DIFF_HEADS, 2 * DIFF_D, d).astype(BF16)
    wg, wu, wd = (w.astype(BF16) for w in (w_ffn_gate, w_ffn_up, w_ffn_down))
    lam_vecs = jnp.stack([diff_lambda_q1, diff_lambda_k1, diff_lambda_q2, diff_lambda_k2])
    row2 = lambda v: v.reshape(1, -1)

    xp = x_prompt[0]
    xs = x_sample[:, 0]
    tab_p = _rope_tables(jnp.arange(t, dtype=jnp.int32))
    tab_s = _rope_tables(jnp.full((b,), past, jnp.int32))

    (mla_p, cmp_p, slc_p, win_p, qm_p, qc_p, qr_p, gate_p, km_p, vm_p, ksa_p, vs_p, kw_p, vw_p) = _proj0(
        xp, row2(attn_norm[0]), w0, row2(mla_q_norm), row2(mla_kv_norm), uq, ukv_prompt, tab_p, True)
    o_mla_p = _flash(qm_p.reshape(nh, 1, t, LANES), km_p, vm_p, tq=2048, tk=1024, groups=nh,
                     q_index=lambda g, k0: g, kv_index=lambda g: g, out_dtype=BF16, name="flash_mla")[:, 0]
    nb_p = t // NSA_BLOCK
    nbp_p = -(-nb_p // LANES) * LANES
    kcvc_p = _compress_prompt(cmp_p, pe, w1c, w2c)
    o_c_p, imp_p = _cmp_attn_prompt(kcvc_p, qc_p, nbp_p)
    q_aug = _select_prompt(imp_p, qr_p)
    keys_per_half = LANES * NSA_BLOCK
    o_s_p = _flash(q_aug, ksa_p[None], vs_p[None], tq=256, tk=1024, groups=1,
                   q_index=lambda g, k0: k0 // keys_per_half, kv_index=lambda g: 0, name="flash_slc")[0]
    o_w_p = _flash(qr_p[None], kw_p[None], vw_p[None], tq=128, tk=128, groups=1, window=NSA_WINDOW,
                   q_index=lambda g, k0: 0, kv_index=lambda g: 0, name="flash_win")[0]
    hp = _mix0(xp, o_mla_p, o_c_p, o_s_p, o_w_p, gate_p, wo_a)
    hp = _ffn(hp, row2(ffn_norm[0]), wg[0], wu[0], wd[0], row2(final_norm), False)

    (mla_s, cmp_s, slc_s, win_s, qm_s, qc_s, qr_s, gate_s, qa_s) = _proj0(
        xs, row2(attn_norm[0]), w0, row2(mla_q_norm), row2(mla_kv_norm), uq, ukt, tab_s, False)
    q_abs = jnp.transpose(qa_s[:, :, :MLA_KV_LORA + MLA_ROPE], (1, 0, 2))
    o_lat = _decode_mla(q_abs, mla_s[:, None, :], cache_mla, page_table, 32)
    n_pages = page_table.shape[1]
    nb_sel_s = -(-(past + ts) // NSA_BLOCK)
    nbp_s = -(-nb_sel_s // LANES) * LANES
    kcvc_s = _compress_paged(cache_nsa_cmp, page_table, pe_t, w1c_t, w2c, 32)
    o_c_s, imp_s = _cmp_attn_sample(kcvc_s, jnp.transpose(qc_s, (1, 0, 2)), past, nbp_s, n_pages)
    idx_s, val_s = _select_sample(imp_s[:, 0], past, nb_sel_s, n_pages)
    o_s_s, o_w_s = _nsa_decode(idx_s, val_s, page_table, jnp.transpose(qr_s, (1, 0, 2)), slc_s[:, None, :],
                               win_s[:, None, :], state_nsa_win, cache_nsa_slc, past)
    tr = lambda a: jnp.transpose(a, (1, 0, 2))
    hs = _mix0(xs, tr(o_lat), tr(o_c_s), tr(o_s_s), tr(o_w_s), gate_s, wo_a, uv_pad)
    hs = _ffn(hs, row2(ffn_norm[0]), wg[0], wu[0], wd[0], row2(final_norm), False)

    rows_p, qd_p, kd_p, vd_p = _proj1(hp, row2(attn_norm[1]), w1, tab_p[2], tab_p[3])
    o_d_p = _flash(qd_p, kd_p, vd_p, tq=512, tk=1024, groups=2 * DIFF_KVH,
                   q_index=lambda g, k0: g, kv_index=lambda g: g // 2, name="flash_diff")
    hp = _mix1(hp, o_d_p, lam_vecs, row2(diff_subln), wo_c)
    y_p = _ffn(hp, row2(ffn_norm[1]), wg[1], wu[1], wd[1], row2(final_norm), True)

    rows_s, qd_s, kd_s, vd_s = _proj1(hs, row2(attn_norm[1]), w1, tab_s[2], tab_s[3])
    qd = jnp.transpose(qd_s.reshape(_DIFF_ROWS, b, LANES), (1, 0, 2))
    o_d_s = _decode_diff(qd, rows_s[:, None, :], cache_diff, page_table, 16)
    o_d_s = jnp.transpose(o_d_s, (1, 0, 2)).reshape(2 * DIFF_KVH, DIFF_GROUP, b, LANES)
    hs = _mix1(hs, o_d_s, lam_vecs, row2(diff_subln), wo_c)
    y_s = _ffn(hs, row2(ffn_norm[1]), wg[1], wu[1], wd[1], row2(final_norm), True)

    wb_p = min(NSA_WINDOW, t)
    pair = lambda a: a.reshape(a.shape[0], 2, 1, NSA_D)
    new_win_s = jnp.concatenate([state_nsa_win, win_s.reshape(b, 1, 2, 1, NSA_D)], axis=1)[:, ts:]
    return (y_p[None], y_s[:, None], mla_p[None], pair(cmp_p)[None], pair(slc_p)[None], pair(win_p[t - wb_p:])[None],
            rows_p.reshape(1, t, 2, DIFF_KVH, 2 * DIFF_D),
            mla_s[:, None], pair(cmp_s)[:, None], pair(slc_s)[:, None], new_win_s,
            rows_s.reshape(b, 1, 2, DIFF_KVH, 2 * DIFF_D))
```

```python
import functools
import math

import numpy as np
import jax
import jax.numpy as jnp
from jax import lax
from jax.experimental import pallas as pl
from jax.experimental.pallas import tpu as pltpu

F32 = jnp.float32
BF16 = jnp.bfloat16

LANES = 128
VMEM_LIMIT = 56 * 1024 * 1024

ROPE_THETA = 10000.0
NORM_EPS = 1e-6
PAGE = 128
MLA_HEADS = 8
MLA_Q_LORA = 256
MLA_KV_LORA = 128
MLA_NOPE = 64
MLA_ROPE = 32
MLA_V = 64
MLA_SCALE = (MLA_NOPE + MLA_ROPE) ** -0.5
NSA_HEADS = 8
NSA_D = 64
NSA_BLOCK = 64
NSA_TOP_N = 16
NSA_WINDOW = 512
NSA_SCALE = NSA_D ** -0.5
NSA_FORCED = 1.0e4
DIFF_HEADS = 8
DIFF_KVH = 2
DIFF_D = 64
DIFF_GROUP = DIFF_HEADS // DIFF_KVH
DIFF_SCALE = DIFF_D ** -0.5
DIFF_LAMBDA_INIT = 0.8 - 0.6 * math.exp(-0.3 * 1)

LOG2E = math.log2(math.e)
SUM_ROW_PAD = 16
NEG_BIG = -1.0e30
SEL_NEG = -float(2 ** 40)


def _cparams(sem):
    return pltpu.CompilerParams(dimension_semantics=sem, vmem_limit_bytes=VMEM_LIMIT)


def _rms(x, g):
    return x * lax.rsqrt(jnp.mean(x * x, axis=-1, keepdims=True) + NORM_EPS) * g


def _dot(a, b):
    return jnp.dot(a, b, preferred_element_type=F32)


def _dot_nt(a, b):
    return lax.dot_general(a, b, (((1,), (1,)), ((), ())), preferred_element_type=F32)


def _lane(shape):
    return lax.broadcasted_iota(jnp.int32, shape, len(shape) - 1)


_S_CQ, _S_CKV, _S_KR, _S_KR_ROT, _S_QN, _S_QN_ROT = 0, 2, 3, 4, 5, 9
_S_CMP, _S_SLC, _S_SLC_ROT, _S_WIN, _S_WIN_ROT, _S_GATE, _N_SLOTS0 = 13, 14, 15, 16, 17, 18, 19


def _proj0_kernel(x_ref, g_ref, w_ref, qn_ref, kvn_ref, uq_ref, ukv_ref, c32_ref, s32_ref, cn_ref, sn_ref,
                  *out_refs, prompt, tm):
    if prompt:
        (mla_ref, cmp_ref, slc_ref, win_ref, qm_ref, qc_ref, qr_ref, gate_ref,
         km_ref, vm_ref, ksa_ref, vs_ref, kw_ref, vw_ref) = out_refs
    else:
        (mla_ref, cmp_ref, slc_ref, win_ref, qm_ref, qc_ref, qr_ref, gate_ref, qa_ref) = out_refs

    hn = _rms(x_ref[...], g_ref[...]).astype(BF16)
    z = _dot(hn, w_ref[...])

    def slot(s, n=1):
        return z[:, s * LANES:(s + n) * LANES]

    c32, s32, cn, sn = c32_ref[...], s32_ref[...], cn_ref[...], sn_ref[...]
    lane = _lane((tm, LANES))
    low = lane < NSA_D

    cqn = _rms(slot(_S_CQ, 2), qn_ref[...]).astype(BF16)
    zq = _dot(cqn, uq_ref[...])
    ckvn = _rms(slot(_S_CKV), kvn_ref[...])
    kpe_hi = slot(_S_KR) * c32 + slot(_S_KR_ROT) * s32
    mla_ref[:, 0:MLA_KV_LORA] = ckvn
    mla_ref[:, MLA_KV_LORA:MLA_KV_LORA + MLA_ROPE] = pltpu.roll(kpe_hi, 64, axis=1)[:, 0:MLA_ROPE]
    ckvb = ckvn.astype(BF16)
    q_cos = jnp.where(low, 1.0, c32)
    nh = MLA_HEADS
    flash_scale = LOG2E if prompt else 1.0
    sum_row = jnp.where(lane == NSA_D, 1.0, 0.0)
    for h in range(nh):
        qh = (zq[:, h * LANES:(h + 1) * LANES] * q_cos
              + zq[:, (nh + h) * LANES:(nh + h + 1) * LANES] * s32) * (MLA_SCALE * flash_scale)
        qm_ref[h] = qh.astype(BF16)
        if not prompt:
            qlat = _dot(qh.astype(BF16), ukv_ref[h])
            qa_ref[h, :, 0:LANES] = qlat.astype(BF16)
            qa_ref[h, :, LANES:2 * LANES] = jnp.where(lane < MLA_ROPE, pltpu.roll(qh, 64, axis=1), 0.0).astype(BF16)
    if prompt:
        zk = _dot(ckvb, ukv_ref[0])
        zv = _dot(ckvb, ukv_ref[1])
        for h in range(nh):
            km_ref[h] = (zk[:, h * LANES:(h + 1) * LANES] + kpe_hi).astype(BF16)
            vm_ref[h] = (zv[:, h * LANES:(h + 1) * LANES] + sum_row).T.astype(BF16)

    for p in range(NSA_HEADS // 2):
        plain = slot(_S_QN + p) * NSA_SCALE
        roped = (slot(_S_QN + p) * cn + slot(_S_QN_ROT + p) * sn) * (NSA_SCALE * flash_scale)
        qc_ref[2 * p] = jnp.where(low, plain, 0.0).astype(BF16)
        qc_ref[2 * p + 1] = jnp.where(low, 0.0, plain).astype(BF16)
        qr_ref[2 * p] = jnp.where(low, roped, 0.0).astype(BF16)
        qr_ref[2 * p + 1] = jnp.where(low, 0.0, roped).astype(BF16)

    ck = jnp.where(low, cn, 1.0)
    cmp_ref[...] = slot(_S_CMP)
    slc = slot(_S_SLC) * ck + slot(_S_SLC_ROT) * sn
    win = slot(_S_WIN) * ck + slot(_S_WIN_ROT) * sn
    slc_ref[...] = slc
    win_ref[...] = win
    gate_ref[...] = jax.nn.sigmoid(slot(_S_GATE))
    if prompt:
        slc_sw = pltpu.roll(slc, 64, axis=1)
        win_sw = pltpu.roll(win, 64, axis=1)
        row = pl.program_id(0) * tm + lax.broadcasted_iota(jnp.int32, (tm, LANES), 0)
        onehot = ((row >> 6) & (LANES - 1)) == lane
        ksa_ref[:, 0:LANES] = jnp.where(onehot, 1.0, 0.0).astype(BF16)
        ksa_ref[:, LANES:2 * LANES] = jnp.where(low, slc, slc_sw).astype(BF16)
        is_sum = lane == NSA_D
        vs_ref[...] = jnp.where(is_sum, 1.0, slc_sw).T.astype(BF16)
        kw_ref[...] = jnp.where(low, win, win_sw).astype(BF16)
        vw_ref[...] = jnp.where(is_sum, 1.0, win_sw).T.astype(BF16)


def _rot_cols(w, d):
    k = w.shape[0]
    w3 = w.reshape(k, -1, d)
    return jnp.concatenate([-w3[..., d // 2:], w3[..., :d // 2]], axis=-1).reshape(k, -1)


def _place(w, left, width=LANES):
    return jnp.pad(w, ((0, 0), (left, width - left - w.shape[1])))


def _prep_layer0(w_in_a, w_mla_uq, w_mla_uk, w_mla_uv):
    o = np.cumsum([0, 256, 128, 32, 512, 64, 64, 64, 64, 64, 64, 24]).tolist()
    c_q, c_kv, k_r, q_n = (w_in_a[:, o[i]:o[i + 1]] for i in range(4))
    kv_c = w_in_a[:, o[4]:o[6]]
    kv_s = w_in_a[:, o[6]:o[8]]
    kv_w = w_in_a[:, o[8]:o[10]]
    g = w_in_a[:, o[10]:o[11]]
    z64 = jnp.zeros((w_in_a.shape[0], 64), F32)
    cols = [c_q, c_kv, _place(k_r, 64), _place(_rot_cols(k_r, MLA_ROPE), 64), q_n, _rot_cols(q_n, NSA_D),
            kv_c,
            kv_s, jnp.concatenate([_rot_cols(kv_s[:, :64], 64), z64], axis=1),
            kv_w, jnp.concatenate([_rot_cols(kv_w[:, :64], 64), z64], axis=1),
            _place(g, 0)]
    w0 = jnp.concatenate(cols, axis=1).astype(BF16)
    uq3 = w_mla_uq.reshape(MLA_Q_LORA, MLA_HEADS, MLA_NOPE + MLA_ROPE)
    nope, pe = uq3[..., :MLA_NOPE], uq3[..., MLA_NOPE:]
    pe_rot = jnp.concatenate([-pe[..., MLA_ROPE // 2:], pe[..., :MLA_ROPE // 2]], axis=-1)
    pad32 = jnp.zeros(pe.shape, F32)
    plain = jnp.concatenate([nope, pe, pad32], axis=-1).reshape(MLA_Q_LORA, -1)
    rot = jnp.concatenate([jnp.zeros(nope.shape, F32), pe_rot, pad32], axis=-1).reshape(MLA_Q_LORA, -1)
    uq = jnp.concatenate([plain, rot], axis=1).astype(BF16)
    pad_k = jnp.zeros((MLA_KV_LORA, MLA_HEADS, LANES - MLA_NOPE), F32)
    uk_slots = jnp.concatenate([w_mla_uk, pad_k], axis=-1).reshape(MLA_KV_LORA, -1)
    uv_slots = jnp.concatenate([w_mla_uv, pad_k], axis=-1).reshape(MLA_KV_LORA, -1)
    ukv_prompt = jnp.stack([uk_slots, uv_slots]).astype(BF16)
    ukt = jnp.transpose(w_mla_uk, (1, 2, 0))
    ukt = jnp.concatenate([ukt, jnp.zeros_like(ukt)], axis=1).astype(BF16)
    uv_pad = jnp.transpose(jnp.concatenate([w_mla_uv, pad_k], axis=-1), (1, 0, 2)).astype(BF16)
    return w0, uq, ukv_prompt, ukt, uv_pad


def _rope_tables(pos):
    def cs(dim):
        inv = ROPE_THETA ** (-jnp.arange(0, dim, 2, dtype=F32) / dim)
        ang = pos.astype(F32)[:, None] * inv[None, :]
        ang = jnp.concatenate([ang, ang], axis=-1)
        return jnp.cos(ang), jnp.sin(ang)
    c32, s32 = cs(MLA_ROPE)
    cn, sn = cs(NSA_D)
    return (_place(c32, 64), _place(s32, 64), jnp.concatenate([cn, cn], axis=1), jnp.concatenate([sn, sn], axis=1))


def _proj0(x, gain, w0, qnorm, kvnorm, uq, ukv, tables, prompt):
    t, d = x.shape
    tm = min(t, 256)
    nh = MLA_HEADS
    row = lambda w: pl.BlockSpec((tm, w), lambda i: (i, 0))
    head = lambda w: pl.BlockSpec((nh, tm, w), lambda i: (0, i, 0))
    full = lambda a: pl.BlockSpec(a.shape, lambda i: (0,) * a.ndim)
    out_shape = [jax.ShapeDtypeStruct((t, MLA_KV_LORA + MLA_ROPE), F32)] + [jax.ShapeDtypeStruct((t, LANES), F32)] * 3
    out_specs = [row(MLA_KV_LORA + MLA_ROPE), row(LANES), row(LANES), row(LANES)]
    out_shape += [jax.ShapeDtypeStruct((nh, t, LANES), BF16)] * 3 + [jax.ShapeDtypeStruct((t, LANES), F32)]
    out_specs += [head(LANES)] * 3 + [row(LANES)]
    if prompt:
        col = pl.BlockSpec((LANES, tm), lambda i: (0, i))
        out_shape += [jax.ShapeDtypeStruct((nh, t, LANES), BF16), jax.ShapeDtypeStruct((nh, LANES, t), BF16)]
        out_specs += [head(LANES), pl.BlockSpec((nh, LANES, tm), lambda i: (0, 0, i))]
        out_shape += [jax.ShapeDtypeStruct((t, 2 * LANES), BF16), jax.ShapeDtypeStruct((LANES, t), BF16),
                      jax.ShapeDtypeStruct((t, LANES), BF16), jax.ShapeDtypeStruct((LANES, t), BF16)]
        out_specs += [row(2 * LANES), col, row(LANES), col]
    else:
        out_shape += [jax.ShapeDtypeStruct((nh, t, 2 * LANES), BF16)]
        out_specs += [head(2 * LANES)]
    return pl.pallas_call(
        functools.partial(_proj0_kernel, prompt=prompt, tm=tm),
        grid=(t // tm,),
        in_specs=[row(d), full(gain), full(w0), full(qnorm), full(kvnorm), full(uq), full(ukv)] + [row(LANES)] * 4,
        out_specs=out_specs, out_shape=out_shape,
        compiler_params=_cparams(("parallel",)),
        name="proj0_prompt" if prompt else "proj0_sample",
    )(x, gain, w0, qnorm, kvnorm, uq, ukv, *tables)


_CMP_PAIR = 2
_STAGE_PITCH = PAGE + 8


def _compress_rows(src_ref, n_blocks, pe_ref, w1_ref):
    acc = None
    for r0 in range(0, NSA_BLOCK, _CMP_PAIR):
        xs = [src_ref[pl.ds(r0 + u, n_blocks, stride=NSA_BLOCK), :] + pe_ref[r0 + u:r0 + u + 1, :]
              for u in range(_CMP_PAIR)]
        part = _dot(jnp.concatenate(xs, axis=1).astype(BF16), w1_ref[r0 // _CMP_PAIR])
        acc = part if acc is None else acc + part
    return acc


def _compress_prompt_kernel(f_ref, pe_ref, w1_ref, w2_ref, o_ref, *, n_blocks):
    pre = _compress_rows(f_ref, n_blocks, pe_ref, w1_ref)
    o_ref[...] = _dot(jax.nn.silu(pre).astype(BF16), w2_ref[...])


def _compress_paged_kernel(pt_ref, *refs, pages):
    page_refs = refs[:pages]
    pe_ref, w1_ref, w2_ref, o_ref, stage, pre_k, pre_v = refs[pages:]
    ch = pl.program_id(1)
    for p in range(pages):
        stage[_STAGE_PITCH * p:_STAGE_PITCH * p + PAGE, :] = page_refs[p][0]
    for kv, pre in enumerate((pre_k, pre_v)):
        acc = None
        for f0 in range(0, NSA_D, _CMP_PAIR):
            xs = [stage[pl.ds(kv * NSA_D + f0 + u, pages, stride=_STAGE_PITCH), :]
                  + pe_ref[kv * NSA_D + f0 + u:kv * NSA_D + f0 + u + 1, :] for u in range(_CMP_PAIR)]
            part = _dot(jnp.concatenate(xs, axis=1).astype(BF16), w1_ref[kv, f0 // _CMP_PAIR])
            acc = part if acc is None else acc + part
        pre[pl.ds(pl.multiple_of(ch * pages, pages), pages), :] = acc

    @pl.when(ch == pl.num_programs(1) - 1)
    def _():
        pk, pv = pre_k[...], pre_v[...]
        low = _lane(pk.shape) < NSA_D
        even = jnp.where(low, pk, pltpu.roll(pv, 64, axis=1))
        odd = jnp.where(low, pltpu.roll(pk, 64, axis=1), pv)
        both = jnp.concatenate([even, odd], axis=0)
        o_ref[0] = _dot(jax.nn.silu(both).astype(BF16), w2_ref[...])


def _prep_compress(pe_k, pe_v, w1_k, w1_v, w2_k, w2_v):
    pe = jnp.concatenate([pe_k, pe_v], axis=1)
    w1k3 = w1_k.reshape(NSA_BLOCK, NSA_D, NSA_D)
    w1v3 = w1_v.reshape(NSA_BLOCK, NSA_D, NSA_D)
    w1 = jnp.zeros((NSA_BLOCK, 2, NSA_D, 2 * NSA_D), F32)
    w1 = w1.at[:, 0, :, :NSA_D].set(w1k3).at[:, 1, :, NSA_D:].set(w1v3)
    w1 = w1.reshape(NSA_BLOCK // _CMP_PAIR, _CMP_PAIR * LANES, 2 * NSA_D).astype(BF16)
    w2 = jnp.zeros((2 * NSA_D, 2 * NSA_D), F32)
    w2 = w2.at[:NSA_D, :NSA_D].set(w2_k).at[NSA_D:, NSA_D:].set(w2_v).astype(BF16)
    pe_t = jnp.tile(jnp.concatenate([pe_k.T, pe_v.T], axis=0), (1, 2))
    def per_dim(w3):
        wd = jnp.transpose(w3, (1, 0, 2))
        z = jnp.zeros_like(wd)
        top = jnp.concatenate([wd, z], axis=2)
        bot = jnp.concatenate([z, wd], axis=2)
        return jnp.concatenate([top, bot], axis=1)
    w1_t = jnp.stack([per_dim(w1k3), per_dim(w1v3)])
    w1_t = w1_t.reshape(2, NSA_D // _CMP_PAIR, _CMP_PAIR * LANES, LANES).astype(BF16)
    return pe, w1, w2, pe_t, w1_t


def _compress_prompt(cmp_rows, pe, w1, w2):
    t = cmp_rows.shape[0]
    nb = t // NSA_BLOCK
    full = lambda a: pl.BlockSpec(a.shape, lambda i: (0,) * a.ndim)
    return pl.pallas_call(
        functools.partial(_compress_prompt_kernel, n_blocks=nb),
        grid=(1,), in_specs=[full(cmp_rows), full(pe), full(w1), full(w2)],
        out_specs=pl.BlockSpec((nb, LANES), lambda i: (0, 0)),
        out_shape=jax.ShapeDtypeStruct((nb, LANES), F32),
        compiler_params=_cparams(("arbitrary",)), name="compress_prompt",
    )(cmp_rows, pe, w1, w2)


def _feature_major(cache):
    n, rows = cache.shape[:2]
    return jnp.transpose(cache, (0, 2, 3, 4, 1)).reshape(n, 2 * NSA_D, rows)


def _compress_paged(cache_cmp, page_table, pe_t, w1_t, w2, pages):
    b, n_pages = page_table.shape
    nb = n_pages * PAGE // NSA_BLOCK
    view = _feature_major(cache_cmp)
    pages = min(pages, n_pages)
    page_specs = [pl.BlockSpec((1, 2 * NSA_D, PAGE),
                               functools.partial(lambda i, c, pt, p: (pt[i, c * pages + p], 0, 0), p=p))
                  for p in range(pages)]
    full = lambda a: pl.BlockSpec(a.shape, lambda i, c, pt: (0,) * a.ndim)
    return pl.pallas_call(
        functools.partial(_compress_paged_kernel, pages=pages),
        grid_spec=pltpu.PrefetchScalarGridSpec(
            num_scalar_prefetch=1, grid=(b, n_pages // pages),
            in_specs=page_specs + [full(pe_t), full(w1_t), full(w2)],
            out_specs=pl.BlockSpec((1, nb, LANES), lambda i, c, pt: (i, 0, 0)),
            scratch_shapes=[pltpu.VMEM((_STAGE_PITCH * pages, LANES), F32), pltpu.VMEM((n_pages, LANES), F32),
                            pltpu.VMEM((n_pages, LANES), F32)]),
        out_shape=jax.ShapeDtypeStruct((b, nb, LANES), F32),
        compiler_params=_cparams(("parallel", "arbitrary")), name="compress_paged",
    )(page_table, *([view] * pages), pe_t, w1_t, w2)


def _block_id(lane, perm_pages):
    if not perm_pages:
        return lane
    assert perm_pages & (perm_pages - 1) == 0
    shift = perm_pages.bit_length() - 1
    return jnp.where(lane < 2 * perm_pages, 2 * (lane & (perm_pages - 1)) + (lane >> shift), lane)


def _cmp_attn_kernel(kv_ref, q_ref, o_ref, imp_ref, *, rows_per_query, tq, pos0, pos_step, nbp, perm_pages):
    kv = kv_ref[...] if len(kv_ref.shape) == 2 else kv_ref[0]
    nb = kv.shape[0]
    sw = pltpu.roll(kv, 64, axis=1)
    kdup = jnp.where(_lane(kv.shape) < NSA_D, kv, sw).astype(BF16)
    vk = sw.astype(BF16)
    if rows_per_query:
        q = q_ref[0]
        qpos = jnp.full((q.shape[0], 1), pos0, jnp.int32)
    else:
        q = q_ref[...].reshape(NSA_HEADS * tq, LANES)
        r = lax.broadcasted_iota(jnp.int32, (NSA_HEADS * tq, 1), 0)
        qpos = pos0 + pos_step * (pl.program_id(0) * tq + (r & (tq - 1)))
    s = _dot_nt(q, kdup)
    last = (_block_id(_lane(s.shape), perm_pages) + 1) * NSA_BLOCK - 1
    s = jnp.where(last <= qpos, s, -jnp.inf)
    m = jnp.max(s, axis=-1, keepdims=True)
    m = jnp.where(m == -jnp.inf, 0.0, m)
    e = jnp.exp(s - m)
    den = jnp.sum(e, axis=-1, keepdims=True)
    p = e / jnp.where(den > 0.0, den, 1.0)
    o = _dot(p.astype(BF16), vk)
    if rows_per_query:
        o_ref[0] = o
        imp = jnp.sum(p, axis=0, keepdims=True)
        if nbp > nb:
            imp = jnp.concatenate([imp, jnp.zeros((1, nbp - nb), F32)], axis=1)
        imp_ref[0] = imp
    else:
        o_ref[...] = o.reshape(NSA_HEADS, tq, LANES)
        imp = jnp.sum(p.reshape(NSA_HEADS, tq, nb), axis=0)
        if nbp > nb:
            imp = jnp.concatenate([imp, jnp.zeros((tq, nbp - nb), F32)], axis=1)
        imp_ref[...] = imp


def _cmp_attn_prompt(kcvc, q_cmp, nbp):
    nh, t, _ = q_cmp.shape
    nb = kcvc.shape[0]
    tq = min(t, 128)
    return pl.pallas_call(
        functools.partial(_cmp_attn_kernel, rows_per_query=False, tq=tq, pos0=0, pos_step=1, nbp=nbp, perm_pages=0),
        grid=(t // tq,),
        in_specs=[pl.BlockSpec((nb, LANES), lambda i: (0, 0)), pl.BlockSpec((nh, tq, LANES), lambda i: (0, i, 0))],
        out_specs=[pl.BlockSpec((nh, tq, LANES), lambda i: (0, i, 0)), pl.BlockSpec((tq, nbp), lambda i: (i, 0))],
        out_shape=[jax.ShapeDtypeStruct((nh, t, LANES), F32), jax.ShapeDtypeStruct((t, nbp), F32)],
        compiler_params=_cparams(("parallel",)), name="cmp_attn_prompt",
    )(kcvc, q_cmp)


def _cmp_attn_sample(kcvc, q_cmp, pos, nbp, perm_pages):
    b, nh, _ = q_cmp.shape
    nb = kcvc.shape[1]
    return pl.pallas_call(
        functools.partial(_cmp_attn_kernel, rows_per_query=True, tq=1, pos0=pos, pos_step=0, nbp=nbp,
                          perm_pages=perm_pages),
        grid=(b,),
        in_specs=[pl.BlockSpec((1, nb, LANES), lambda i: (i, 0, 0)), pl.BlockSpec((1, nh, LANES), lambda i: (i, 0, 0))],
        out_specs=[pl.BlockSpec((1, nh, LANES), lambda i: (i, 0, 0)), pl.BlockSpec((1, 1, nbp), lambda i: (i, 0, 0))],
        out_shape=[jax.ShapeDtypeStruct((b, nh, LANES), F32), jax.ShapeDtypeStruct((b, 1, nbp), F32)],
        compiler_params=_cparams(("parallel",)), name="cmp_attn_sample",
    )(kcvc, q_cmp)


def _select_kernel(imp_ref, *refs, tq, pos0, pos_step, nb_sel, n_halves, with_query, perm_pages):
    if with_query:
        q_ref, qa_ref = refs
    else:
        idx_ref, val_ref = refs
    imp = imp_ref[...]
    nbp = imp.shape[1]
    lane = _block_id(_lane(imp.shape), perm_pages)
    qpos = pos0 + pos_step * (pl.program_id(0) * tq + lax.broadcasted_iota(jnp.int32, (tq, 1), 0))
    cur = qpos >> 6
    forced = (lane == 0) | (lane == cur) | (lane == cur - 1)
    score = jnp.where(forced, NSA_FORCED, imp)
    work = jnp.where((lane <= cur) & (lane < nb_sel), score, -jnp.inf)
    lane_f = lane.astype(F32)
    neg = jnp.full(imp.shape, SEL_NEG, F32)
    lane16 = _lane((tq, LANES))
    idx_acc = jnp.zeros((tq, LANES), jnp.int32)
    val_acc = jnp.zeros((tq, LANES), jnp.int32)
    for n in range(min(NSA_TOP_N, nb_sel)):
        mx = jnp.max(work, axis=-1, keepdims=True)
        first = jnp.min(jnp.where(work == mx, lane_f, float(nbp)), axis=-1, keepdims=True)
        pick = lane_f == first
        valid_neg = jnp.where(mx > -jnp.inf, 0.0, SEL_NEG)
        neg = jnp.where(pick, jnp.maximum(neg, valid_neg), neg)
        work = jnp.where(pick, -jnp.inf, work)
        if not with_query:
            idx_acc = jnp.where(lane16 == n, first.astype(jnp.int32), idx_acc)
            val_acc = jnp.where(lane16 == n, jnp.where(mx > -jnp.inf, 1, 0), val_acc)
    if with_query:
        neg = neg.astype(BF16)
        for hf in range(n_halves):
            part = neg[:, hf * LANES:(hf + 1) * LANES]
            for h in range(NSA_HEADS):
                qa_ref[hf, h, :, 0:LANES] = part
                qa_ref[hf, h, :, LANES:2 * LANES] = q_ref[h]
    else:
        idx_ref[...] = idx_acc
        val_ref[...] = val_acc


def _select_prompt(imp, q_rot):
    t, nbp = imp.shape
    nh = q_rot.shape[0]
    tq = min(t, 128)
    n_halves = nbp // LANES
    return pl.pallas_call(
        functools.partial(_select_kernel, tq=tq, pos0=0, pos_step=1, nb_sel=t // NSA_BLOCK, n_halves=n_halves,
                          with_query=True, perm_pages=0),
        grid=(t // tq,),
        in_specs=[pl.BlockSpec((tq, nbp), lambda i: (i, 0)), pl.BlockSpec((nh, tq, LANES), lambda i: (0, i, 0))],
        out_specs=pl.BlockSpec((n_halves, nh, tq, 2 * LANES), lambda i: (0, 0, i, 0)),
        out_shape=jax.ShapeDtypeStruct((n_halves, nh, t, 2 * LANES), BF16),
        compiler_params=_cparams(("parallel",)), name="select_prompt",
    )(imp, q_rot)


def _select_sample(imp, pos, nb_sel, perm_pages):
    b, nbp = imp.shape
    return pl.pallas_call(
        functools.partial(_select_kernel, tq=b, pos0=pos, pos_step=0, nb_sel=nb_sel, n_halves=0, with_query=False,
                          perm_pages=perm_pages),
        grid=(1,),
        in_specs=[pl.BlockSpec((b, nbp), lambda i: (0, 0))],
        out_specs=[pl.BlockSpec((b, LANES), lambda i: (0, 0))] * 2,
        out_shape=[jax.ShapeDtypeStruct((b, LANES), jnp.int32)] * 2,
        compiler_params=_cparams(("arbitrary",)), name="select_sample",
    )(imp)


_FLASH_CQ = 256


def _flash_kernel(qi_ref, kj_ref, flag_ref, q_ref, k_ref, vt_ref, o_ref, m_scr, acc_scr, *, r, tq, tk, window,
                  sum_row):
    step_id = pl.program_id(1)
    i, kv_blk, flags = qi_ref[step_id], kj_ref[step_id], flag_ref[step_id]
    m_rows = r * tq
    cq = min(_FLASH_CQ, m_rows)

    @pl.when((flags & 1) != 0)
    def _():
        m_scr[...] = jnp.full(m_scr.shape, NEG_BIG, F32)
        acc_scr[...] = jnp.zeros(acc_scr.shape, F32)

    need_mask = None if window else kv_blk * tk + (tk - 1) > i * tq

    def q_chunk(c):
        if tq >= cq:
            per_head = tq // cq
            return q_ref[0, c // per_head, (c % per_head) * cq:(c % per_head + 1) * cq, :]
        heads = cq // tq
        return q_ref[0, c * heads:(c + 1) * heads].reshape(cq, q_ref.shape[-1])

    def scores(c, masked):
        s = _dot_nt(k_ref[0], q_chunk(c))
        if masked:
            qpos = i * tq + ((c * cq + _lane((tk, cq))) & (tq - 1))
            kpos = kv_blk * tk + lax.broadcasted_iota(jnp.int32, (tk, cq), 0)
            ok = kpos <= qpos
            if window:
                ok = ok & (qpos - kpos < window)
            s = jnp.where(ok, s, NEG_BIG)
        return s

    def step(masked):
        m_all = m_scr[...]
        n_chunks = m_rows // cq
        def accumulate(cols, alpha, p):
            acc_scr[:, cols] = alpha * acc_scr[:, cols] + _dot(vt_ref[0], p)

        s_next = scores(0, masked)
        pending = None
        for c in range(n_chunks):
            cols = slice(c * cq, (c + 1) * cq)
            s = s_next
            if c + 1 < n_chunks:
                s_next = scores(c + 1, masked)
            m_prev = m_all[:, cols]
            m_new = jnp.maximum(m_prev, jnp.max(s, axis=0, keepdims=True))
            alpha = jnp.exp2(m_prev - m_new)
            p = jnp.exp2(s - m_new).astype(BF16)
            m_scr[:, cols] = m_new
            if pending is not None:
                accumulate(*pending)
            pending = (cols, alpha, p)
        accumulate(*pending)

    if window:
        step(True)
    else:
        pl.when(need_mask)(lambda: step(True))
        pl.when(jnp.logical_not(need_mask))(lambda: step(False))

    @pl.when((flags & 2) != 0)
    def _():
        o = (acc_scr[0:LANES, :] / acc_scr[sum_row:sum_row + 1, :]).T
        o_ref[0] = o.reshape(r, tq, LANES).astype(o_ref.dtype)


def _flash(q, k, vt, *, tq, tk, groups, q_index, kv_index, sum_row, window=0, out_dtype=F32, name):
    r, t, dq = q.shape[1:]
    dv = vt.shape[1]
    tq, tk = min(tq, t), min(tk, t)
    qi, kj, flags = [], [], []
    for i in range(t // tq):
        if window:
            assert tq == tk and window % tk == 0
            blocks = [b for b in range(i - window // tk, i + 1) if b >= 0]
        else:
            blocks = list(range((i * tq + tq - 1) // tk + 1))
        for n, b in enumerate(blocks):
            qi.append(i)
            kj.append(b)
            flags.append((1 if n == 0 else 0) | (2 if n == len(blocks) - 1 else 0))
    qi, kj, flags = (jnp.asarray(np.array(a, np.int32)) for a in (qi, kj, flags))
    return pl.pallas_call(
        functools.partial(_flash_kernel, r=r, tq=tq, tk=tk, window=window, sum_row=sum_row),
        grid_spec=pltpu.PrefetchScalarGridSpec(
            num_scalar_prefetch=3, grid=(groups, int(qi.shape[0])),
            in_specs=[pl.BlockSpec((1, r, tq, dq), lambda g, s, qi, kj, fl: (q_index(g, kj[s] * tk), 0, qi[s], 0)),
                      pl.BlockSpec((1, tk, dq), lambda g, s, qi, kj, fl: (kv_index(g), kj[s], 0)),
                      pl.BlockSpec((1, dv, tk), lambda g, s, qi, kj, fl: (kv_index(g), 0, kj[s]))],
            out_specs=pl.BlockSpec((1, r, tq, LANES), lambda g, s, qi, kj, fl: (g, 0, qi[s], 0)),
            scratch_shapes=[pltpu.VMEM((1, r * tq), F32), pltpu.VMEM((dv, r * tq), F32)]),
        out_shape=jax.ShapeDtypeStruct((groups, r, t, LANES), out_dtype),
        compiler_params=_cparams(("parallel", "arbitrary")), name=name,
    )(qi, kj, flags, q, k, vt)


def _online_softmax_step(s, m_scr, l_scr):
    m_prev = m_scr[...]
    m_new = jnp.maximum(m_prev, jnp.max(s, axis=-1, keepdims=True))
    alpha = jnp.exp(m_prev - m_new)
    p = jnp.exp(s - m_new)
    l_scr[...] = alpha * l_scr[...] + jnp.sum(p, axis=-1, keepdims=True)
    m_scr[...] = m_new
    return alpha, p


def _decode_mla_kernel(pt_ref, q_ref, new_ref, *refs, pages):
    page_refs = refs[:pages]
    o_ref, m_scr, l_scr, acc_scr = refs[pages:]
    c = pl.program_id(1)
    q = q_ref[0]

    @pl.when(c == 0)
    def _():
        new = new_ref[0].astype(BF16).astype(F32)
        m_scr[...] = jnp.sum(q.astype(F32) * new, axis=-1, keepdims=True)
        l_scr[...] = jnp.ones(l_scr.shape, F32)
        acc_scr[...] = jnp.broadcast_to(new[:, 0:MLA_KV_LORA], acc_scr.shape)

    kt = jnp.concatenate([pr[0] for pr in page_refs], axis=1).astype(BF16)
    alpha, p = _online_softmax_step(_dot(q, kt), m_scr, l_scr)
    acc_scr[...] = alpha * acc_scr[...] + _dot_nt(p.astype(BF16), kt[0:MLA_KV_LORA])

    @pl.when(c == pl.num_programs(1) - 1)
    def _():
        o_ref[0] = acc_scr[...] / l_scr[...]


def _decode_mla(q, new_rows, cache_mla, page_table, pages):
    b, rows, kw = q.shape
    n_pages = page_table.shape[1]
    pages = min(pages, n_pages)
    view = jnp.transpose(cache_mla, (0, 2, 1))
    page_specs = [pl.BlockSpec((1, kw, PAGE), functools.partial(lambda i, c, pt, p: (pt[i, c * pages + p], 0, 0), p=p))
                  for p in range(pages)]
    return pl.pallas_call(
        functools.partial(_decode_mla_kernel, pages=pages),
        grid_spec=pltpu.PrefetchScalarGridSpec(
            num_scalar_prefetch=1, grid=(b, n_pages // pages),
            in_specs=[pl.BlockSpec((1, rows, kw), lambda i, c, pt: (i, 0, 0)),
                      pl.BlockSpec((1, 1, kw), lambda i, c, pt: (i, 0, 0))] + page_specs,
            out_specs=pl.BlockSpec((1, rows, MLA_KV_LORA), lambda i, c, pt: (i, 0, 0)),
            scratch_shapes=[pltpu.VMEM((rows, 1), F32), pltpu.VMEM((rows, 1), F32),
                            pltpu.VMEM((rows, MLA_KV_LORA), F32)]),
        out_shape=jax.ShapeDtypeStruct((b, rows, MLA_KV_LORA), F32),
        compiler_params=_cparams(("parallel", "arbitrary")), name="decode_mla",
    )(page_table, q, new_rows, *([view] * pages))


_DIFF_ROWS = 2 * DIFF_KVH * DIFF_GROUP
_DIFF_COLS = 2 * DIFF_KVH


def _decode_diff_kernel(pt_ref, q_ref, new_ref, *refs, pages):
    page_refs = refs[:pages]
    o_ref, m_scr, l_scr, acc_scr = refs[pages:]
    c = pl.program_id(1)
    q = q_ref[0]
    head = lax.broadcasted_iota(jnp.int32, (_DIFF_ROWS, 1), 0) >> 3

    @pl.when(c == 0)
    def _():
        new = new_ref[0].astype(BF16).astype(F32)
        knew = jnp.where(head == 0, new[:, 0:LANES], new[:, LANES:2 * LANES])
        vnew = jnp.where(head == 0, new[:, 2 * LANES:3 * LANES], new[:, 3 * LANES:4 * LANES])
        m_scr[...] = jnp.sum(q.astype(F32) * knew, axis=-1, keepdims=True)
        l_scr[...] = jnp.ones(l_scr.shape, F32)
        acc_scr[...] = vnew

    def cache_rows(c):
        return jnp.concatenate([pr[0, pl.ds(c, PAGE, stride=_DIFF_COLS), :] for pr in page_refs],
                               axis=0).astype(BF16)

    rows_q = _DIFF_ROWS // DIFF_KVH
    qf = q.astype(F32)
    s = jnp.concatenate([_dot_nt(qf[g * rows_q:(g + 1) * rows_q].astype(BF16), cache_rows(g))
                         for g in range(DIFF_KVH)], axis=0)
    alpha, p = _online_softmax_step(s, m_scr, l_scr)
    pv = jnp.concatenate([_dot(p[g * rows_q:(g + 1) * rows_q].astype(BF16), cache_rows(DIFF_KVH + g))
                          for g in range(DIFF_KVH)], axis=0)
    acc_scr[...] = alpha * acc_scr[...] + pv

    @pl.when(c == pl.num_programs(1) - 1)
    def _():
        o_ref[0] = acc_scr[...] / l_scr[...]


def _decode_diff(q, new_rows, cache_diff, page_table, pages):
    b, rows, _ = q.shape
    n_pool = cache_diff.shape[0]
    n_pages = page_table.shape[1]
    pages = min(pages, n_pages)
    view = cache_diff.reshape(n_pool, PAGE * _DIFF_COLS, LANES)
    page_specs = [pl.BlockSpec((1, PAGE * _DIFF_COLS, LANES),
                               functools.partial(lambda i, c, pt, p: (pt[i, c * pages + p], 0, 0), p=p))
                  for p in range(pages)]
    return pl.pallas_call(
        functools.partial(_decode_diff_kernel, pages=pages),
        grid_spec=pltpu.PrefetchScalarGridSpec(
            num_scalar_prefetch=1, grid=(b, n_pages // pages),
            in_specs=[pl.BlockSpec((1, rows, LANES), lambda i, c, pt: (i, 0, 0)),
                      pl.BlockSpec((1, 1, _DIFF_COLS * LANES), lambda i, c, pt: (i, 0, 0))] + page_specs,
            out_specs=pl.BlockSpec((1, rows, LANES), lambda i, c, pt: (i, 0, 0)),
            scratch_shapes=[pltpu.VMEM((rows, 1), F32), pltpu.VMEM((rows, 1), F32), pltpu.VMEM((rows, LANES), F32)]),
        out_shape=jax.ShapeDtypeStruct((b, rows, LANES), F32),
        compiler_params=_cparams(("parallel", "arbitrary")), name="decode_diff",
    )(page_table, q, new_rows, *([view] * pages))


def _nsa_decode_kernel(idx_ref, val_ref, pt_ref, q_ref, news_ref, neww_ref, win_ref, *refs, n_sel, nb_cached, pos,
                       win_pos0):
    blk_refs = refs[:n_sel]
    os_ref, ow_ref = refs[n_sel:]
    b = pl.program_id(0)
    q = q_ref[0]
    qf = q.astype(F32)

    def new_key_score(new_ref):
        row = new_ref[0].astype(BF16).astype(F32)
        kd = jnp.where(_lane(row.shape) < NSA_D, row, pltpu.roll(row, 64, axis=1))
        vnew = jnp.where(_lane(row.shape) < NSA_D, pltpu.roll(row, 64, axis=1), 0.0)
        return jnp.sum(qf * kd, axis=-1, keepdims=True), vnew

    def attend(ktv, ok, s_new, new_ok, vnew):
        ktv = ktv.astype(BF16)
        s = _dot(q, jnp.concatenate([ktv[0:NSA_D], ktv[0:NSA_D]], axis=0))
        s = jnp.where(ok, s, -jnp.inf)
        if new_ok is not True:
            s_new = jnp.where(new_ok, s_new, -jnp.inf)
        m = jnp.maximum(jnp.max(s, axis=-1, keepdims=True), s_new)
        m = jnp.where(m == -jnp.inf, 0.0, m)
        e = jnp.exp(s - m)
        e_new = jnp.exp(s_new - m)
        den = jnp.sum(e, axis=-1, keepdims=True) + e_new
        den = jnp.where(den > 0.0, den, 1.0)
        o = _dot_nt(e.astype(BF16), ktv)
        return (pltpu.roll(o, 64, axis=1) + e_new * vnew) / den

    ktv = jnp.concatenate([r[0] for r in blk_refs], axis=1)
    lane = _lane((NSA_HEADS, n_sel * PAGE))
    page_of, half_of = lane >> 7, (lane >> 6) & 1
    ok = jnp.zeros(lane.shape, F32)
    new_ok = jnp.int32(0)
    for n in range(n_sel):
        valid = val_ref[b * n_sel + n] > 0
        blk = idx_ref[b * n_sel + n]
        use = jnp.where(valid & (blk < nb_cached), 1.0, 0.0)
        ok = jnp.where(page_of == n, jnp.where(half_of == (blk & 1), use, 0.0), ok)
        new_ok = new_ok | jnp.where(valid & (blk == nb_cached), 1, 0)
    s_new, vnew = new_key_score(news_ref)
    os_ref[0] = attend(ktv, ok > 0.0, s_new, new_ok > 0, vnew)

    wpos = win_pos0 + _lane((NSA_HEADS, win_ref.shape[-1]))
    d = pos - wpos
    okw = (d >= 0) & (d < NSA_WINDOW) & (wpos >= 0)
    sw_new, vwnew = new_key_score(neww_ref)
    ow_ref[0] = attend(win_ref[0], okw, sw_new, True, vwnew)


def _nsa_decode(idx, val, page_table, q_rot, new_slc, new_win, state_win, cache_slc, pos):
    b = q_rot.shape[0]
    n_pages = page_table.shape[1]
    nb_cached = n_pages * PAGE // NSA_BLOCK
    n_sel = min(NSA_TOP_N, nb_cached + 1)
    wb = state_win.shape[1]
    blocks = _feature_major(cache_slc)
    state_win = _feature_major(state_win)

    def blk_index(i, idx_r, val_r, pt_r, n):
        blk = jnp.minimum(idx_r[i * n_sel + n], nb_cached - 1)
        return (pt_r[i, blk >> 1], 0, 0)

    blk_specs = [pl.BlockSpec((1, 2 * NSA_D, PAGE), functools.partial(blk_index, n=n)) for n in range(n_sel)]
    per_seq = lambda shp: pl.BlockSpec((1,) + shp, lambda i, a, c, d: (i, 0, 0))
    return pl.pallas_call(
        functools.partial(_nsa_decode_kernel, n_sel=n_sel, nb_cached=nb_cached, pos=pos, win_pos0=pos - wb),
        grid_spec=pltpu.PrefetchScalarGridSpec(
            num_scalar_prefetch=3, grid=(b,),
            in_specs=[per_seq((NSA_HEADS, LANES)), per_seq((1, LANES)), per_seq((1, LANES)), per_seq((2 * NSA_D, wb))]
            + blk_specs,
            out_specs=[per_seq((NSA_HEADS, LANES))] * 2),
        out_shape=[jax.ShapeDtypeStruct((b, NSA_HEADS, LANES), F32)] * 2,
        compiler_params=_cparams(("parallel",)), name="nsa_decode",
    )(idx[:, :n_sel].reshape(-1), val[:, :n_sel].reshape(-1), page_table, q_rot, new_slc, new_win, state_win, *([blocks] * n_sel))


def _mix0_kernel(x_ref, om_ref, oc_ref, os_ref, ow_ref, g_ref, w_ref, *refs, absorbed):
    if absorbed:
        uv_ref, o_ref = refs
    else:
        (o_ref,) = refs
    gates = g_ref[...]
    acc = x_ref[...]
    for h in range(MLA_HEADS):
        om = om_ref[h]
        if absorbed:
            om = _dot(om.astype(BF16), uv_ref[h])
        acc = acc + _dot(om.astype(BF16), w_ref[h])
    for h in range(NSA_HEADS):
        gc, gs, gw = (gates[:, k * NSA_HEADS + h:k * NSA_HEADS + h + 1] for k in range(3))
        on = gc * oc_ref[h] + gs * os_ref[h] + gw * ow_ref[h]
        acc = acc + _dot(on.astype(BF16), w_ref[MLA_HEADS + h])
    o_ref[...] = acc


def _mix0(x, o_mla, o_c, o_s, o_w, gates, w_out, uv=None):
    t, d = x.shape
    tm = min(t, 512)
    row = lambda w: pl.BlockSpec((tm, w), lambda i: (i, 0))
    head = pl.BlockSpec((NSA_HEADS, tm, LANES), lambda i: (0, i, 0))
    full = lambda a: pl.BlockSpec(a.shape, lambda i: (0,) * a.ndim)
    ins = [x, o_mla, o_c, o_s, o_w, gates, w_out] + ([uv] if uv is not None else [])
    specs = [row(d), head, head, head, head, row(LANES), full(w_out)] + ([full(uv)] if uv is not None else [])
    return pl.pallas_call(
        functools.partial(_mix0_kernel, absorbed=uv is not None),
        grid=(t // tm,), in_specs=specs, out_specs=row(d), out_shape=jax.ShapeDtypeStruct((t, d), F32),
        compiler_params=_cparams(("parallel",)), name="mix0",
    )(*ins)


def _ffn_kernel(x_ref, g_ref, wg_ref, wu_ref, wd_ref, fg_ref, o_ref, hn_scr, acc_scr, *, final_norm):
    f = pl.program_id(1)

    @pl.when(f == 0)
    def _():
        hn_scr[...] = _rms(x_ref[...], g_ref[...]).astype(BF16)
        acc_scr[...] = x_ref[...]

    hn = hn_scr[...]
    act = jax.nn.silu(_dot(hn, wg_ref[...])) * _dot(hn, wu_ref[...])
    acc_scr[...] += _dot(act.astype(BF16), wd_ref[...])

    @pl.when(f == pl.num_programs(1) - 1)
    def _():
        y = acc_scr[...]
        o_ref[...] = _rms(y, fg_ref[...]) if final_norm else y


def _ffn(x, gain, wg, wu, wd, final_gain, final_norm):
    t, d = x.shape
    dff = wg.shape[1]
    tm = min(t, 512)
    tf = dff // 2 if (dff // 2) % LANES == 0 else dff
    return pl.pallas_call(
        functools.partial(_ffn_kernel, final_norm=final_norm),
        grid=(t // tm, dff // tf),
        in_specs=[pl.BlockSpec((tm, d), lambda i, f: (i, 0)), pl.BlockSpec((1, d), lambda i, f: (0, 0)),
                  pl.BlockSpec((d, tf), lambda i, f: (0, f)), pl.BlockSpec((d, tf), lambda i, f: (0, f)),
                  pl.BlockSpec((tf, d), lambda i, f: (f, 0)), pl.BlockSpec((1, d), lambda i, f: (0, 0))],
        out_specs=pl.BlockSpec((tm, d), lambda i, f: (i, 0)),
        out_shape=jax.ShapeDtypeStruct((t, d), F32),
        scratch_shapes=[pltpu.VMEM((tm, d), BF16), pltpu.VMEM((tm, d), F32)],
        compiler_params=_cparams(("parallel", "arbitrary")), name="ffn",
    )(x, gain, wg, wu, wd, final_gain)


_N_QSLOT = DIFF_HEADS


def _proj1_kernel(x_ref, g_ref, w_ref, cn_ref, sn_ref, rows_ref, q_ref, k_ref, v_ref, *, tm, prompt):
    hn = _rms(x_ref[...], g_ref[...]).astype(BF16)
    z = _dot(hn, w_ref[...])
    cn, sn = cn_ref[...], sn_ref[...]
    low = _lane((tm, LANES)) < DIFF_D
    q_scale = DIFF_SCALE * (LOG2E if prompt else 1.0)
    sum_rows = jnp.where(lax.broadcasted_iota(jnp.int32, (SUM_ROW_PAD, tm), 0) == 0, 1.0, 0.0).astype(BF16)

    def slot(s):
        return z[:, s * LANES:(s + 1) * LANES]

    for g in range(DIFF_KVH):
        for r in range(DIFF_GROUP):
            h = g * DIFF_GROUP + r
            qp = (slot(h) * cn + slot(_N_QSLOT + h) * sn) * q_scale
            q_ref[2 * g, r] = jnp.where(low, qp, 0.0).astype(BF16)
            q_ref[2 * g + 1, r] = jnp.where(low, 0.0, qp).astype(BF16)
        k = slot(2 * _N_QSLOT + g) * cn + slot(2 * _N_QSLOT + DIFF_KVH + g) * sn
        v = slot(2 * _N_QSLOT + 2 * DIFF_KVH + g)
        rows_ref[:, g * LANES:(g + 1) * LANES] = k
        rows_ref[:, (DIFF_KVH + g) * LANES:(DIFF_KVH + g + 1) * LANES] = v
        k_ref[g] = k.astype(BF16)
        v_ref[g, 0:LANES, :] = v.T.astype(BF16)
        v_ref[g, LANES:LANES + SUM_ROW_PAD, :] = sum_rows


def _prep_layer1(w_in_c):
    nq = DIFF_HEADS * 2 * DIFF_D
    nk = DIFF_KVH * 2 * DIFF_D
    wq, wk, wv = w_in_c[:, :nq], w_in_c[:, nq:nq + nk], w_in_c[:, nq + nk:]
    return jnp.concatenate([wq, _rot_cols(wq, DIFF_D), wk, _rot_cols(wk, DIFF_D), wv], axis=1).astype(BF16)


def _proj1(x, gain, w1, cn, sn, prompt):
    t, d = x.shape
    tm = min(t, 256)
    row = lambda w: pl.BlockSpec((tm, w), lambda i: (i, 0))
    full = lambda a: pl.BlockSpec(a.shape, lambda i: (0,) * a.ndim)
    ng = 2 * DIFF_KVH
    return pl.pallas_call(
        functools.partial(_proj1_kernel, tm=tm, prompt=prompt),
        grid=(t // tm,),
        in_specs=[row(d), full(gain), full(w1), row(LANES), row(LANES)],
        out_specs=[row(2 * DIFF_KVH * LANES),
                   pl.BlockSpec((ng, DIFF_GROUP, tm, LANES), lambda i: (0, 0, i, 0)),
                   pl.BlockSpec((DIFF_KVH, tm, LANES), lambda i: (0, i, 0)),
                   pl.BlockSpec((DIFF_KVH, LANES + SUM_ROW_PAD, tm), lambda i: (0, 0, i))],
        out_shape=[jax.ShapeDtypeStruct((t, 2 * DIFF_KVH * LANES), F32),
                   jax.ShapeDtypeStruct((ng, DIFF_GROUP, t, LANES), BF16),
                   jax.ShapeDtypeStruct((DIFF_KVH, t, LANES), BF16),
                   jax.ShapeDtypeStruct((DIFF_KVH, LANES + SUM_ROW_PAD, t), BF16)],
        compiler_params=_cparams(("parallel",)), name="proj1",
    )(x, gain, w1, cn, sn)


def _mix1_kernel(x_ref, o_ref_in, lam_ref, sub_ref, w_ref, o_ref):
    lv = lam_ref[...]
    lam = (jnp.exp(jnp.sum(lv[0:1] * lv[1:2], axis=-1, keepdims=True))
           - jnp.exp(jnp.sum(lv[2:3] * lv[3:4], axis=-1, keepdims=True)) + DIFF_LAMBDA_INIT)
    acc = x_ref[...]
    for g in range(DIFF_KVH):
        for r in range(DIFF_GROUP):
            o = o_ref_in[2 * g, r] - lam * o_ref_in[2 * g + 1, r]
            o = _rms(o, sub_ref[...]) * (1.0 - DIFF_LAMBDA_INIT)
            acc = acc + _dot(o.astype(BF16), w_ref[g * DIFF_GROUP + r])
    o_ref[...] = acc


def _mix1(x, o, lam_vecs, subln, w_out):
    t, d = x.shape
    tm = min(t, 512)
    full = lambda a: pl.BlockSpec(a.shape, lambda i: (0,) * a.ndim)
    return pl.pallas_call(
        _mix1_kernel, grid=(t // tm,),
        in_specs=[pl.BlockSpec((tm, d), lambda i: (i, 0)),
                  pl.BlockSpec((2 * DIFF_KVH, DIFF_GROUP, tm, LANES), lambda i: (0, 0, i, 0)),
                  full(lam_vecs), full(subln), full(w_out)],
        out_specs=pl.BlockSpec((tm, d), lambda i: (i, 0)), out_shape=jax.ShapeDtypeStruct((t, d), F32),
        compiler_params=_cparams(("parallel",)), name="mix1",
    )(x, o, lam_vecs, subln, w_out)


def kernel(x_prompt, x_sample, cache_mla, cache_nsa_cmp, cache_nsa_slc, state_nsa_win, cache_diff, page_table,
           attn_norm, ffn_norm, final_norm,
           w_in_a, mla_q_norm, mla_kv_norm, w_mla_uq, w_mla_uk, w_mla_uv,
           nsa_pe_k, nsa_pe_v, nsa_cmp_w1_k, nsa_cmp_w2_k, nsa_cmp_w1_v, nsa_cmp_w2_v, w_out_a,
           w_in_c, diff_lambda_q1, diff_lambda_k1, diff_lambda_q2, diff_lambda_k2, diff_subln, w_out_c,
           w_ffn_gate, w_ffn_up, w_ffn_down):
    bp, t, d = x_prompt.shape
    b, ts, _ = x_sample.shape
    assert bp == 1 and ts == 1 and t % NSA_BLOCK == 0
    n_pool = cache_mla.shape[0]
    past = page_table.shape[1] * PAGE
    assert past % NSA_BLOCK == 0
    nh = MLA_HEADS

    w0, uq, ukv_prompt, ukt, uv_pad = _prep_layer0(w_in_a, w_mla_uq, w_mla_uk, w_mla_uv)
    pe, w1c, w2c, pe_t, w1c_t = _prep_compress(nsa_pe_k, nsa_pe_v, nsa_cmp_w1_k, nsa_cmp_w1_v, nsa_cmp_w2_k,
                                               nsa_cmp_w2_v)
    w1 = _prep_layer1(w_in_c)
    wo_a = w_out_a.reshape(2 * nh, NSA_D, d)
    wo_a = jnp.concatenate([wo_a, jnp.zeros_like(wo_a)], axis=1).astype(BF16)
    wo_c = w_out_c.reshape(DIFF_HEADS, 2 * DIFF_D, d).astype(BF16)
    wg, wu, wd = (w.astype(BF16) for w in (w_ffn_gate, w_ffn_up, w_ffn_down))
    lam_vecs = jnp.stack([diff_lambda_q1, diff_lambda_k1, diff_lambda_q2, diff_lambda_k2])
    row2 = lambda v: v.reshape(1, -1)

    xp = x_prompt[0]
    xs = x_sample[:, 0]
    tab_p = _rope_tables(jnp.arange(t, dtype=jnp.int32))
    tab_s = _rope_tables(jnp.full((b,), past, jnp.int32))

    (mla_p, cmp_p, slc_p, win_p, qm_p, qc_p, qr_p, gate_p, km_p, vm_p, ksa_p, vs_p, kw_p, vw_p) = _proj0(
        xp, row2(attn_norm[0]), w0, row2(mla_q_norm), row2(mla_kv_norm), uq, ukv_prompt, tab_p, True)
    o_mla_p = _flash(qm_p.reshape(nh, 1, t, LANES), km_p, vm_p, tq=2048, tk=1024, groups=nh,
                     q_index=lambda g, k0: g, kv_index=lambda g: g, sum_row=MLA_V, out_dtype=BF16,
                     name="flash_mla")[:, 0]
    nb_p = t // NSA_BLOCK
    nbp_p = -(-nb_p // LANES) * LANES
    kcvc_p = _compress_prompt(cmp_p, pe, w1c, w2c)
    o_c_p, imp_p = _cmp_attn_prompt(kcvc_p, qc_p, nbp_p)
    q_aug = _select_prompt(imp_p, qr_p)
    keys_per_half = LANES * NSA_BLOCK
    o_s_p = _flash(q_aug, ksa_p[None], vs_p[None], tq=256, tk=1024, groups=1,
                   q_index=lambda g, k0: k0 // keys_per_half, kv_index=lambda g: 0, sum_row=NSA_D,
                   name="flash_slc")[0]
    o_w_p = _flash(qr_p[None], kw_p[None], vw_p[None], tq=128, tk=128, groups=1, window=NSA_WINDOW,
                   q_index=lambda g, k0: 0, kv_index=lambda g: 0, sum_row=NSA_D, name="flash_win")[0]
    hp = _mix0(xp, o_mla_p, o_c_p, o_s_p, o_w_p, gate_p, wo_a)
    hp = _ffn(hp, row2(ffn_norm[0]), wg[0], wu[0], wd[0], row2(final_norm), False)

    (mla_s, cmp_s, slc_s, win_s, qm_s, qc_s, qr_s, gate_s, qa_s) = _proj0(
        xs, row2(attn_norm[0]), w0, row2(mla_q_norm), row2(mla_kv_norm), uq, ukt, tab_s, False)
    q_abs = jnp.transpose(qa_s[:, :, :MLA_KV_LORA + MLA_ROPE], (1, 0, 2))
    o_lat = _decode_mla(q_abs, mla_s[:, None, :], cache_mla, page_table, 32)
    n_pages = page_table.shape[1]
    nb_sel_s = -(-(past + ts) // NSA_BLOCK)
    nbp_s = -(-nb_sel_s // LANES) * LANES
    kcvc_s = _compress_paged(cache_nsa_cmp, page_table, pe_t, w1c_t, w2c, 64)
    o_c_s, imp_s = _cmp_attn_sample(kcvc_s, jnp.transpose(qc_s, (1, 0, 2)), past, nbp_s, n_pages)
    idx_s, val_s = _select_sample(imp_s[:, 0], past, nb_sel_s, n_pages)
    o_s_s, o_w_s = _nsa_decode(idx_s, val_s, page_table, jnp.transpose(qr_s, (1, 0, 2)), slc_s[:, None, :],
                               win_s[:, None, :], state_nsa_win, cache_nsa_slc, past)
    tr = lambda a: jnp.transpose(a, (1, 0, 2))
    hs = _mix0(xs, tr(o_lat), tr(o_c_s), tr(o_s_s), tr(o_w_s), gate_s, wo_a, uv_pad)
    hs = _ffn(hs, row2(ffn_norm[0]), wg[0], wu[0], wd[0], row2(final_norm), False)

    rows_p, qd_p, kd_p, vd_p = _proj1(hp, row2(attn_norm[1]), w1, tab_p[2], tab_p[3], True)
    o_d_p = _flash(qd_p, kd_p, vd_p, tq=512, tk=1024, groups=2 * DIFF_KVH,
                   q_index=lambda g, k0: g, kv_index=lambda g: g // 2, sum_row=2 * DIFF_D, name="flash_diff")
    hp = _mix1(hp, o_d_p, lam_vecs, row2(diff_subln), wo_c)
    y_p = _ffn(hp, row2(ffn_norm[1]), wg[1], wu[1], wd[1], row2(final_norm), True)

    rows_s, qd_s, kd_s, vd_s = _proj1(hs, row2(attn_norm[1]), w1, tab_s[2], tab_s[3], False)
    qd = jnp.transpose(qd_s.reshape(_DIFF_ROWS, b, LANES), (1, 0, 2))
    o_d_s = _decode_diff(qd, rows_s[:, None, :], cache_diff, page_table, 16)
    o_d_s = jnp.transpose(o_d_s, (1, 0, 2)).reshape(2 * DIFF_KVH, DIFF_GROUP, b, LANES)
    hs = _mix1(hs, o_d_s, lam_vecs, row2(diff_subln), wo_c)
    y_s = _ffn(hs, row2(ffn_norm[1]), wg[1], wu[1], wd[1], row2(final_norm), True)

    wb_p = min(NSA_WINDOW, t)
    pair = lambda a: a.reshape(a.shape[0], 2, 1, NSA_D)
    new_win_s = jnp.concatenate([state_nsa_win, win_s.reshape(b, 1, 2, 1, NSA_D)], axis=1)[:, ts:]
    return (y_p[None], y_s[:, None], mla_p[None], pair(cmp_p)[None], pair(slc_p)[None], pair(win_p[t - wb_p:])[None],
            rows_p.reshape(1, t, 2, DIFF_KVH, 2 * DIFF_D),
            mla_s[:, None], pair(cmp_s)[:, None], pair(slc_s)[:, None], new_win_s,
            rows_s.reshape(b, 1, 2, DIFF_KVH, 2 * DIFF_D))
```

```python
import functools
import math

import numpy as np
import jax
import jax.numpy as jnp
from jax import lax
from jax.experimental import pallas as pl
from jax.experimental.pallas import tpu as pltpu

F32 = jnp.float32
BF16 = jnp.bfloat16

LANES = 128
VMEM_LIMIT = 56 * 1024 * 1024

ROPE_THETA = 10000.0
NORM_EPS = 1e-6
PAGE = 128
MLA_HEADS = 8
MLA_Q_LORA = 256
MLA_KV_LORA = 128
MLA_NOPE = 64
MLA_ROPE = 32
MLA_V = 64
MLA_SCALE = (MLA_NOPE + MLA_ROPE) ** -0.5
NSA_HEADS = 8
NSA_D = 64
NSA_BLOCK = 64
NSA_TOP_N = 16
NSA_WINDOW = 512
NSA_SCALE = NSA_D ** -0.5
NSA_FORCED = 1.0e4
DIFF_HEADS = 8
DIFF_KVH = 2
DIFF_D = 64
DIFF_GROUP = DIFF_HEADS // DIFF_KVH
DIFF_SCALE = DIFF_D ** -0.5
DIFF_LAMBDA_INIT = 0.8 - 0.6 * math.exp(-0.3 * 1)

LOG2E = math.log2(math.e)
SUM_ROW_PAD = 16
NEG_BIG = -1.0e30
SEL_NEG = -float(2 ** 40)


def _cparams(sem):
    return pltpu.CompilerParams(dimension_semantics=sem, vmem_limit_bytes=VMEM_LIMIT)


def _rms(x, g):
    return x * lax.rsqrt(jnp.mean(x * x, axis=-1, keepdims=True) + NORM_EPS) * g


def _dot(a, b):
    return jnp.dot(a, b, preferred_element_type=F32)


def _dot_nt(a, b):
    return lax.dot_general(a, b, (((1,), (1,)), ((), ())), preferred_element_type=F32)


def _lane(shape):
    return lax.broadcasted_iota(jnp.int32, shape, len(shape) - 1)


_S_CQ, _S_CKV, _S_KR, _S_KR_ROT, _S_QN, _S_QN_ROT = 0, 2, 3, 4, 5, 9
_S_CMP, _S_SLC, _S_SLC_ROT, _S_WIN, _S_WIN_ROT, _S_GATE, _N_SLOTS0 = 13, 14, 15, 16, 17, 18, 19
_VT_ROWS = NSA_D + SUM_ROW_PAD


def _proj0_kernel(x_ref, g_ref, w_ref, qn_ref, kvn_ref, uq_ref, ukv_ref, c32_ref, s32_ref, cn_ref, sn_ref,
                  *out_refs, prompt, tm):
    if prompt:
        (mla_ref, cmp_ref, slc_ref, win_ref, qm_ref, qc_ref, qr_ref, gate_ref,
         km_ref, vm_ref, ksa_ref, vs_ref, kw_ref, vw_ref) = out_refs
    else:
        (mla_ref, cmp_ref, slc_ref, win_ref, qm_ref, qc_ref, qr_ref, gate_ref, qa_ref) = out_refs

    hn = _rms(x_ref[...], g_ref[...]).astype(BF16)
    z = _dot(hn, w_ref[...])

    def slot(s, n=1):
        return z[:, s * LANES:(s + n) * LANES]

    c32, s32, cn, sn = c32_ref[...], s32_ref[...], cn_ref[...], sn_ref[...]
    lane = _lane((tm, LANES))
    low = lane < NSA_D

    cqn = _rms(slot(_S_CQ, 2), qn_ref[...]).astype(BF16)
    zq = _dot(cqn, uq_ref[...])
    ckvn = _rms(slot(_S_CKV), kvn_ref[...])
    kpe_hi = slot(_S_KR) * c32 + slot(_S_KR_ROT) * s32
    mla_ref[:, 0:MLA_KV_LORA] = ckvn
    mla_ref[:, MLA_KV_LORA:MLA_KV_LORA + MLA_ROPE] = pltpu.roll(kpe_hi, 64, axis=1)[:, 0:MLA_ROPE]
    ckvb = ckvn.astype(BF16)
    q_cos = jnp.where(low, 1.0, c32)
    nh = MLA_HEADS
    flash_scale = LOG2E if prompt else 1.0
    sum_row = jnp.where(lane == NSA_D, 1.0, 0.0)
    for h in range(nh):
        qh = (zq[:, h * LANES:(h + 1) * LANES] * q_cos
              + zq[:, (nh + h) * LANES:(nh + h + 1) * LANES] * s32) * (MLA_SCALE * flash_scale)
        qm_ref[h] = qh.astype(BF16)
        if not prompt:
            qlat = _dot(qh.astype(BF16), ukv_ref[h])
            qa_ref[h, :, 0:LANES] = qlat.astype(BF16)
            qa_ref[h, :, LANES:2 * LANES] = jnp.where(lane < MLA_ROPE, pltpu.roll(qh, 64, axis=1), 0.0).astype(BF16)
    if prompt:
        zk = _dot(ckvb, ukv_ref[0])
        zv = _dot(ckvb, ukv_ref[1])
        for h in range(nh):
            km_ref[h] = (zk[:, h * LANES:(h + 1) * LANES] + kpe_hi).astype(BF16)
            vm_ref[h] = (zv[:, h * LANES:(h + 1) * LANES] + sum_row).T[0:_VT_ROWS].astype(BF16)

    for p in range(NSA_HEADS // 2):
        plain = slot(_S_QN + p) * NSA_SCALE
        roped = (slot(_S_QN + p) * cn + slot(_S_QN_ROT + p) * sn) * (NSA_SCALE * flash_scale)
        qc_ref[2 * p] = jnp.where(low, plain, 0.0).astype(BF16)
        qc_ref[2 * p + 1] = jnp.where(low, 0.0, plain).astype(BF16)
        qr_ref[2 * p] = jnp.where(low, roped, 0.0).astype(BF16)
        qr_ref[2 * p + 1] = jnp.where(low, 0.0, roped).astype(BF16)

    ck = jnp.where(low, cn, 1.0)
    cmp_ref[...] = slot(_S_CMP)
    slc = slot(_S_SLC) * ck + slot(_S_SLC_ROT) * sn
    win = slot(_S_WIN) * ck + slot(_S_WIN_ROT) * sn
    slc_ref[...] = slc
    win_ref[...] = win
    gate_ref[...] = jax.nn.sigmoid(slot(_S_GATE))
    if prompt:
        slc_sw = pltpu.roll(slc, 64, axis=1)
        win_sw = pltpu.roll(win, 64, axis=1)
        row = pl.program_id(0) * tm + lax.broadcasted_iota(jnp.int32, (tm, LANES), 0)
        onehot = ((row >> 6) & (LANES - 1)) == lane
        ksa_ref[:, 0:LANES] = jnp.where(onehot, 1.0, 0.0).astype(BF16)
        ksa_ref[:, LANES:2 * LANES] = jnp.where(low, slc, slc_sw).astype(BF16)
        is_sum = lane == NSA_D
        keep = lane <= NSA_D
        vs_ref[...] = jnp.where(is_sum, 1.0, jnp.where(keep, slc_sw, 0.0)).T[0:_VT_ROWS].astype(BF16)
        kw_ref[...] = jnp.where(low, win, win_sw).astype(BF16)
        vw_ref[...] = jnp.where(is_sum, 1.0, jnp.where(keep, win_sw, 0.0)).T[0:_VT_ROWS].astype(BF16)


def _rot_cols(w, d):
    k = w.shape[0]
    w3 = w.reshape(k, -1, d)
    return jnp.concatenate([-w3[..., d // 2:], w3[..., :d // 2]], axis=-1).reshape(k, -1)


def _place(w, left, width=LANES):
    return jnp.pad(w, ((0, 0), (left, width - left - w.shape[1])))


def _prep_layer0(w_in_a, w_mla_uq, w_mla_uk, w_mla_uv):
    o = np.cumsum([0, 256, 128, 32, 512, 64, 64, 64, 64, 64, 64, 24]).tolist()
    c_q, c_kv, k_r, q_n = (w_in_a[:, o[i]:o[i + 1]] for i in range(4))
    kv_c = w_in_a[:, o[4]:o[6]]
    kv_s = w_in_a[:, o[6]:o[8]]
    kv_w = w_in_a[:, o[8]:o[10]]
    g = w_in_a[:, o[10]:o[11]]
    z64 = jnp.zeros((w_in_a.shape[0], 64), F32)
    cols = [c_q, c_kv, _place(k_r, 64), _place(_rot_cols(k_r, MLA_ROPE), 64), q_n, _rot_cols(q_n, NSA_D),
            kv_c,
            kv_s, jnp.concatenate([_rot_cols(kv_s[:, :64], 64), z64], axis=1),
            kv_w, jnp.concatenate([_rot_cols(kv_w[:, :64], 64), z64], axis=1),
            _place(g, 0)]
    w0 = jnp.concatenate(cols, axis=1).astype(BF16)
    uq3 = w_mla_uq.reshape(MLA_Q_LORA, MLA_HEADS, MLA_NOPE + MLA_ROPE)
    nope, pe = uq3[..., :MLA_NOPE], uq3[..., MLA_NOPE:]
    pe_rot = jnp.concatenate([-pe[..., MLA_ROPE // 2:], pe[..., :MLA_ROPE // 2]], axis=-1)
    pad32 = jnp.zeros(pe.shape, F32)
    plain = jnp.concatenate([nope, pe, pad32], axis=-1).reshape(MLA_Q_LORA, -1)
    rot = jnp.concatenate([jnp.zeros(nope.shape, F32), pe_rot, pad32], axis=-1).reshape(MLA_Q_LORA, -1)
    uq = jnp.concatenate([plain, rot], axis=1).astype(BF16)
    pad_k = jnp.zeros((MLA_KV_LORA, MLA_HEADS, LANES - MLA_NOPE), F32)
    uk_slots = jnp.concatenate([w_mla_uk, pad_k], axis=-1).reshape(MLA_KV_LORA, -1)
    uv_slots = jnp.concatenate([w_mla_uv, pad_k], axis=-1).reshape(MLA_KV_LORA, -1)
    ukv_prompt = jnp.stack([uk_slots, uv_slots]).astype(BF16)
    ukt = jnp.transpose(w_mla_uk, (1, 2, 0))
    ukt = jnp.concatenate([ukt, jnp.zeros_like(ukt)], axis=1).astype(BF16)
    uv_pad = jnp.transpose(jnp.concatenate([w_mla_uv, pad_k], axis=-1), (1, 0, 2)).astype(BF16)
    return w0, uq, ukv_prompt, ukt, uv_pad


def _rope_tables(pos):
    def cs(dim):
        inv = ROPE_THETA ** (-jnp.arange(0, dim, 2, dtype=F32) / dim)
        ang = pos.astype(F32)[:, None] * inv[None, :]
        ang = jnp.concatenate([ang, ang], axis=-1)
        return jnp.cos(ang), jnp.sin(ang)
    c32, s32 = cs(MLA_ROPE)
    cn, sn = cs(NSA_D)
    return (_place(c32, 64), _place(s32, 64), jnp.concatenate([cn, cn], axis=1), jnp.concatenate([sn, sn], axis=1))


def _proj0(x, gain, w0, qnorm, kvnorm, uq, ukv, tables, prompt):
    t, d = x.shape
    tm = min(t, 256)
    nh = MLA_HEADS
    row = lambda w: pl.BlockSpec((tm, w), lambda i: (i, 0))
    head = lambda w: pl.BlockSpec((nh, tm, w), lambda i: (0, i, 0))
    full = lambda a: pl.BlockSpec(a.shape, lambda i: (0,) * a.ndim)
    out_shape = [jax.ShapeDtypeStruct((t, MLA_KV_LORA + MLA_ROPE), F32)] + [jax.ShapeDtypeStruct((t, LANES), F32)] * 3
    out_specs = [row(MLA_KV_LORA + MLA_ROPE), row(LANES), row(LANES), row(LANES)]
    out_shape += [jax.ShapeDtypeStruct((nh, t, LANES), BF16)] * 3 + [jax.ShapeDtypeStruct((t, LANES), F32)]
    out_specs += [head(LANES)] * 3 + [row(LANES)]
    if prompt:
        col = pl.BlockSpec((_VT_ROWS, tm), lambda i: (0, i))
        out_shape += [jax.ShapeDtypeStruct((nh, t, LANES), BF16), jax.ShapeDtypeStruct((nh, _VT_ROWS, t), BF16)]
        out_specs += [head(LANES), pl.BlockSpec((nh, _VT_ROWS, tm), lambda i: (0, 0, i))]
        out_shape += [jax.ShapeDtypeStruct((t, 2 * LANES), BF16), jax.ShapeDtypeStruct((_VT_ROWS, t), BF16),
                      jax.ShapeDtypeStruct((t, LANES), BF16), jax.ShapeDtypeStruct((_VT_ROWS, t), BF16)]
        out_specs += [row(2 * LANES), col, row(LANES), col]
    else:
        out_shape += [jax.ShapeDtypeStruct((nh, t, 2 * LANES), BF16)]
        out_specs += [head(2 * LANES)]
    return pl.pallas_call(
        functools.partial(_proj0_kernel, prompt=prompt, tm=tm),
        grid=(t // tm,),
        in_specs=[row(d), full(gain), full(w0), full(qnorm), full(kvnorm), full(uq), full(ukv)] + [row(LANES)] * 4,
        out_specs=out_specs, out_shape=out_shape,
        compiler_params=_cparams(("parallel",)),
        name="proj0_prompt" if prompt else "proj0_sample",
    )(x, gain, w0, qnorm, kvnorm, uq, ukv, *tables)


_CMP_PAIR = 2
_STAGE_PITCH = PAGE + 8


def _compress_rows(src_ref, n_blocks, pe_ref, w1_ref):
    acc = None
    for r0 in range(0, NSA_BLOCK, _CMP_PAIR):
        xs = [src_ref[pl.ds(r0 + u, n_blocks, stride=NSA_BLOCK), :] + pe_ref[r0 + u:r0 + u + 1, :]
              for u in range(_CMP_PAIR)]
        part = _dot(jnp.concatenate(xs, axis=1).astype(BF16), w1_ref[r0 // _CMP_PAIR])
        acc = part if acc is None else acc + part
    return acc


def _compress_prompt_kernel(f_ref, pe_ref, w1_ref, w2_ref, o_ref, *, n_blocks):
    pre = _compress_rows(f_ref, n_blocks, pe_ref, w1_ref)
    o_ref[...] = _dot(jax.nn.silu(pre).astype(BF16), w2_ref[...])


def _compress_paged_kernel(pt_ref, *refs, pages):
    page_refs = refs[:pages]
    pe_ref, w1_ref, w2_ref, o_ref, stage, pre_k, pre_v = refs[pages:]
    ch = pl.program_id(1)
    for p in range(pages):
        stage[_STAGE_PITCH * p:_STAGE_PITCH * p + PAGE, :] = page_refs[p][0]
    for kv, pre in enumerate((pre_k, pre_v)):
        acc = None
        for f0 in range(0, NSA_D, _CMP_PAIR):
            xs = [stage[pl.ds(kv * NSA_D + f0 + u, pages, stride=_STAGE_PITCH), :]
                  + pe_ref[kv * NSA_D + f0 + u:kv * NSA_D + f0 + u + 1, :] for u in range(_CMP_PAIR)]
            part = _dot(jnp.concatenate(xs, axis=1).astype(BF16), w1_ref[kv, f0 // _CMP_PAIR])
            acc = part if acc is None else acc + part
        pre[pl.ds(pl.multiple_of(ch * pages, pages), pages), :] = acc

    @pl.when(ch == pl.num_programs(1) - 1)
    def _():
        pk, pv = pre_k[...], pre_v[...]
        low = _lane(pk.shape) < NSA_D
        even = jnp.where(low, pk, pltpu.roll(pv, 64, axis=1))
        odd = jnp.where(low, pltpu.roll(pk, 64, axis=1), pv)
        both = jnp.concatenate([even, odd], axis=0)
        o_ref[0] = _dot(jax.nn.silu(both).astype(BF16), w2_ref[...])


def _prep_compress(pe_k, pe_v, w1_k, w1_v, w2_k, w2_v):
    pe = jnp.concatenate([pe_k, pe_v], axis=1)
    w1k3 = w1_k.reshape(NSA_BLOCK, NSA_D, NSA_D)
    w1v3 = w1_v.reshape(NSA_BLOCK, NSA_D, NSA_D)
    w1 = jnp.zeros((NSA_BLOCK, 2, NSA_D, 2 * NSA_D), F32)
    w1 = w1.at[:, 0, :, :NSA_D].set(w1k3).at[:, 1, :, NSA_D:].set(w1v3)
    w1 = w1.reshape(NSA_BLOCK // _CMP_PAIR, _CMP_PAIR * LANES, 2 * NSA_D).astype(BF16)
    w2 = jnp.zeros((2 * NSA_D, 2 * NSA_D), F32)
    w2 = w2.at[:NSA_D, :NSA_D].set(w2_k).at[NSA_D:, NSA_D:].set(w2_v).astype(BF16)
    pe_t = jnp.tile(jnp.concatenate([pe_k.T, pe_v.T], axis=0), (1, 2))
    def per_dim(w3):
        wd = jnp.transpose(w3, (1, 0, 2))
        z = jnp.zeros_like(wd)
        top = jnp.concatenate([wd, z], axis=2)
        bot = jnp.concatenate([z, wd], axis=2)
        return jnp.concatenate([top, bot], axis=1)
    w1_t = jnp.stack([per_dim(w1k3), per_dim(w1v3)])
    w1_t = w1_t.reshape(2, NSA_D // _CMP_PAIR, _CMP_PAIR * LANES, LANES).astype(BF16)
    return pe, w1, w2, pe_t, w1_t


def _compress_prompt(cmp_rows, pe, w1, w2):
    t = cmp_rows.shape[0]
    nb = t // NSA_BLOCK
    full = lambda a: pl.BlockSpec(a.shape, lambda i: (0,) * a.ndim)
    return pl.pallas_call(
        functools.partial(_compress_prompt_kernel, n_blocks=nb),
        grid=(1,), in_specs=[full(cmp_rows), full(pe), full(w1), full(w2)],
        out_specs=pl.BlockSpec((nb, LANES), lambda i: (0, 0)),
        out_shape=jax.ShapeDtypeStruct((nb, LANES), F32),
        compiler_params=_cparams(("arbitrary",)), name="compress_prompt",
    )(cmp_rows, pe, w1, w2)


def _feature_major(cache):
    n, rows = cache.shape[:2]
    return jnp.transpose(cache, (0, 2, 3, 4, 1)).reshape(n, 2 * NSA_D, rows)


def _compress_paged(cache_cmp, page_table, pe_t, w1_t, w2, pages):
    b, n_pages = page_table.shape
    nb = n_pages * PAGE // NSA_BLOCK
    view = _feature_major(cache_cmp)
    pages = min(pages, n_pages)
    page_specs = [pl.BlockSpec((1, 2 * NSA_D, PAGE),
                               functools.partial(lambda i, c, pt, p: (pt[i, c * pages + p], 0, 0), p=p))
                  for p in range(pages)]
    full = lambda a: pl.BlockSpec(a.shape, lambda i, c, pt: (0,) * a.ndim)
    return pl.pallas_call(
        functools.partial(_compress_paged_kernel, pages=pages),
        grid_spec=pltpu.PrefetchScalarGridSpec(
            num_scalar_prefetch=1, grid=(b, n_pages // pages),
            in_specs=page_specs + [full(pe_t), full(w1_t), full(w2)],
            out_specs=pl.BlockSpec((1, nb, LANES), lambda i, c, pt: (i, 0, 0)),
            scratch_shapes=[pltpu.VMEM((_STAGE_PITCH * pages, LANES), F32), pltpu.VMEM((n_pages, LANES), F32),
                            pltpu.VMEM((n_pages, LANES), F32)]),
        out_shape=jax.ShapeDtypeStruct((b, nb, LANES), F32),
        compiler_params=_cparams(("parallel", "arbitrary")), name="compress_paged",
    )(page_table, *([view] * pages), pe_t, w1_t, w2)


def _block_id(lane, perm_pages):
    if not perm_pages:
        return lane
    assert perm_pages & (perm_pages - 1) == 0
    shift = perm_pages.bit_length() - 1
    return jnp.where(lane < 2 * perm_pages, 2 * (lane & (perm_pages - 1)) + (lane >> shift), lane)


def _cmp_attn_kernel(kv_ref, q_ref, o_ref, imp_ref, *, rows_per_query, tq, pos0, pos_step, nbp, perm_pages):
    kv = kv_ref[...] if len(kv_ref.shape) == 2 else kv_ref[0]
    nb = kv.shape[0]
    sw = pltpu.roll(kv, 64, axis=1)
    kdup = jnp.where(_lane(kv.shape) < NSA_D, kv, sw).astype(BF16)
    vk = sw.astype(BF16)
    if rows_per_query:
        q = q_ref[0]
        qpos = jnp.full((q.shape[0], 1), pos0, jnp.int32)
    else:
        q = q_ref[...].reshape(NSA_HEADS * tq, LANES)
        r = lax.broadcasted_iota(jnp.int32, (NSA_HEADS * tq, 1), 0)
        qpos = pos0 + pos_step * (pl.program_id(0) * tq + (r & (tq - 1)))
    s = _dot_nt(q, kdup)
    last = (_block_id(_lane(s.shape), perm_pages) + 1) * NSA_BLOCK - 1
    s = jnp.where(last <= qpos, s, -jnp.inf)
    m = jnp.max(s, axis=-1, keepdims=True)
    m = jnp.where(m == -jnp.inf, 0.0, m)
    e = jnp.exp(s - m)
    den = jnp.sum(e, axis=-1, keepdims=True)
    p = e / jnp.where(den > 0.0, den, 1.0)
    o = _dot(p.astype(BF16), vk)
    if rows_per_query:
        o_ref[0] = o
        imp = jnp.sum(p, axis=0, keepdims=True)
        if nbp > nb:
            imp = jnp.concatenate([imp, jnp.zeros((1, nbp - nb), F32)], axis=1)
        imp_ref[0] = imp
    else:
        o_ref[...] = o.reshape(NSA_HEADS, tq, LANES)
        imp = jnp.sum(p.reshape(NSA_HEADS, tq, nb), axis=0)
        if nbp > nb:
            imp = jnp.concatenate([imp, jnp.zeros((tq, nbp - nb), F32)], axis=1)
        imp_ref[...] = imp


def _cmp_attn_prompt(kcvc, q_cmp, nbp):
    nh, t, _ = q_cmp.shape
    nb = kcvc.shape[0]
    tq = min(t, 128)
    return pl.pallas_call(
        functools.partial(_cmp_attn_kernel, rows_per_query=False, tq=tq, pos0=0, pos_step=1, nbp=nbp, perm_pages=0),
        grid=(t // tq,),
        in_specs=[pl.BlockSpec((nb, LANES), lambda i: (0, 0)), pl.BlockSpec((nh, tq, LANES), lambda i: (0, i, 0))],
        out_specs=[pl.BlockSpec((nh, tq, LANES), lambda i: (0, i, 0)), pl.BlockSpec((tq, nbp), lambda i: (i, 0))],
        out_shape=[jax.ShapeDtypeStruct((nh, t, LANES), F32), jax.ShapeDtypeStruct((t, nbp), F32)],
        compiler_params=_cparams(("parallel",)), name="cmp_attn_prompt",
    )(kcvc, q_cmp)


def _cmp_attn_sample(kcvc, q_cmp, pos, nbp, perm_pages):
    b, nh, _ = q_cmp.shape
    nb = kcvc.shape[1]
    return pl.pallas_call(
        functools.partial(_cmp_attn_kernel, rows_per_query=True, tq=1, pos0=pos, pos_step=0, nbp=nbp,
                          perm_pages=perm_pages),
        grid=(b,),
        in_specs=[pl.BlockSpec((1, nb, LANES), lambda i: (i, 0, 0)), pl.BlockSpec((1, nh, LANES), lambda i: (i, 0, 0))],
        out_specs=[pl.BlockSpec((1, nh, LANES), lambda i: (i, 0, 0)), pl.BlockSpec((1, 1, nbp), lambda i: (i, 0, 0))],
        out_shape=[jax.ShapeDtypeStruct((b, nh, LANES), F32), jax.ShapeDtypeStruct((b, 1, nbp), F32)],
        compiler_params=_cparams(("parallel",)), name="cmp_attn_sample",
    )(kcvc, q_cmp)


def _select_kernel(imp_ref, *refs, tq, pos0, pos_step, nb_sel, n_halves, with_query, perm_pages):
    if with_query:
        q_ref, qa_ref = refs
    else:
        idx_ref, val_ref = refs
    imp = imp_ref[...]
    nbp = imp.shape[1]
    lane = _block_id(_lane(imp.shape), perm_pages)
    qpos = pos0 + pos_step * (pl.program_id(0) * tq + lax.broadcasted_iota(jnp.int32, (tq, 1), 0))
    cur = qpos >> 6
    forced = (lane == 0) | (lane == cur) | (lane == cur - 1)
    score = jnp.where(forced, NSA_FORCED, imp)
    work = jnp.where((lane <= cur) & (lane < nb_sel), score, -jnp.inf)
    lane_f = lane.astype(F32)
    neg = jnp.full(imp.shape, SEL_NEG, F32)
    lane16 = _lane((tq, LANES))
    idx_acc = jnp.zeros((tq, LANES), jnp.int32)
    val_acc = jnp.zeros((tq, LANES), jnp.int32)
    for n in range(min(NSA_TOP_N, nb_sel)):
        mx = jnp.max(work, axis=-1, keepdims=True)
        first = jnp.min(jnp.where(work == mx, lane_f, float(nbp)), axis=-1, keepdims=True)
        pick = lane_f == first
        valid_neg = jnp.where(mx > -jnp.inf, 0.0, SEL_NEG)
        neg = jnp.where(pick, jnp.maximum(neg, valid_neg), neg)
        work = jnp.where(pick, -jnp.inf, work)
        if not with_query:
            idx_acc = jnp.where(lane16 == n, first.astype(jnp.int32), idx_acc)
            val_acc = jnp.where(lane16 == n, jnp.where(mx > -jnp.inf, 1, 0), val_acc)
    if with_query:
        neg = neg.astype(BF16)
        for hf in range(n_halves):
            part = neg[:, hf * LANES:(hf + 1) * LANES]
            for h in range(NSA_HEADS):
                qa_ref[hf, h, :, 0:LANES] = part
                qa_ref[hf, h, :, LANES:2 * LANES] = q_ref[h]
    else:
        idx_ref[...] = idx_acc
        val_ref[...] = val_acc


def _select_prompt(imp, q_rot):
    t, nbp = imp.shape
    nh = q_rot.shape[0]
    tq = min(t, 512)
    n_halves = nbp // LANES
    return pl.pallas_call(
        functools.partial(_select_kernel, tq=tq, pos0=0, pos_step=1, nb_sel=t // NSA_BLOCK, n_halves=n_halves,
                          with_query=True, perm_pages=0),
        grid=(t // tq,),
        in_specs=[pl.BlockSpec((tq, nbp), lambda i: (i, 0)), pl.BlockSpec((nh, tq, LANES), lambda i: (0, i, 0))],
        out_specs=pl.BlockSpec((n_halves, nh, tq, 2 * LANES), lambda i: (0, 0, i, 0)),
        out_shape=jax.ShapeDtypeStruct((n_halves, nh, t, 2 * LANES), BF16),
        compiler_params=_cparams(("parallel",)), name="select_prompt",
    )(imp, q_rot)


def _select_sample(imp, pos, nb_sel, perm_pages):
    b, nbp = imp.shape
    return pl.pallas_call(
        functools.partial(_select_kernel, tq=b, pos0=pos, pos_step=0, nb_sel=nb_sel, n_halves=0, with_query=False,
                          perm_pages=perm_pages),
        grid=(1,),
        in_specs=[pl.BlockSpec((b, nbp), lambda i: (0, 0))],
        out_specs=[pl.BlockSpec((b, LANES), lambda i: (0, 0))] * 2,
        out_shape=[jax.ShapeDtypeStruct((b, LANES), jnp.int32)] * 2,
        compiler_params=_cparams(("arbitrary",)), name="select_sample",
    )(imp)


_FLASH_CQ = 256


def _flash_kernel(qi_ref, kj_ref, flag_ref, q_ref, k_ref, vt_ref, o_ref, m_scr, acc_scr, *, r, tq, tk, window,
                  sum_row):
    step_id = pl.program_id(1)
    i, kv_blk, flags = qi_ref[step_id], kj_ref[step_id], flag_ref[step_id]
    m_rows = r * tq
    cq = min(_FLASH_CQ, m_rows)

    @pl.when((flags & 1) != 0)
    def _():
        m_scr[...] = jnp.full(m_scr.shape, NEG_BIG, F32)
        acc_scr[...] = jnp.zeros(acc_scr.shape, F32)

    need_mask = None if window else kv_blk * tk + (tk - 1) > i * tq

    def q_chunk(c):
        if tq >= cq:
            per_head = tq // cq
            return q_ref[0, c // per_head, (c % per_head) * cq:(c % per_head + 1) * cq, :]
        heads = cq // tq
        return q_ref[0, c * heads:(c + 1) * heads].reshape(cq, q_ref.shape[-1])

    def scores(c, masked):
        s = _dot_nt(k_ref[0], q_chunk(c))
        if masked:
            qpos = i * tq + ((c * cq + _lane((tk, cq))) & (tq - 1))
            kpos = kv_blk * tk + lax.broadcasted_iota(jnp.int32, (tk, cq), 0)
            ok = kpos <= qpos
            if window:
                ok = ok & (qpos - kpos < window)
            s = jnp.where(ok, s, NEG_BIG)
        return s

    def step(masked):
        m_all = m_scr[...]
        n_chunks = m_rows // cq
        def accumulate(cols, alpha, p):
            acc_scr[:, cols] = alpha * acc_scr[:, cols] + _dot(vt_ref[0], p)

        s_next = scores(0, masked)
        pending = None
        for c in range(n_chunks):
            cols = slice(c * cq, (c + 1) * cq)
            s = s_next
            if c + 1 < n_chunks:
                s_next = scores(c + 1, masked)
            m_prev = m_all[:, cols]
            m_new = jnp.maximum(m_prev, jnp.max(s, axis=0, keepdims=True))
            alpha = jnp.exp2(m_prev - m_new)
            p = jnp.exp2(s - m_new).astype(BF16)
            m_scr[:, cols] = m_new
            if pending is not None:
                accumulate(*pending)
            pending = (cols, alpha, p)
        accumulate(*pending)

    if window:
        step(True)
    else:
        pl.when(need_mask)(lambda: step(True))
        pl.when(jnp.logical_not(need_mask))(lambda: step(False))

    @pl.when((flags & 2) != 0)
    def _():
        vals = acc_scr[0:sum_row, :] / acc_scr[sum_row:sum_row + 1, :]
        if sum_row < LANES:
            vals = jnp.concatenate([vals, jnp.zeros((LANES - sum_row, m_rows), F32)], axis=0)
        o_ref[0] = vals.T.reshape(r, tq, LANES).astype(o_ref.dtype)


def _flash(q, k, vt, *, tq, tk, groups, q_index, kv_index, sum_row, window=0, out_dtype=F32, name):
    r, t, dq = q.shape[1:]
    dv = vt.shape[1]
    tq, tk = min(tq, t), min(tk, t)
    qi, kj, flags = [], [], []
    for i in range(t // tq):
        if window:
            assert tq == tk and window % tk == 0
            blocks = [b for b in range(i - window // tk, i + 1) if b >= 0]
        else:
            blocks = list(range((i * tq + tq - 1) // tk + 1))
        for n, b in enumerate(blocks):
            qi.append(i)
            kj.append(b)
            flags.append((1 if n == 0 else 0) | (2 if n == len(blocks) - 1 else 0))
    qi, kj, flags = (jnp.asarray(np.array(a, np.int32)) for a in (qi, kj, flags))
    return pl.pallas_call(
        functools.partial(_flash_kernel, r=r, tq=tq, tk=tk, window=window, sum_row=sum_row),
        grid_spec=pltpu.PrefetchScalarGridSpec(
            num_scalar_prefetch=3, grid=(groups, int(qi.shape[0])),
            in_specs=[pl.BlockSpec((1, r, tq, dq), lambda g, s, qi, kj, fl: (q_index(g, kj[s] * tk), 0, qi[s], 0)),
                      pl.BlockSpec((1, tk, dq), lambda g, s, qi, kj, fl: (kv_index(g), kj[s], 0)),
                      pl.BlockSpec((1, dv, tk), lambda g, s, qi, kj, fl: (kv_index(g), 0, kj[s]))],
            out_specs=pl.BlockSpec((1, r, tq, LANES), lambda g, s, qi, kj, fl: (g, 0, qi[s], 0)),
            scratch_shapes=[pltpu.VMEM((1, r * tq), F32), pltpu.VMEM((dv, r * tq), F32)]),
        out_shape=jax.ShapeDtypeStruct((groups, r, t, LANES), out_dtype),
        compiler_params=_cparams(("parallel", "arbitrary")), name=name,
    )(qi, kj, flags, q, k, vt)


def _online_softmax_step(s, m_scr, l_scr):
    m_prev = m_scr[...]
    m_new = jnp.maximum(m_prev, jnp.max(s, axis=-1, keepdims=True))
    alpha = jnp.exp(m_prev - m_new)
    p = jnp.exp(s - m_new)
    l_scr[...] = alpha * l_scr[...] + jnp.sum(p, axis=-1, keepdims=True)
    m_scr[...] = m_new
    return alpha, p


def _decode_mla_kernel(pt_ref, q_ref, new_ref, *refs, pages):
    page_refs = refs[:pages]
    o_ref, m_scr, l_scr, acc_scr = refs[pages:]
    c = pl.program_id(1)
    q = q_ref[0]

    @pl.when(c == 0)
    def _():
        new = new_ref[0].astype(BF16).astype(F32)
        m_scr[...] = jnp.sum(q.astype(F32) * new, axis=-1, keepdims=True)
        l_scr[...] = jnp.ones(l_scr.shape, F32)
        acc_scr[...] = jnp.broadcast_to(new[:, 0:MLA_KV_LORA], acc_scr.shape)

    kt = jnp.concatenate([pr[0] for pr in page_refs], axis=1).astype(BF16)
    alpha, p = _online_softmax_step(_dot(q, kt), m_scr, l_scr)
    acc_scr[...] = alpha * acc_scr[...] + _dot_nt(p.astype(BF16), kt[0:MLA_KV_LORA])

    @pl.when(c == pl.num_programs(1) - 1)
    def _():
        o_ref[0] = acc_scr[...] / l_scr[...]


def _decode_mla(q, new_rows, cache_mla, page_table, pages):
    b, rows, kw = q.shape
    n_pages = page_table.shape[1]
    pages = min(pages, n_pages)
    view = jnp.transpose(cache_mla, (0, 2, 1))
    page_specs = [pl.BlockSpec((1, kw, PAGE), functools.partial(lambda i, c, pt, p: (pt[i, c * pages + p], 0, 0), p=p))
                  for p in range(pages)]
    return pl.pallas_call(
        functools.partial(_decode_mla_kernel, pages=pages),
        grid_spec=pltpu.PrefetchScalarGridSpec(
            num_scalar_prefetch=1, grid=(b, n_pages // pages),
            in_specs=[pl.BlockSpec((1, rows, kw), lambda i, c, pt: (i, 0, 0)),
                      pl.BlockSpec((1, 1, kw), lambda i, c, pt: (i, 0, 0))] + page_specs,
            out_specs=pl.BlockSpec((1, rows, MLA_KV_LORA), lambda i, c, pt: (i, 0, 0)),
            scratch_shapes=[pltpu.VMEM((rows, 1), F32), pltpu.VMEM((rows, 1), F32),
                            pltpu.VMEM((rows, MLA_KV_LORA), F32)]),
        out_shape=jax.ShapeDtypeStruct((b, rows, MLA_KV_LORA), F32),
        compiler_params=_cparams(("parallel", "arbitrary")), name="decode_mla",
    )(page_table, q, new_rows, *([view] * pages))


_DIFF_ROWS = 2 * DIFF_KVH * DIFF_GROUP
_DIFF_COLS = 2 * DIFF_KVH


def _decode_diff_kernel(pt_ref, q_ref, new_ref, *refs, pages):
    page_refs = refs[:pages]
    o_ref, m_scr, l_scr, acc_scr = refs[pages:]
    c = pl.program_id(1)
    q = q_ref[0]
    head = lax.broadcasted_iota(jnp.int32, (_DIFF_ROWS, 1), 0) >> 3

    @pl.when(c == 0)
    def _():
        new = new_ref[0].astype(BF16).astype(F32)
        knew = jnp.where(head == 0, new[:, 0:LANES], new[:, LANES:2 * LANES])
        vnew = jnp.where(head == 0, new[:, 2 * LANES:3 * LANES], new[:, 3 * LANES:4 * LANES])
        m_scr[...] = jnp.sum(q.astype(F32) * knew, axis=-1, keepdims=True)
        l_scr[...] = jnp.ones(l_scr.shape, F32)
        acc_scr[...] = vnew

    def cache_rows(c):
        return jnp.concatenate([pr[0, pl.ds(c, PAGE, stride=_DIFF_COLS), :] for pr in page_refs],
                               axis=0).astype(BF16)

    rows_q = _DIFF_ROWS // DIFF_KVH
    qf = q.astype(F32)
    s = jnp.concatenate([_dot_nt(qf[g * rows_q:(g + 1) * rows_q].astype(BF16), cache_rows(g))
                         for g in range(DIFF_KVH)], axis=0)
    alpha, p = _online_softmax_step(s, m_scr, l_scr)
    pv = jnp.concatenate([_dot(p[g * rows_q:(g + 1) * rows_q].astype(BF16), cache_rows(DIFF_KVH + g))
                          for g in range(DIFF_KVH)], axis=0)
    acc_scr[...] = alpha * acc_scr[...] + pv

    @pl.when(c == pl.num_programs(1) - 1)
    def _():
        o_ref[0] = acc_scr[...] / l_scr[...]


def _decode_diff(q, new_rows, cache_diff, page_table, pages):
    b, rows, _ = q.shape
    n_pool = cache_diff.shape[0]
    n_pages = page_table.shape[1]
    pages = min(pages, n_pages)
    view = cache_diff.reshape(n_pool, PAGE * _DIFF_COLS, LANES)
    page_specs = [pl.BlockSpec((1, PAGE * _DIFF_COLS, LANES),
                               functools.partial(lambda i, c, pt, p: (pt[i, c * pages + p], 0, 0), p=p))
                  for p in range(pages)]
    return pl.pallas_call(
        functools.partial(_decode_diff_kernel, pages=pages),
        grid_spec=pltpu.PrefetchScalarGridSpec(
            num_scalar_prefetch=1, grid=(b, n_pages // pages),
            in_specs=[pl.BlockSpec((1, rows, LANES), lambda i, c, pt: (i, 0, 0)),
                      pl.BlockSpec((1, 1, _DIFF_COLS * LANES), lambda i, c, pt: (i, 0, 0))] + page_specs,
            out_specs=pl.BlockSpec((1, rows, LANES), lambda i, c, pt: (i, 0, 0)),
            scratch_shapes=[pltpu.VMEM((rows, 1), F32), pltpu.VMEM((rows, 1), F32), pltpu.VMEM((rows, LANES), F32)]),
        out_shape=jax.ShapeDtypeStruct((b, rows, LANES), F32),
        compiler_params=_cparams(("parallel", "arbitrary")), name="decode_diff",
    )(page_table, q, new_rows, *([view] * pages))


def _nsa_decode_kernel(idx_ref, val_ref, pt_ref, q_ref, news_ref, neww_ref, win_ref, *refs, n_sel, nb_cached, pos,
                       win_pos0):
    blk_refs = refs[:n_sel]
    os_ref, ow_ref = refs[n_sel:]
    b = pl.program_id(0)
    q = q_ref[0]
    qf = q.astype(F32)

    def new_key_score(new_ref):
        row = new_ref[0].astype(BF16).astype(F32)
        kd = jnp.where(_lane(row.shape) < NSA_D, row, pltpu.roll(row, 64, axis=1))
        vnew = jnp.where(_lane(row.shape) < NSA_D, pltpu.roll(row, 64, axis=1), 0.0)
        return jnp.sum(qf * kd, axis=-1, keepdims=True), vnew

    def attend(ktv, ok, s_new, new_ok, vnew):
        ktv = ktv.astype(BF16)
        s = _dot(q, jnp.concatenate([ktv[0:NSA_D], ktv[0:NSA_D]], axis=0))
        s = jnp.where(ok, s, -jnp.inf)
        if new_ok is not True:
            s_new = jnp.where(new_ok, s_new, -jnp.inf)
        m = jnp.maximum(jnp.max(s, axis=-1, keepdims=True), s_new)
        m = jnp.where(m == -jnp.inf, 0.0, m)
        e = jnp.exp(s - m)
        e_new = jnp.exp(s_new - m)
        den = jnp.sum(e, axis=-1, keepdims=True) + e_new
        den = jnp.where(den > 0.0, den, 1.0)
        o = _dot_nt(e.astype(BF16), ktv)
        return (pltpu.roll(o, 64, axis=1) + e_new * vnew) / den

    ktv = jnp.concatenate([r[0] for r in blk_refs], axis=1)
    lane = _lane((NSA_HEADS, n_sel * PAGE))
    page_of, half_of = lane >> 7, (lane >> 6) & 1
    ok = jnp.zeros(lane.shape, F32)
    new_ok = jnp.int32(0)
    for n in range(n_sel):
        valid = val_ref[b * n_sel + n] > 0
        blk = idx_ref[b * n_sel + n]
        use = jnp.where(valid & (blk < nb_cached), 1.0, 0.0)
        ok = jnp.where(page_of == n, jnp.where(half_of == (blk & 1), use, 0.0), ok)
        new_ok = new_ok | jnp.where(valid & (blk == nb_cached), 1, 0)
    s_new, vnew = new_key_score(news_ref)
    os_ref[0] = attend(ktv, ok > 0.0, s_new, new_ok > 0, vnew)

    wpos = win_pos0 + _lane((NSA_HEADS, win_ref.shape[-1]))
    d = pos - wpos
    okw = (d >= 0) & (d < NSA_WINDOW) & (wpos >= 0)
    sw_new, vwnew = new_key_score(neww_ref)
    ow_ref[0] = attend(win_ref[0], okw, sw_new, True, vwnew)


def _nsa_decode(idx, val, page_table, q_rot, new_slc, new_win, state_win, cache_slc, pos):
    b = q_rot.shape[0]
    n_pages = page_table.shape[1]
    nb_cached = n_pages * PAGE // NSA_BLOCK
    n_sel = min(NSA_TOP_N, nb_cached + 1)
    wb = state_win.shape[1]
    blocks = _feature_major(cache_slc)
    state_win = _feature_major(state_win)

    def blk_index(i, idx_r, val_r, pt_r, n):
        blk = jnp.minimum(idx_r[i * n_sel + n], nb_cached - 1)
        return (pt_r[i, blk >> 1], 0, 0)

    blk_specs = [pl.BlockSpec((1, 2 * NSA_D, PAGE), functools.partial(blk_index, n=n)) for n in range(n_sel)]
    per_seq = lambda shp: pl.BlockSpec((1,) + shp, lambda i, a, c, d: (i, 0, 0))
    return pl.pallas_call(
        functools.partial(_nsa_decode_kernel, n_sel=n_sel, nb_cached=nb_cached, pos=pos, win_pos0=pos - wb),
        grid_spec=pltpu.PrefetchScalarGridSpec(
            num_scalar_prefetch=3, grid=(b,),
            in_specs=[per_seq((NSA_HEADS, LANES)), per_seq((1, LANES)), per_seq((1, LANES)), per_seq((2 * NSA_D, wb))]
            + blk_specs,
            out_specs=[per_seq((NSA_HEADS, LANES))] * 2),
        out_shape=[jax.ShapeDtypeStruct((b, NSA_HEADS, LANES), F32)] * 2,
        compiler_params=_cparams(("parallel",)), name="nsa_decode",
    )(idx[:, :n_sel].reshape(-1), val[:, :n_sel].reshape(-1), page_table, q_rot, new_slc, new_win, state_win, *([blocks] * n_sel))


def _mix0_kernel(x_ref, om_ref, oc_ref, os_ref, ow_ref, g_ref, w_ref, *refs, absorbed):
    if absorbed:
        uv_ref, o_ref = refs
    else:
        (o_ref,) = refs
    gates = g_ref[...]
    acc = x_ref[...]
    for h in range(MLA_HEADS):
        om = om_ref[h]
        if absorbed:
            om = _dot(om.astype(BF16), uv_ref[h])
        acc = acc + _dot(om.astype(BF16), w_ref[h])
    for h in range(NSA_HEADS):
        gc, gs, gw = (gates[:, k * NSA_HEADS + h:k * NSA_HEADS + h + 1] for k in range(3))
        on = gc * oc_ref[h] + gs * os_ref[h] + gw * ow_ref[h]
        acc = acc + _dot(on.astype(BF16), w_ref[MLA_HEADS + h])
    o_ref[...] = acc


def _mix0(x, o_mla, o_c, o_s, o_w, gates, w_out, uv=None):
    t, d = x.shape
    tm = min(t, 512)
    row = lambda w: pl.BlockSpec((tm, w), lambda i: (i, 0))
    head = pl.BlockSpec((NSA_HEADS, tm, LANES), lambda i: (0, i, 0))
    full = lambda a: pl.BlockSpec(a.shape, lambda i: (0,) * a.ndim)
    ins = [x, o_mla, o_c, o_s, o_w, gates, w_out] + ([uv] if uv is not None else [])
    specs = [row(d), head, head, head, head, row(LANES), full(w_out)] + ([full(uv)] if uv is not None else [])
    return pl.pallas_call(
        functools.partial(_mix0_kernel, absorbed=uv is not None),
        grid=(t // tm,), in_specs=specs, out_specs=row(d), out_shape=jax.ShapeDtypeStruct((t, d), F32),
        compiler_params=_cparams(("parallel",)), name="mix0",
    )(*ins)


def _ffn_kernel(x_ref, g_ref, wg_ref, wu_ref, wd_ref, fg_ref, o_ref, hn_scr, acc_scr, *, final_norm):
    f = pl.program_id(1)

    @pl.when(f == 0)
    def _():
        hn_scr[...] = _rms(x_ref[...], g_ref[...]).astype(BF16)
        acc_scr[...] = x_ref[...]

    hn = hn_scr[...]
    act = jax.nn.silu(_dot(hn, wg_ref[...])) * _dot(hn, wu_ref[...])
    acc_scr[...] += _dot(act.astype(BF16), wd_ref[...])

    @pl.when(f == pl.num_programs(1) - 1)
    def _():
        y = acc_scr[...]
        o_ref[...] = _rms(y, fg_ref[...]) if final_norm else y


def _ffn(x, gain, wg, wu, wd, final_gain, final_norm):
    t, d = x.shape
    dff = wg.shape[1]
    tm = min(t, 512)
    tf = dff // 2 if (dff // 2) % LANES == 0 else dff
    return pl.pallas_call(
        functools.partial(_ffn_kernel, final_norm=final_norm),
        grid=(t // tm, dff // tf),
        in_specs=[pl.BlockSpec((tm, d), lambda i, f: (i, 0)), pl.BlockSpec((1, d), lambda i, f: (0, 0)),
                  pl.BlockSpec((d, tf), lambda i, f: (0, f)), pl.BlockSpec((d, tf), lambda i, f: (0, f)),
                  pl.BlockSpec((tf, d), lambda i, f: (f, 0)), pl.BlockSpec((1, d), lambda i, f: (0, 0))],
        out_specs=pl.BlockSpec((tm, d), lambda i, f: (i, 0)),
        out_shape=jax.ShapeDtypeStruct((t, d), F32),
        scratch_shapes=[pltpu.VMEM((tm, d), BF16), pltpu.VMEM((tm, d), F32)],
        compiler_params=_cparams(("parallel", "arbitrary")), name="ffn",
    )(x, gain, wg, wu, wd, final_gain)


_N_QSLOT = DIFF_HEADS


def _proj1_kernel(x_ref, g_ref, w_ref, cn_ref, sn_ref, rows_ref, q_ref, k_ref, v_ref, *, tm, prompt):
    hn = _rms(x_ref[...], g_ref[...]).astype(BF16)
    z = _dot(hn, w_ref[...])
    cn, sn = cn_ref[...], sn_ref[...]
    low = _lane((tm, LANES)) < DIFF_D
    q_scale = DIFF_SCALE * (LOG2E if prompt else 1.0)
    sum_rows = jnp.where(lax.broadcasted_iota(jnp.int32, (SUM_ROW_PAD, tm), 0) == 0, 1.0, 0.0).astype(BF16)

    def slot(s):
        return z[:, s * LANES:(s + 1) * LANES]

    for g in range(DIFF_KVH):
        for r in range(DIFF_GROUP):
            h = g * DIFF_GROUP + r
            qp = (slot(h) * cn + slot(_N_QSLOT + h) * sn) * q_scale
            q_ref[2 * g, r] = jnp.where(low, qp, 0.0).astype(BF16)
            q_ref[2 * g + 1, r] = jnp.where(low, 0.0, qp).astype(BF16)
        k = slot(2 * _N_QSLOT + g) * cn + slot(2 * _N_QSLOT + DIFF_KVH + g) * sn
        v = slot(2 * _N_QSLOT + 2 * DIFF_KVH + g)
        rows_ref[:, g * LANES:(g + 1) * LANES] = k
        rows_ref[:, (DIFF_KVH + g) * LANES:(DIFF_KVH + g + 1) * LANES] = v
        k_ref[g] = k.astype(BF16)
        v_ref[g, 0:LANES, :] = v.T.astype(BF16)
        v_ref[g, LANES:LANES + SUM_ROW_PAD, :] = sum_rows


def _prep_layer1(w_in_c):
    nq = DIFF_HEADS * 2 * DIFF_D
    nk = DIFF_KVH * 2 * DIFF_D
    wq, wk, wv = w_in_c[:, :nq], w_in_c[:, nq:nq + nk], w_in_c[:, nq + nk:]
    return jnp.concatenate([wq, _rot_cols(wq, DIFF_D), wk, _rot_cols(wk, DIFF_D), wv], axis=1).astype(BF16)


def _proj1(x, gain, w1, cn, sn, prompt):
    t, d = x.shape
    tm = min(t, 256)
    row = lambda w: pl.BlockSpec((tm, w), lambda i: (i, 0))
    full = lambda a: pl.BlockSpec(a.shape, lambda i: (0,) * a.ndim)
    ng = 2 * DIFF_KVH
    return pl.pallas_call(
        functools.partial(_proj1_kernel, tm=tm, prompt=prompt),
        grid=(t // tm,),
        in_specs=[row(d), full(gain), full(w1), row(LANES), row(LANES)],
        out_specs=[row(2 * DIFF_KVH * LANES),
                   pl.BlockSpec((ng, DIFF_GROUP, tm, LANES), lambda i: (0, 0, i, 0)),
                   pl.BlockSpec((DIFF_KVH, tm, LANES), lambda i: (0, i, 0)),
                   pl.BlockSpec((DIFF_KVH, LANES + SUM_ROW_PAD, tm), lambda i: (0, 0, i))],
        out_shape=[jax.ShapeDtypeStruct((t, 2 * DIFF_KVH * LANES), F32),
                   jax.ShapeDtypeStruct((ng, DIFF_GROUP, t, LANES), BF16),
                   jax.ShapeDtypeStruct((DIFF_KVH, t, LANES), BF16),
                   jax.ShapeDtypeStruct((DIFF_KVH, LANES + SUM_ROW_PAD, t), BF16)],
        compiler_params=_cparams(("parallel",)), name="proj1",
    )(x, gain, w1, cn, sn)


def _mix1_kernel(x_ref, o_ref_in, lam_ref, sub_ref, w_ref, o_ref):
    lv = lam_ref[...]
    lam = (jnp.exp(jnp.sum(lv[0:1] * lv[1:2], axis=-1, keepdims=True))
           - jnp.exp(jnp.sum(lv[2:3] * lv[3:4], axis=-1, keepdims=True)) + DIFF_LAMBDA_INIT)
    acc = x_ref[...]
    for g in range(DIFF_KVH):
        for r in range(DIFF_GROUP):
            o = o_ref_in[2 * g, r] - lam * o_ref_in[2 * g + 1, r]
            o = _rms(o, sub_ref[...]) * (1.0 - DIFF_LAMBDA_INIT)
            acc = acc + _dot(o.astype(BF16), w_ref[g * DIFF_GROUP + r])
    o_ref[...] = acc


def _mix1(x, o, lam_vecs, subln, w_out):
    t, d = x.shape
    tm = min(t, 512)
    full = lambda a: pl.BlockSpec(a.shape, lambda i: (0,) * a.ndim)
    return pl.pallas_call(
        _mix1_kernel, grid=(t // tm,),
        in_specs=[pl.BlockSpec((tm, d), lambda i: (i, 0)),
                  pl.BlockSpec((2 * DIFF_KVH, DIFF_GROUP, tm, LANES), lambda i: (0, 0, i, 0)),
                  full(lam_vecs), full(subln), full(w_out)],
        out_specs=pl.BlockSpec((tm, d), lambda i: (i, 0)), out_shape=jax.ShapeDtypeStruct((t, d), F32),
        compiler_params=_cparams(("parallel",)), name="mix1",
    )(x, o, lam_vecs, subln, w_out)


def kernel(x_prompt, x_sample, cache_mla, cache_nsa_cmp, cache_nsa_slc, state_nsa_win, cache_diff, page_table,
           attn_norm, ffn_norm, final_norm,
           w_in_a, mla_q_norm, mla_kv_norm, w_mla_uq, w_mla_uk, w_mla_uv,
           nsa_pe_k, nsa_pe_v, nsa_cmp_w1_k, nsa_cmp_w2_k, nsa_cmp_w1_v, nsa_cmp_w2_v, w_out_a,
           w_in_c, diff_lambda_q1, diff_lambda_k1, diff_lambda_q2, diff_lambda_k2, diff_subln, w_out_c,
           w_ffn_gate, w_ffn_up, w_ffn_down):
    bp, t, d = x_prompt.shape
    b, ts, _ = x_sample.shape
    assert bp == 1 and ts == 1 and t % NSA_BLOCK == 0
    n_pool = cache_mla.shape[0]
    past = page_table.shape[1] * PAGE
    assert past % NSA_BLOCK == 0
    nh = MLA_HEADS

    w0, uq, ukv_prompt, ukt, uv_pad = _prep_layer0(w_in_a, w_mla_uq, w_mla_uk, w_mla_uv)
    pe, w1c, w2c, pe_t, w1c_t = _prep_compress(nsa_pe_k, nsa_pe_v, nsa_cmp_w1_k, nsa_cmp_w1_v, nsa_cmp_w2_k,
                                               nsa_cmp_w2_v)
    w1 = _prep_layer1(w_in_c)
    wo_a = w_out_a.reshape(2 * nh, NSA_D, d)
    wo_a = jnp.concatenate([wo_a, jnp.zeros_like(wo_a)], axis=1).astype(BF16)
    wo_c = w_out_c.reshape(DIFF_HEADS, 2 * DIFF_D, d).astype(BF16)
    wg, wu, wd = (w.astype(BF16) for w in (w_ffn_gate, w_ffn_up, w_ffn_down))
    lam_vecs = jnp.stack([diff_lambda_q1, diff_lambda_k1, diff_lambda_q2, diff_lambda_k2])
    row2 = lambda v: v.reshape(1, -1)

    xp = x_prompt[0]
    xs = x_sample[:, 0]
    tab_p = _rope_tables(jnp.arange(t, dtype=jnp.int32))
    tab_s = _rope_tables(jnp.full((b,), past, jnp.int32))

    (mla_p, cmp_p, slc_p, win_p, qm_p, qc_p, qr_p, gate_p, km_p, vm_p, ksa_p, vs_p, kw_p, vw_p) = _proj0(
        xp, row2(attn_norm[0]), w0, row2(mla_q_norm), row2(mla_kv_norm), uq, ukv_prompt, tab_p, True)
    o_mla_p = _flash(qm_p.reshape(nh, 1, t, LANES), km_p, vm_p, tq=2048, tk=1024, groups=nh,
                     q_index=lambda g, k0: g, kv_index=lambda g: g, sum_row=MLA_V, out_dtype=BF16,
                     name="flash_mla")[:, 0]
    nb_p = t // NSA_BLOCK
    nbp_p = -(-nb_p // LANES) * LANES
    kcvc_p = _compress_prompt(cmp_p, pe, w1c, w2c)
    o_c_p, imp_p = _cmp_attn_prompt(kcvc_p, qc_p, nbp_p)
    q_aug = _select_prompt(imp_p, qr_p)
    keys_per_half = LANES * NSA_BLOCK
    o_s_p = _flash(q_aug, ksa_p[None], vs_p[None], tq=256, tk=1024, groups=1,
                   q_index=lambda g, k0: k0 // keys_per_half, kv_index=lambda g: 0, sum_row=NSA_D,
                   name="flash_slc")[0]
    o_w_p = _flash(qr_p[None], kw_p[None], vw_p[None], tq=256, tk=256, groups=1, window=NSA_WINDOW,
                   q_index=lambda g, k0: 0, kv_index=lambda g: 0, sum_row=NSA_D, name="flash_win")[0]
    hp = _mix0(xp, o_mla_p, o_c_p, o_s_p, o_w_p, gate_p, wo_a)
    hp = _ffn(hp, row2(ffn_norm[0]), wg[0], wu[0], wd[0], row2(final_norm), False)

    (mla_s, cmp_s, slc_s, win_s, qm_s, qc_s, qr_s, gate_s, qa_s) = _proj0(
        xs, row2(attn_norm[0]), w0, row2(mla_q_norm), row2(mla_kv_norm), uq, ukt, tab_s, False)
    q_abs = jnp.transpose(qa_s[:, :, :MLA_KV_LORA + MLA_ROPE], (1, 0, 2))
    o_lat = _decode_mla(q_abs, mla_s[:, None, :], cache_mla, page_table, 64)
    n_pages = page_table.shape[1]
    nb_sel_s = -(-(past + ts) // NSA_BLOCK)
    nbp_s = -(-nb_sel_s // LANES) * LANES
    kcvc_s = _compress_paged(cache_nsa_cmp, page_table, pe_t, w1c_t, w2c, 64)
    o_c_s, imp_s = _cmp_attn_sample(kcvc_s, jnp.transpose(qc_s, (1, 0, 2)), past, nbp_s, n_pages)
    idx_s, val_s = _select_sample(imp_s[:, 0], past, nb_sel_s, n_pages)
    o_s_s, o_w_s = _nsa_decode(idx_s, val_s, page_table, jnp.transpose(qr_s, (1, 0, 2)), slc_s[:, None, :],
                               win_s[:, None, :], state_nsa_win, cache_nsa_slc, past)
    tr = lambda a: jnp.transpose(a, (1, 0, 2))
    hs = _mix0(xs, tr(o_lat), tr(o_c_s), tr(o_s_s), tr(o_w_s), gate_s, wo_a, uv_pad)
    hs = _ffn(hs, row2(ffn_norm[0]), wg[0], wu[0], wd[0], row2(final_norm), False)

    rows_p, qd_p, kd_p, vd_p = _proj1(hp, row2(attn_norm[1]), w1, tab_p[2], tab_p[3], True)
    o_d_p = _flash(qd_p, kd_p, vd_p, tq=512, tk=1024, groups=2 * DIFF_KVH,
                   q_index=lambda g, k0: g, kv_index=lambda g: g // 2, sum_row=2 * DIFF_D, name="flash_diff")
    hp = _mix1(hp, o_d_p, lam_vecs, row2(diff_subln), wo_c)
    y_p = _ffn(hp, row2(ffn_norm[1]), wg[1], wu[1], wd[1], row2(final_norm), True)

    rows_s, qd_s, kd_s, vd_s = _proj1(hs, row2(attn_norm[1]), w1, tab_s[2], tab_s[3], False)
    qd = jnp.transpose(qd_s.reshape(_DIFF_ROWS, b, LANES), (1, 0, 2))
    o_d_s = _decode_diff(qd, rows_s[:, None, :], cache_diff, page_table, 32)
    o_d_s = jnp.transpose(o_d_s, (1, 0, 2)).reshape(2 * DIFF_KVH, DIFF_GROUP, b, LANES)
    hs = _mix1(hs, o_d_s, lam_vecs, row2(diff_subln), wo_c)
    y_s = _ffn(hs, row2(ffn_norm[1]), wg[1], wu[1], wd[1], row2(final_norm), True)

    wb_p = min(NSA_WINDOW, t)
    pair = lambda a: a.reshape(a.shape[0], 2, 1, NSA_D)
    new_win_s = jnp.concatenate([state_nsa_win, win_s.reshape(b, 1, 2, 1, NSA_D)], axis=1)[:, ts:]
    return (y_p[None], y_s[:, None], mla_p[None], pair(cmp_p)[None], pair(slc_p)[None], pair(win_p[t - wb_p:])[None],
            rows_p.reshape(1, t, 2, DIFF_KVH, 2 * DIFF_D),
            mla_s[:, None], pair(cmp_s)[:, None], pair(slc_s)[:, None], new_win_s,
            rows_s.reshape(b, 1, 2, DIFF_KVH, 2 * DIFF_D))
```

```python
import functools
import math

import numpy as np
import jax
import jax.numpy as jnp
from jax import lax
from jax.experimental import pallas as pl
from jax.experimental.pallas import tpu as pltpu

F32 = jnp.float32
BF16 = jnp.bfloat16

LANES = 128
VMEM_LIMIT = 56 * 1024 * 1024

ROPE_THETA = 10000.0
NORM_EPS = 1e-6
PAGE = 128
MLA_HEADS = 8
MLA_Q_LORA = 256
MLA_KV_LORA = 128
MLA_NOPE = 64
MLA_ROPE = 32
MLA_V = 64
MLA_SCALE = (MLA_NOPE + MLA_ROPE) ** -0.5
NSA_HEADS = 8
NSA_D = 64
NSA_BLOCK = 64
NSA_TOP_N = 16
NSA_WINDOW = 512
NSA_SCALE = NSA_D ** -0.5
NSA_FORCED = 1.0e4
DIFF_HEADS = 8
DIFF_KVH = 2
DIFF_D = 64
DIFF_GROUP = DIFF_HEADS // DIFF_KVH
DIFF_SCALE = DIFF_D ** -0.5
DIFF_LAMBDA_INIT = 0.8 - 0.6 * math.exp(-0.3 * 1)

LOG2E = math.log2(math.e)
SUM_ROW_PAD = 16
NEG_BIG = -1.0e30
SEL_NEG = -float(2 ** 40)


def _cparams(sem):
    return pltpu.CompilerParams(dimension_semantics=sem, vmem_limit_bytes=VMEM_LIMIT)


def _rms(x, g):
    return x * lax.rsqrt(jnp.mean(x * x, axis=-1, keepdims=True) + NORM_EPS) * g


def _dot(a, b):
    return jnp.dot(a, b, preferred_element_type=F32)


def _dot_nt(a, b):
    return lax.dot_general(a, b, (((1,), (1,)), ((), ())), preferred_element_type=F32)


def _lane(shape):
    return lax.broadcasted_iota(jnp.int32, shape, len(shape) - 1)


_S_CQ, _S_CKV, _S_KR, _S_KR_ROT, _S_QN, _S_QN_ROT = 0, 2, 3, 4, 5, 9
_S_CMP, _S_SLC, _S_SLC_ROT, _S_WIN, _S_WIN_ROT, _S_GATE, _N_SLOTS0 = 13, 14, 15, 16, 17, 18, 19
_VT_ROWS = NSA_D + SUM_ROW_PAD


def _proj0_kernel(x_ref, g_ref, w_ref, qn_ref, kvn_ref, uq_ref, ukv_ref, c32_ref, s32_ref, cn_ref, sn_ref,
                  *out_refs, prompt, tm):
    if prompt:
        (mla_ref, cmp_ref, slc_ref, win_ref, qm_ref, qc_ref, qr_ref, gate_ref,
         km_ref, vm_ref, ksa_ref, vs_ref, kw_ref, vw_ref) = out_refs
    else:
        (mla_ref, cmp_ref, slc_ref, win_ref, qm_ref, qc_ref, qr_ref, gate_ref, qa_ref) = out_refs

    hn = _rms(x_ref[...], g_ref[...]).astype(BF16)
    z = _dot(hn, w_ref[...])

    def slot(s, n=1):
        return z[:, s * LANES:(s + n) * LANES]

    c32, s32, cn, sn = c32_ref[...], s32_ref[...], cn_ref[...], sn_ref[...]
    lane = _lane((tm, LANES))
    low = lane < NSA_D

    cqn = _rms(slot(_S_CQ, 2), qn_ref[...]).astype(BF16)
    zq = _dot(cqn, uq_ref[...])
    ckvn = _rms(slot(_S_CKV), kvn_ref[...])
    kpe_hi = slot(_S_KR) * c32 + slot(_S_KR_ROT) * s32
    mla_ref[:, 0:MLA_KV_LORA] = ckvn
    mla_ref[:, MLA_KV_LORA:MLA_KV_LORA + MLA_ROPE] = pltpu.roll(kpe_hi, 64, axis=1)[:, 0:MLA_ROPE]
    ckvb = ckvn.astype(BF16)
    q_cos = jnp.where(low, 1.0, c32)
    nh = MLA_HEADS
    flash_scale = LOG2E if prompt else 1.0
    sum_row = jnp.where(lane == NSA_D, 1.0, 0.0)
    for h in range(nh):
        qh = (zq[:, h * LANES:(h + 1) * LANES] * q_cos
              + zq[:, (nh + h) * LANES:(nh + h + 1) * LANES] * s32) * (MLA_SCALE * flash_scale)
        qm_ref[h] = qh.astype(BF16)
        if not prompt:
            qlat = _dot(qh.astype(BF16), ukv_ref[h])
            qa_ref[h, :, 0:LANES] = qlat.astype(BF16)
            qa_ref[h, :, LANES:2 * LANES] = jnp.where(lane < MLA_ROPE, pltpu.roll(qh, 64, axis=1), 0.0).astype(BF16)
    if prompt:
        zk = _dot(ckvb, ukv_ref[0])
        zv = _dot(ckvb, ukv_ref[1])
        for h in range(nh):
            km_ref[h] = (zk[:, h * LANES:(h + 1) * LANES] + kpe_hi).astype(BF16)
            vm_ref[h] = (zv[:, h * LANES:(h + 1) * LANES] + sum_row).T[0:_VT_ROWS].astype(BF16)

    for p in range(NSA_HEADS // 2):
        plain = slot(_S_QN + p) * NSA_SCALE
        roped = (slot(_S_QN + p) * cn + slot(_S_QN_ROT + p) * sn) * (NSA_SCALE * flash_scale)
        qc_ref[2 * p] = jnp.where(low, plain, 0.0).astype(BF16)
        qc_ref[2 * p + 1] = jnp.where(low, 0.0, plain).astype(BF16)
        qr_ref[2 * p] = jnp.where(low, roped, 0.0).astype(BF16)
        qr_ref[2 * p + 1] = jnp.where(low, 0.0, roped).astype(BF16)

    ck = jnp.where(low, cn, 1.0)
    cmp_ref[...] = slot(_S_CMP)
    slc = slot(_S_SLC) * ck + slot(_S_SLC_ROT) * sn
    win = slot(_S_WIN) * ck + slot(_S_WIN_ROT) * sn
    slc_ref[...] = slc
    win_ref[...] = win
    gate_ref[...] = jax.nn.sigmoid(slot(_S_GATE))
    if prompt:
        slc_sw = pltpu.roll(slc, 64, axis=1)
        win_sw = pltpu.roll(win, 64, axis=1)
        row = pl.program_id(0) * tm + lax.broadcasted_iota(jnp.int32, (tm, LANES), 0)
        onehot = ((row >> 6) & (LANES - 1)) == lane
        ksa_ref[:, 0:LANES] = jnp.where(onehot, 1.0, 0.0).astype(BF16)
        ksa_ref[:, LANES:2 * LANES] = jnp.where(low, slc, slc_sw).astype(BF16)
        is_sum = lane == NSA_D
        keep = lane <= NSA_D
        vs_ref[...] = jnp.where(is_sum, 1.0, jnp.where(keep, slc_sw, 0.0)).T[0:_VT_ROWS].astype(BF16)
        kw_ref[...] = jnp.where(low, win, win_sw).astype(BF16)
        vw_ref[...] = jnp.where(is_sum, 1.0, jnp.where(keep, win_sw, 0.0)).T[0:_VT_ROWS].astype(BF16)


def _rot_cols(w, d):
    k = w.shape[0]
    w3 = w.reshape(k, -1, d)
    return jnp.concatenate([-w3[..., d // 2:], w3[..., :d // 2]], axis=-1).reshape(k, -1)


def _place(w, left, width=LANES):
    return jnp.pad(w, ((0, 0), (left, width - left - w.shape[1])))


def _prep_layer0(w_in_a, w_mla_uq, w_mla_uk, w_mla_uv):
    o = np.cumsum([0, 256, 128, 32, 512, 64, 64, 64, 64, 64, 64, 24]).tolist()
    c_q, c_kv, k_r, q_n = (w_in_a[:, o[i]:o[i + 1]] for i in range(4))
    kv_c = w_in_a[:, o[4]:o[6]]
    kv_s = w_in_a[:, o[6]:o[8]]
    kv_w = w_in_a[:, o[8]:o[10]]
    g = w_in_a[:, o[10]:o[11]]
    z64 = jnp.zeros((w_in_a.shape[0], 64), F32)
    cols = [c_q, c_kv, _place(k_r, 64), _place(_rot_cols(k_r, MLA_ROPE), 64), q_n, _rot_cols(q_n, NSA_D),
            kv_c,
            kv_s, jnp.concatenate([_rot_cols(kv_s[:, :64], 64), z64], axis=1),
            kv_w, jnp.concatenate([_rot_cols(kv_w[:, :64], 64), z64], axis=1),
            _place(g, 0)]
    w0 = jnp.concatenate(cols, axis=1).astype(BF16)
    uq3 = w_mla_uq.reshape(MLA_Q_LORA, MLA_HEADS, MLA_NOPE + MLA_ROPE)
    nope, pe = uq3[..., :MLA_NOPE], uq3[..., MLA_NOPE:]
    pe_rot = jnp.concatenate([-pe[..., MLA_ROPE // 2:], pe[..., :MLA_ROPE // 2]], axis=-1)
    pad32 = jnp.zeros(pe.shape, F32)
    plain = jnp.concatenate([nope, pe, pad32], axis=-1).reshape(MLA_Q_LORA, -1)
    rot = jnp.concatenate([jnp.zeros(nope.shape, F32), pe_rot, pad32], axis=-1).reshape(MLA_Q_LORA, -1)
    uq = jnp.concatenate([plain, rot], axis=1).astype(BF16)
    pad_k = jnp.zeros((MLA_KV_LORA, MLA_HEADS, LANES - MLA_NOPE), F32)
    uk_slots = jnp.concatenate([w_mla_uk, pad_k], axis=-1).reshape(MLA_KV_LORA, -1)
    uv_slots = jnp.concatenate([w_mla_uv, pad_k], axis=-1).reshape(MLA_KV_LORA, -1)
    ukv_prompt = jnp.stack([uk_slots, uv_slots]).astype(BF16)
    ukt = jnp.transpose(w_mla_uk, (1, 2, 0))
    ukt = jnp.concatenate([ukt, jnp.zeros_like(ukt)], axis=1).astype(BF16)
    uv_pad = jnp.transpose(jnp.concatenate([w_mla_uv, pad_k], axis=-1), (1, 0, 2)).astype(BF16)
    return w0, uq, ukv_prompt, ukt, uv_pad


def _rope_tables(pos):
    def cs(dim):
        inv = ROPE_THETA ** (-jnp.arange(0, dim, 2, dtype=F32) / dim)
        ang = pos.astype(F32)[:, None] * inv[None, :]
        ang = jnp.concatenate([ang, ang], axis=-1)
        return jnp.cos(ang), jnp.sin(ang)
    c32, s32 = cs(MLA_ROPE)
    cn, sn = cs(NSA_D)
    return (_place(c32, 64), _place(s32, 64), jnp.concatenate([cn, cn], axis=1), jnp.concatenate([sn, sn], axis=1))


def _proj0(x, gain, w0, qnorm, kvnorm, uq, ukv, tables, prompt):
    t, d = x.shape
    tm = min(t, 256)
    nh = MLA_HEADS
    row = lambda w: pl.BlockSpec((tm, w), lambda i: (i, 0))
    head = lambda w: pl.BlockSpec((nh, tm, w), lambda i: (0, i, 0))
    full = lambda a: pl.BlockSpec(a.shape, lambda i: (0,) * a.ndim)
    out_shape = [jax.ShapeDtypeStruct((t, MLA_KV_LORA + MLA_ROPE), F32)] + [jax.ShapeDtypeStruct((t, LANES), F32)] * 3
    out_specs = [row(MLA_KV_LORA + MLA_ROPE), row(LANES), row(LANES), row(LANES)]
    out_shape += [jax.ShapeDtypeStruct((nh, t, LANES), BF16)] * 3 + [jax.ShapeDtypeStruct((t, LANES), F32)]
    out_specs += [head(LANES)] * 3 + [row(LANES)]
    if prompt:
        col = pl.BlockSpec((_VT_ROWS, tm), lambda i: (0, i))
        out_shape += [jax.ShapeDtypeStruct((nh, t, LANES), BF16), jax.ShapeDtypeStruct((nh, _VT_ROWS, t), BF16)]
        out_specs += [head(LANES), pl.BlockSpec((nh, _VT_ROWS, tm), lambda i: (0, 0, i))]
        out_shape += [jax.ShapeDtypeStruct((t, 2 * LANES), BF16), jax.ShapeDtypeStruct((_VT_ROWS, t), BF16),
                      jax.ShapeDtypeStruct((t, LANES), BF16), jax.ShapeDtypeStruct((_VT_ROWS, t), BF16)]
        out_specs += [row(2 * LANES), col, row(LANES), col]
    else:
        out_shape += [jax.ShapeDtypeStruct((nh, t, 2 * LANES), BF16)]
        out_specs += [head(2 * LANES)]
    return pl.pallas_call(
        functools.partial(_proj0_kernel, prompt=prompt, tm=tm),
        grid=(t // tm,),
        in_specs=[row(d), full(gain), full(w0), full(qnorm), full(kvnorm), full(uq), full(ukv)] + [row(LANES)] * 4,
        out_specs=out_specs, out_shape=out_shape,
        compiler_params=_cparams(("parallel",)),
        name="proj0_prompt" if prompt else "proj0_sample",
    )(x, gain, w0, qnorm, kvnorm, uq, ukv, *tables)


_CMP_PAIR = 2
_STAGE_PITCH = PAGE + 8


def _compress_rows(src_ref, n_blocks, pe_ref, w1_ref):
    acc = None
    for r0 in range(0, NSA_BLOCK, _CMP_PAIR):
        xs = [src_ref[pl.ds(r0 + u, n_blocks, stride=NSA_BLOCK), :] + pe_ref[r0 + u:r0 + u + 1, :]
              for u in range(_CMP_PAIR)]
        part = _dot(jnp.concatenate(xs, axis=1).astype(BF16), w1_ref[r0 // _CMP_PAIR])
        acc = part if acc is None else acc + part
    return acc


def _compress_prompt_kernel(f_ref, pe_ref, w1_ref, w2_ref, o_ref, *, n_blocks):
    pre = _compress_rows(f_ref, n_blocks, pe_ref, w1_ref)
    o_ref[...] = _dot(jax.nn.silu(pre).astype(BF16), w2_ref[...])


def _compress_paged_kernel(pt_ref, *refs, pages):
    page_refs = refs[:pages]
    pe_ref, w1_ref, w2_ref, o_ref, stage, pre_k, pre_v = refs[pages:]
    ch = pl.program_id(1)
    for p in range(pages):
        stage[_STAGE_PITCH * p:_STAGE_PITCH * p + PAGE, :] = page_refs[p][0]
    for kv, pre in enumerate((pre_k, pre_v)):
        acc = None
        for f0 in range(0, NSA_D, _CMP_PAIR):
            xs = [stage[pl.ds(kv * NSA_D + f0 + u, pages, stride=_STAGE_PITCH), :]
                  + pe_ref[kv * NSA_D + f0 + u:kv * NSA_D + f0 + u + 1, :] for u in range(_CMP_PAIR)]
            part = _dot(jnp.concatenate(xs, axis=1).astype(BF16), w1_ref[kv, f0 // _CMP_PAIR])
            acc = part if acc is None else acc + part
        pre[pl.ds(pl.multiple_of(ch * pages, pages), pages), :] = acc

    @pl.when(ch == pl.num_programs(1) - 1)
    def _():
        pk, pv = pre_k[...], pre_v[...]
        low = _lane(pk.shape) < NSA_D
        even = jnp.where(low, pk, pltpu.roll(pv, 64, axis=1))
        odd = jnp.where(low, pltpu.roll(pk, 64, axis=1), pv)
        both = jnp.concatenate([even, odd], axis=0)
        o_ref[0] = _dot(jax.nn.silu(both).astype(BF16), w2_ref[...])


def _prep_compress(pe_k, pe_v, w1_k, w1_v, w2_k, w2_v):
    pe = jnp.concatenate([pe_k, pe_v], axis=1)
    w1k3 = w1_k.reshape(NSA_BLOCK, NSA_D, NSA_D)
    w1v3 = w1_v.reshape(NSA_BLOCK, NSA_D, NSA_D)
    w1 = jnp.zeros((NSA_BLOCK, 2, NSA_D, 2 * NSA_D), F32)
    w1 = w1.at[:, 0, :, :NSA_D].set(w1k3).at[:, 1, :, NSA_D:].set(w1v3)
    w1 = w1.reshape(NSA_BLOCK // _CMP_PAIR, _CMP_PAIR * LANES, 2 * NSA_D).astype(BF16)
    w2 = jnp.zeros((2 * NSA_D, 2 * NSA_D), F32)
    w2 = w2.at[:NSA_D, :NSA_D].set(w2_k).at[NSA_D:, NSA_D:].set(w2_v).astype(BF16)
    pe_t = jnp.tile(jnp.concatenate([pe_k.T, pe_v.T], axis=0), (1, 2))
    def per_dim(w3):
        wd = jnp.transpose(w3, (1, 0, 2))
        z = jnp.zeros_like(wd)
        top = jnp.concatenate([wd, z], axis=2)
        bot = jnp.concatenate([z, wd], axis=2)
        return jnp.concatenate([top, bot], axis=1)
    w1_t = jnp.stack([per_dim(w1k3), per_dim(w1v3)])
    w1_t = w1_t.reshape(2, NSA_D // _CMP_PAIR, _CMP_PAIR * LANES, LANES).astype(BF16)
    return pe, w1, w2, pe_t, w1_t


def _compress_prompt(cmp_rows, pe, w1, w2):
    t = cmp_rows.shape[0]
    nb = t // NSA_BLOCK
    full = lambda a: pl.BlockSpec(a.shape, lambda i: (0,) * a.ndim)
    return pl.pallas_call(
        functools.partial(_compress_prompt_kernel, n_blocks=nb),
        grid=(1,), in_specs=[full(cmp_rows), full(pe), full(w1), full(w2)],
        out_specs=pl.BlockSpec((nb, LANES), lambda i: (0, 0)),
        out_shape=jax.ShapeDtypeStruct((nb, LANES), F32),
        compiler_params=_cparams(("arbitrary",)), name="compress_prompt",
    )(cmp_rows, pe, w1, w2)


def _feature_major(cache):
    n, rows = cache.shape[:2]
    return jnp.transpose(cache, (0, 2, 3, 4, 1)).reshape(n, 2 * NSA_D, rows)


def _compress_paged(cache_cmp, page_table, pe_t, w1_t, w2, pages):
    b, n_pages = page_table.shape
    nb = n_pages * PAGE // NSA_BLOCK
    view = _feature_major(cache_cmp)
    pages = min(pages, n_pages)
    page_specs = [pl.BlockSpec((1, 2 * NSA_D, PAGE),
                               functools.partial(lambda i, c, pt, p: (pt[i, c * pages + p], 0, 0), p=p))
                  for p in range(pages)]
    full = lambda a: pl.BlockSpec(a.shape, lambda i, c, pt: (0,) * a.ndim)
    return pl.pallas_call(
        functools.partial(_compress_paged_kernel, pages=pages),
        grid_spec=pltpu.PrefetchScalarGridSpec(
            num_scalar_prefetch=1, grid=(b, n_pages // pages),
            in_specs=page_specs + [full(pe_t), full(w1_t), full(w2)],
            out_specs=pl.BlockSpec((1, nb, LANES), lambda i, c, pt: (i, 0, 0)),
            scratch_shapes=[pltpu.VMEM((_STAGE_PITCH * pages, LANES), F32), pltpu.VMEM((n_pages, LANES), F32),
                            pltpu.VMEM((n_pages, LANES), F32)]),
        out_shape=jax.ShapeDtypeStruct((b, nb, LANES), F32),
        compiler_params=_cparams(("parallel", "arbitrary")), name="compress_paged",
    )(page_table, *([view] * pages), pe_t, w1_t, w2)


def _block_id(lane, perm_pages):
    if not perm_pages:
        return lane
    assert perm_pages & (perm_pages - 1) == 0
    shift = perm_pages.bit_length() - 1
    return jnp.where(lane < 2 * perm_pages, 2 * (lane & (perm_pages - 1)) + (lane >> shift), lane)


def _cmp_attn_kernel(kv_ref, q_ref, o_ref, imp_ref, *, rows_per_query, tq, pos0, pos_step, nbp, perm_pages):
    kv = kv_ref[...] if len(kv_ref.shape) == 2 else kv_ref[0]
    nb = kv.shape[0]
    sw = pltpu.roll(kv, 64, axis=1)
    kdup = jnp.where(_lane(kv.shape) < NSA_D, kv, sw).astype(BF16)
    vk = sw.astype(BF16)
    if rows_per_query:
        q = q_ref[0]
        qpos = jnp.full((q.shape[0], 1), pos0, jnp.int32)
    else:
        q = q_ref[...].reshape(NSA_HEADS * tq, LANES)
        r = lax.broadcasted_iota(jnp.int32, (NSA_HEADS * tq, 1), 0)
        qpos = pos0 + pos_step * (pl.program_id(0) * tq + (r & (tq - 1)))
    s = _dot_nt(q, kdup)
    last = (_block_id(_lane(s.shape), perm_pages) + 1) * NSA_BLOCK - 1
    s = jnp.where(last <= qpos, s, -jnp.inf)
    m = jnp.max(s, axis=-1, keepdims=True)
    m = jnp.where(m == -jnp.inf, 0.0, m)
    e = jnp.exp(s - m)
    den = jnp.sum(e, axis=-1, keepdims=True)
    p = e / jnp.where(den > 0.0, den, 1.0)
    o = _dot(p.astype(BF16), vk)
    if rows_per_query:
        o_ref[0] = o
        imp = jnp.sum(p, axis=0, keepdims=True)
        if nbp > nb:
            imp = jnp.concatenate([imp, jnp.zeros((1, nbp - nb), F32)], axis=1)
        imp_ref[0] = imp
    else:
        o_ref[...] = o.reshape(NSA_HEADS, tq, LANES)
        imp = jnp.sum(p.reshape(NSA_HEADS, tq, nb), axis=0)
        if nbp > nb:
            imp = jnp.concatenate([imp, jnp.zeros((tq, nbp - nb), F32)], axis=1)
        imp_ref[...] = imp


def _cmp_attn_prompt(kcvc, q_cmp, nbp):
    nh, t, _ = q_cmp.shape
    nb = kcvc.shape[0]
    tq = min(t, 128)
    return pl.pallas_call(
        functools.partial(_cmp_attn_kernel, rows_per_query=False, tq=tq, pos0=0, pos_step=1, nbp=nbp, perm_pages=0),
        grid=(t // tq,),
        in_specs=[pl.BlockSpec((nb, LANES), lambda i: (0, 0)), pl.BlockSpec((nh, tq, LANES), lambda i: (0, i, 0))],
        out_specs=[pl.BlockSpec((nh, tq, LANES), lambda i: (0, i, 0)), pl.BlockSpec((tq, nbp), lambda i: (i, 0))],
        out_shape=[jax.ShapeDtypeStruct((nh, t, LANES), F32), jax.ShapeDtypeStruct((t, nbp), F32)],
        compiler_params=_cparams(("parallel",)), name="cmp_attn_prompt",
    )(kcvc, q_cmp)


def _cmp_attn_sample(kcvc, q_cmp, pos, nbp, perm_pages):
    b, nh, _ = q_cmp.shape
    nb = kcvc.shape[1]
    return pl.pallas_call(
        functools.partial(_cmp_attn_kernel, rows_per_query=True, tq=1, pos0=pos, pos_step=0, nbp=nbp,
                          perm_pages=perm_pages),
        grid=(b,),
        in_specs=[pl.BlockSpec((1, nb, LANES), lambda i: (i, 0, 0)), pl.BlockSpec((1, nh, LANES), lambda i: (i, 0, 0))],
        out_specs=[pl.BlockSpec((1, nh, LANES), lambda i: (i, 0, 0)), pl.BlockSpec((1, 1, nbp), lambda i: (i, 0, 0))],
        out_shape=[jax.ShapeDtypeStruct((b, nh, LANES), F32), jax.ShapeDtypeStruct((b, 1, nbp), F32)],
        compiler_params=_cparams(("parallel",)), name="cmp_attn_sample",
    )(kcvc, q_cmp)


def _select_kernel(imp_ref, *refs, tq, pos0, pos_step, nb_sel, n_halves, with_query, perm_pages):
    if with_query:
        q_ref, qa_ref = refs
    else:
        idx_ref, val_ref = refs
    imp = imp_ref[...]
    nbp = imp.shape[1]
    lane = _block_id(_lane(imp.shape), perm_pages)
    qpos = pos0 + pos_step * (pl.program_id(0) * tq + lax.broadcasted_iota(jnp.int32, (tq, 1), 0))
    cur = qpos >> 6
    forced = (lane == 0) | (lane == cur) | (lane == cur - 1)
    score = jnp.where(forced, NSA_FORCED, imp)
    work = jnp.where((lane <= cur) & (lane < nb_sel), score, -jnp.inf)
    lane_f = lane.astype(F32)
    neg = jnp.full(imp.shape, SEL_NEG, F32)
    lane16 = _lane((tq, LANES))
    idx_acc = jnp.zeros((tq, LANES), jnp.int32)
    val_acc = jnp.zeros((tq, LANES), jnp.int32)
    for n in range(min(NSA_TOP_N, nb_sel)):
        mx = jnp.max(work, axis=-1, keepdims=True)
        first = jnp.min(jnp.where(work == mx, lane_f, float(nbp)), axis=-1, keepdims=True)
        pick = lane_f == first
        valid_neg = jnp.where(mx > -jnp.inf, 0.0, SEL_NEG)
        neg = jnp.where(pick, jnp.maximum(neg, valid_neg), neg)
        work = jnp.where(pick, -jnp.inf, work)
        if not with_query:
            idx_acc = jnp.where(lane16 == n, first.astype(jnp.int32), idx_acc)
            val_acc = jnp.where(lane16 == n, jnp.where(mx > -jnp.inf, 1, 0), val_acc)
    if with_query:
        neg = neg.astype(BF16)
        for hf in range(n_halves):
            part = neg[:, hf * LANES:(hf + 1) * LANES]
            for h in range(NSA_HEADS):
                qa_ref[hf, h, :, 0:LANES] = part
                qa_ref[hf, h, :, LANES:2 * LANES] = q_ref[h]
    else:
        idx_ref[...] = idx_acc
        val_ref[...] = val_acc


def _select_prompt(imp, q_rot):
    t, nbp = imp.shape
    nh = q_rot.shape[0]
    tq = min(t, 512)
    n_halves = nbp // LANES
    return pl.pallas_call(
        functools.partial(_select_kernel, tq=tq, pos0=0, pos_step=1, nb_sel=t // NSA_BLOCK, n_halves=n_halves,
                          with_query=True, perm_pages=0),
        grid=(t // tq,),
        in_specs=[pl.BlockSpec((tq, nbp), lambda i: (i, 0)), pl.BlockSpec((nh, tq, LANES), lambda i: (0, i, 0))],
        out_specs=pl.BlockSpec((n_halves, nh, tq, 2 * LANES), lambda i: (0, 0, i, 0)),
        out_shape=jax.ShapeDtypeStruct((n_halves, nh, t, 2 * LANES), BF16),
        compiler_params=_cparams(("parallel",)), name="select_prompt",
    )(imp, q_rot)


def _select_sample(imp, pos, nb_sel, perm_pages):
    b, nbp = imp.shape
    return pl.pallas_call(
        functools.partial(_select_kernel, tq=b, pos0=pos, pos_step=0, nb_sel=nb_sel, n_halves=0, with_query=False,
                          perm_pages=perm_pages),
        grid=(1,),
        in_specs=[pl.BlockSpec((b, nbp), lambda i: (0, 0))],
        out_specs=[pl.BlockSpec((b, LANES), lambda i: (0, 0))] * 2,
        out_shape=[jax.ShapeDtypeStruct((b, LANES), jnp.int32)] * 2,
        compiler_params=_cparams(("arbitrary",)), name="select_sample",
    )(imp)


_FLASH_CQ = 256


def _flash_kernel(qi_ref, kj_ref, flag_ref, q_ref, k_ref, vt_ref, o_ref, m_scr, acc_scr, *, r, tq, tk, window,
                  sum_row):
    step_id = pl.program_id(1)
    i, kv_blk, flags = qi_ref[step_id], kj_ref[step_id], flag_ref[step_id]
    m_rows = r * tq
    cq = min(_FLASH_CQ, m_rows)

    @pl.when((flags & 1) != 0)
    def _():
        m_scr[...] = jnp.full(m_scr.shape, NEG_BIG, F32)
        acc_scr[...] = jnp.zeros(acc_scr.shape, F32)

    need_mask = None if window else kv_blk * tk + (tk - 1) > i * tq

    def q_chunk(c):
        if tq >= cq:
            per_head = tq // cq
            return q_ref[0, c // per_head, (c % per_head) * cq:(c % per_head + 1) * cq, :]
        heads = cq // tq
        return q_ref[0, c * heads:(c + 1) * heads].reshape(cq, q_ref.shape[-1])

    def scores(c, masked):
        s = _dot_nt(k_ref[0], q_chunk(c))
        if masked:
            qpos = i * tq + ((c * cq + _lane((tk, cq))) & (tq - 1))
            kpos = kv_blk * tk + lax.broadcasted_iota(jnp.int32, (tk, cq), 0)
            ok = kpos <= qpos
            if window:
                ok = ok & (qpos - kpos < window)
            s = jnp.where(ok, s, NEG_BIG)
        return s

    def step(masked):
        m_all = m_scr[...]
        n_chunks = m_rows // cq
        def accumulate(cols, alpha, p):
            acc_scr[:, cols] = alpha * acc_scr[:, cols] + _dot(vt_ref[0], p)

        s_next = scores(0, masked)
        pending = None
        for c in range(n_chunks):
            cols = slice(c * cq, (c + 1) * cq)
            s = s_next
            if c + 1 < n_chunks:
                s_next = scores(c + 1, masked)
            m_prev = m_all[:, cols]
            m_new = jnp.maximum(m_prev, jnp.max(s, axis=0, keepdims=True))
            alpha = jnp.exp2(m_prev - m_new)
            p = jnp.exp2(s - m_new).astype(BF16)
            m_scr[:, cols] = m_new
            if pending is not None:
                accumulate(*pending)
            pending = (cols, alpha, p)
        accumulate(*pending)

    if window:
        step(True)
    else:
        pl.when(need_mask)(lambda: step(True))
        pl.when(jnp.logical_not(need_mask))(lambda: step(False))

    @pl.when((flags & 2) != 0)
    def _():
        vals = acc_scr[0:sum_row, :] / acc_scr[sum_row:sum_row + 1, :]
        if sum_row < LANES:
            vals = jnp.concatenate([vals, jnp.zeros((LANES - sum_row, m_rows), F32)], axis=0)
        o_ref[0] = vals.T.reshape(r, tq, LANES).astype(o_ref.dtype)


def _flash(q, k, vt, *, tq, tk, groups, q_index, kv_index, sum_row, window=0, out_dtype=F32, name):
    r, t, dq = q.shape[1:]
    dv = vt.shape[1]
    tq, tk = min(tq, t), min(tk, t)
    qi, kj, flags = [], [], []
    for i in range(t // tq):
        if window:
            assert tq == tk and window % tk == 0
            blocks = [b for b in range(i - window // tk, i + 1) if b >= 0]
        else:
            blocks = list(range((i * tq + tq - 1) // tk + 1))
        for n, b in enumerate(blocks):
            qi.append(i)
            kj.append(b)
            flags.append((1 if n == 0 else 0) | (2 if n == len(blocks) - 1 else 0))
    qi, kj, flags = (jnp.asarray(np.array(a, np.int32)) for a in (qi, kj, flags))
    return pl.pallas_call(
        functools.partial(_flash_kernel, r=r, tq=tq, tk=tk, window=window, sum_row=sum_row),
        grid_spec=pltpu.PrefetchScalarGridSpec(
            num_scalar_prefetch=3, grid=(groups, int(qi.shape[0])),
            in_specs=[pl.BlockSpec((1, r, tq, dq), lambda g, s, qi, kj, fl: (q_index(g, kj[s] * tk), 0, qi[s], 0)),
                      pl.BlockSpec((1, tk, dq), lambda g, s, qi, kj, fl: (kv_index(g), kj[s], 0)),
                      pl.BlockSpec((1, dv, tk), lambda g, s, qi, kj, fl: (kv_index(g), 0, kj[s]))],
            out_specs=pl.BlockSpec((1, r, tq, LANES), lambda g, s, qi, kj, fl: (g, 0, qi[s], 0)),
            scratch_shapes=[pltpu.VMEM((1, r * tq), F32), pltpu.VMEM((dv, r * tq), F32)]),
        out_shape=jax.ShapeDtypeStruct((groups, r, t, LANES), out_dtype),
        compiler_params=_cparams(("parallel", "arbitrary")), name=name,
    )(qi, kj, flags, q, k, vt)


def _online_softmax_step(s, m_scr, l_scr):
    m_prev = m_scr[...]
    m_new = jnp.maximum(m_prev, jnp.max(s, axis=-1, keepdims=True))
    alpha = jnp.exp(m_prev - m_new)
    p = jnp.exp(s - m_new)
    l_scr[...] = alpha * l_scr[...] + jnp.sum(p, axis=-1, keepdims=True)
    m_scr[...] = m_new
    return alpha, p


def _decode_mla_kernel(pt_ref, q_ref, new_ref, *refs, pages):
    page_refs = refs[:pages]
    o_ref, m_scr, l_scr, acc_scr = refs[pages:]
    c = pl.program_id(1)
    q = q_ref[0]

    @pl.when(c == 0)
    def _():
        new = new_ref[0].astype(BF16).astype(F32)
        m_scr[...] = jnp.sum(q.astype(F32) * new, axis=-1, keepdims=True)
        l_scr[...] = jnp.ones(l_scr.shape, F32)
        acc_scr[...] = jnp.broadcast_to(new[:, 0:MLA_KV_LORA], acc_scr.shape)

    kt = jnp.concatenate([pr[0] for pr in page_refs], axis=1).astype(BF16)
    alpha, p = _online_softmax_step(_dot(q, kt), m_scr, l_scr)
    acc_scr[...] = alpha * acc_scr[...] + _dot_nt(p.astype(BF16), kt[0:MLA_KV_LORA])

    @pl.when(c == pl.num_programs(1) - 1)
    def _():
        o_ref[0] = acc_scr[...] / l_scr[...]


def _decode_mla(q, new_rows, cache_mla, page_table, pages):
    b, rows, kw = q.shape
    n_pages = page_table.shape[1]
    pages = min(pages, n_pages)
    view = jnp.transpose(cache_mla, (0, 2, 1))
    page_specs = [pl.BlockSpec((1, kw, PAGE), functools.partial(lambda i, c, pt, p: (pt[i, c * pages + p], 0, 0), p=p))
                  for p in range(pages)]
    return pl.pallas_call(
        functools.partial(_decode_mla_kernel, pages=pages),
        grid_spec=pltpu.PrefetchScalarGridSpec(
            num_scalar_prefetch=1, grid=(b, n_pages // pages),
            in_specs=[pl.BlockSpec((1, rows, kw), lambda i, c, pt: (i, 0, 0)),
                      pl.BlockSpec((1, 1, kw), lambda i, c, pt: (i, 0, 0))] + page_specs,
            out_specs=pl.BlockSpec((1, rows, MLA_KV_LORA), lambda i, c, pt: (i, 0, 0)),
            scratch_shapes=[pltpu.VMEM((rows, 1), F32), pltpu.VMEM((rows, 1), F32),
                            pltpu.VMEM((rows, MLA_KV_LORA), F32)]),
        out_shape=jax.ShapeDtypeStruct((b, rows, MLA_KV_LORA), F32),
        compiler_params=_cparams(("parallel", "arbitrary")), name="decode_mla",
    )(page_table, q, new_rows, *([view] * pages))


_DIFF_ROWS = 2 * DIFF_KVH * DIFF_GROUP
_DIFF_COLS = 2 * DIFF_KVH


def _decode_diff_kernel(pt_ref, q_ref, new_ref, *refs, pages):
    page_refs = refs[:pages]
    o_ref, m_scr, l_scr, acc_scr = refs[pages:]
    c = pl.program_id(1)
    q = q_ref[0]
    head = lax.broadcasted_iota(jnp.int32, (_DIFF_ROWS, 1), 0) >> 3

    @pl.when(c == 0)
    def _():
        new = new_ref[0].astype(BF16).astype(F32)
        knew = jnp.where(head == 0, new[:, 0:LANES], new[:, LANES:2 * LANES])
        vnew = jnp.where(head == 0, new[:, 2 * LANES:3 * LANES], new[:, 3 * LANES:4 * LANES])
        m_scr[...] = jnp.sum(q.astype(F32) * knew, axis=-1, keepdims=True)
        l_scr[...] = jnp.ones(l_scr.shape, F32)
        acc_scr[...] = vnew

    def cache_rows(c):
        return jnp.concatenate([pr[0, pl.ds(c, PAGE, stride=_DIFF_COLS), :] for pr in page_refs],
                               axis=0).astype(BF16)

    rows_q = _DIFF_ROWS // DIFF_KVH
    qf = q.astype(F32)
    s = jnp.concatenate([_dot_nt(qf[g * rows_q:(g + 1) * rows_q].astype(BF16), cache_rows(g))
                         for g in range(DIFF_KVH)], axis=0)
    alpha, p = _online_softmax_step(s, m_scr, l_scr)
    pv = jnp.concatenate([_dot(p[g * rows_q:(g + 1) * rows_q].astype(BF16), cache_rows(DIFF_KVH + g))
                          for g in range(DIFF_KVH)], axis=0)
    acc_scr[...] = alpha * acc_scr[...] + pv

    @pl.when(c == pl.num_programs(1) - 1)
    def _():
        o_ref[0] = acc_scr[...] / l_scr[...]


def _decode_diff(q, new_rows, cache_diff, page_table, pages):
    b, rows, _ = q.shape
    n_pool = cache_diff.shape[0]
    n_pages = page_table.shape[1]
    pages = min(pages, n_pages)
    view = cache_diff.reshape(n_pool, PAGE * _DIFF_COLS, LANES)
    page_specs = [pl.BlockSpec((1, PAGE * _DIFF_COLS, LANES),
                               functools.partial(lambda i, c, pt, p: (pt[i, c * pages + p], 0, 0), p=p))
                  for p in range(pages)]
    return pl.pallas_call(
        functools.partial(_decode_diff_kernel, pages=pages),
        grid_spec=pltpu.PrefetchScalarGridSpec(
            num_scalar_prefetch=1, grid=(b, n_pages // pages),
            in_specs=[pl.BlockSpec((1, rows, LANES), lambda i, c, pt: (i, 0, 0)),
                      pl.BlockSpec((1, 1, _DIFF_COLS * LANES), lambda i, c, pt: (i, 0, 0))] + page_specs,
            out_specs=pl.BlockSpec((1, rows, LANES), lambda i, c, pt: (i, 0, 0)),
            scratch_shapes=[pltpu.VMEM((rows, 1), F32), pltpu.VMEM((rows, 1), F32), pltpu.VMEM((rows, LANES), F32)]),
        out_shape=jax.ShapeDtypeStruct((b, rows, LANES), F32),
        compiler_params=_cparams(("parallel", "arbitrary")), name="decode_diff",
    )(page_table, q, new_rows, *([view] * pages))


def _nsa_decode_kernel(idx_ref, val_ref, pt_ref, q_ref, news_ref, neww_ref, win_ref, *refs, n_sel, nb_cached, pos,
                       win_pos0):
    blk_refs = refs[:n_sel]
    os_ref, ow_ref = refs[n_sel:]
    b = pl.program_id(0)
    q = q_ref[0]
    qf = q.astype(F32)

    def new_key_score(new_ref):
        row = new_ref[0].astype(BF16).astype(F32)
        kd = jnp.where(_lane(row.shape) < NSA_D, row, pltpu.roll(row, 64, axis=1))
        vnew = jnp.where(_lane(row.shape) < NSA_D, pltpu.roll(row, 64, axis=1), 0.0)
        return jnp.sum(qf * kd, axis=-1, keepdims=True), vnew

    def attend(ktv, ok, s_new, new_ok, vnew):
        ktv = ktv.astype(BF16)
        s = _dot(q, jnp.concatenate([ktv[0:NSA_D], ktv[0:NSA_D]], axis=0))
        s = jnp.where(ok, s, -jnp.inf)
        if new_ok is not True:
            s_new = jnp.where(new_ok, s_new, -jnp.inf)
        m = jnp.maximum(jnp.max(s, axis=-1, keepdims=True), s_new)
        m = jnp.where(m == -jnp.inf, 0.0, m)
        e = jnp.exp(s - m)
        e_new = jnp.exp(s_new - m)
        den = jnp.sum(e, axis=-1, keepdims=True) + e_new
        den = jnp.where(den > 0.0, den, 1.0)
        o = _dot_nt(e.astype(BF16), ktv)
        return (pltpu.roll(o, 64, axis=1) + e_new * vnew) / den

    ktv = jnp.concatenate([r[0] for r in blk_refs], axis=1)
    lane = _lane((NSA_HEADS, n_sel * PAGE))
    page_of, half_of = lane >> 7, (lane >> 6) & 1
    ok = jnp.zeros(lane.shape, F32)
    new_ok = jnp.int32(0)
    for n in range(n_sel):
        valid = val_ref[b * n_sel + n] > 0
        blk = idx_ref[b * n_sel + n]
        use = jnp.where(valid & (blk < nb_cached), 1.0, 0.0)
        ok = jnp.where(page_of == n, jnp.where(half_of == (blk & 1), use, 0.0), ok)
        new_ok = new_ok | jnp.where(valid & (blk == nb_cached), 1, 0)
    s_new, vnew = new_key_score(news_ref)
    os_ref[0] = attend(ktv, ok > 0.0, s_new, new_ok > 0, vnew)

    wpos = win_pos0 + _lane((NSA_HEADS, win_ref.shape[-1]))
    d = pos - wpos
    okw = (d >= 0) & (d < NSA_WINDOW) & (wpos >= 0)
    sw_new, vwnew = new_key_score(neww_ref)
    ow_ref[0] = attend(win_ref[0], okw, sw_new, True, vwnew)


def _nsa_decode(idx, val, page_table, q_rot, new_slc, new_win, state_win, cache_slc, pos):
    b = q_rot.shape[0]
    n_pages = page_table.shape[1]
    nb_cached = n_pages * PAGE // NSA_BLOCK
    n_sel = min(NSA_TOP_N, nb_cached + 1)
    wb = state_win.shape[1]
    blocks = _feature_major(cache_slc)
    state_win = _feature_major(state_win)

    def blk_index(i, idx_r, val_r, pt_r, n):
        blk = jnp.minimum(idx_r[i * n_sel + n], nb_cached - 1)
        return (pt_r[i, blk >> 1], 0, 0)

    blk_specs = [pl.BlockSpec((1, 2 * NSA_D, PAGE), functools.partial(blk_index, n=n)) for n in range(n_sel)]
    per_seq = lambda shp: pl.BlockSpec((1,) + shp, lambda i, a, c, d: (i, 0, 0))
    return pl.pallas_call(
        functools.partial(_nsa_decode_kernel, n_sel=n_sel, nb_cached=nb_cached, pos=pos, win_pos0=pos - wb),
        grid_spec=pltpu.PrefetchScalarGridSpec(
            num_scalar_prefetch=3, grid=(b,),
            in_specs=[per_seq((NSA_HEADS, LANES)), per_seq((1, LANES)), per_seq((1, LANES)), per_seq((2 * NSA_D, wb))]
            + blk_specs,
            out_specs=[per_seq((NSA_HEADS, LANES))] * 2),
        out_shape=[jax.ShapeDtypeStruct((b, NSA_HEADS, LANES), F32)] * 2,
        compiler_params=_cparams(("parallel",)), name="nsa_decode",
    )(idx[:, :n_sel].reshape(-1), val[:, :n_sel].reshape(-1), page_table, q_rot, new_slc, new_win, state_win, *([blocks] * n_sel))


def _mix0_kernel(x_ref, om_ref, oc_ref, os_ref, ow_ref, g_ref, w_ref, *refs, absorbed):
    if absorbed:
        uv_ref, o_ref = refs
    else:
        (o_ref,) = refs
    gates = g_ref[...]
    acc = x_ref[...]
    for h in range(MLA_HEADS):
        om = om_ref[h]
        if absorbed:
            om = _dot(om.astype(BF16), uv_ref[h])
        acc = acc + _dot(om.astype(BF16), w_ref[h])
    for h in range(NSA_HEADS):
        gc, gs, gw = (gates[:, k * NSA_HEADS + h:k * NSA_HEADS + h + 1] for k in range(3))
        on = gc * oc_ref[h] + gs * os_ref[h] + gw * ow_ref[h]
        acc = acc + _dot(on.astype(BF16), w_ref[MLA_HEADS + h])
    o_ref[...] = acc


def _mix0(x, o_mla, o_c, o_s, o_w, gates, w_out, uv=None):
    t, d = x.shape
    tm = min(t, 512)
    row = lambda w: pl.BlockSpec((tm, w), lambda i: (i, 0))
    head = pl.BlockSpec((NSA_HEADS, tm, LANES), lambda i: (0, i, 0))
    full = lambda a: pl.BlockSpec(a.shape, lambda i: (0,) * a.ndim)
    ins = [x, o_mla, o_c, o_s, o_w, gates, w_out] + ([uv] if uv is not None else [])
    specs = [row(d), head, head, head, head, row(LANES), full(w_out)] + ([full(uv)] if uv is not None else [])
    return pl.pallas_call(
        functools.partial(_mix0_kernel, absorbed=uv is not None),
        grid=(t // tm,), in_specs=specs, out_specs=row(d), out_shape=jax.ShapeDtypeStruct((t, d), F32),
        compiler_params=_cparams(("parallel",)), name="mix0",
    )(*ins)


def _ffn_kernel(x_ref, g_ref, wg_ref, wu_ref, wd_ref, fg_ref, o_ref, hn_scr, acc_scr, *, final_norm):
    f = pl.program_id(1)

    @pl.when(f == 0)
    def _():
        hn_scr[...] = _rms(x_ref[...], g_ref[...]).astype(BF16)
        acc_scr[...] = x_ref[...]

    hn = hn_scr[...]
    act = jax.nn.silu(_dot(hn, wg_ref[...])) * _dot(hn, wu_ref[...])
    acc_scr[...] += _dot(act.astype(BF16), wd_ref[...])

    @pl.when(f == pl.num_programs(1) - 1)
    def _():
        y = acc_scr[...]
        o_ref[...] = _rms(y, fg_ref[...]) if final_norm else y


def _ffn(x, gain, wg, wu, wd, final_gain, final_norm):
    t, d = x.shape
    dff = wg.shape[1]
    tm = min(t, 512)
    tf = dff // 2 if (dff // 2) % LANES == 0 else dff
    return pl.pallas_call(
        functools.partial(_ffn_kernel, final_norm=final_norm),
        grid=(t // tm, dff // tf),
        in_specs=[pl.BlockSpec((tm, d), lambda i, f: (i, 0)), pl.BlockSpec((1, d), lambda i, f: (0, 0)),
                  pl.BlockSpec((d, tf), lambda i, f: (0, f)), pl.BlockSpec((d, tf), lambda i, f: (0, f)),
                  pl.BlockSpec((tf, d), lambda i, f: (f, 0)), pl.BlockSpec((1, d), lambda i, f: (0, 0))],
        out_specs=pl.BlockSpec((tm, d), lambda i, f: (i, 0)),
        out_shape=jax.ShapeDtypeStruct((t, d), F32),
        scratch_shapes=[pltpu.VMEM((tm, d), BF16), pltpu.VMEM((tm, d), F32)],
        compiler_params=_cparams(("parallel", "arbitrary")), name="ffn",
    )(x, gain, wg, wu, wd, final_gain)


_N_QSLOT = DIFF_HEADS


def _proj1_kernel(x_ref, g_ref, w_ref, cn_ref, sn_ref, rows_ref, q_ref, k_ref, v_ref, *, tm, prompt):
    hn = _rms(x_ref[...], g_ref[...]).astype(BF16)
    z = _dot(hn, w_ref[...])
    cn, sn = cn_ref[...], sn_ref[...]
    low = _lane((tm, LANES)) < DIFF_D
    q_scale = DIFF_SCALE * (LOG2E if prompt else 1.0)
    sum_rows = jnp.where(lax.broadcasted_iota(jnp.int32, (SUM_ROW_PAD, tm), 0) == 0, 1.0, 0.0).astype(BF16)

    def slot(s):
        return z[:, s * LANES:(s + 1) * LANES]

    for g in range(DIFF_KVH):
        for r in range(DIFF_GROUP):
            h = g * DIFF_GROUP + r
            qp = (slot(h) * cn + slot(_N_QSLOT + h) * sn) * q_scale
            q_ref[2 * g, r] = jnp.where(low, qp, 0.0).astype(BF16)
            q_ref[2 * g + 1, r] = jnp.where(low, 0.0, qp).astype(BF16)
        k = slot(2 * _N_QSLOT + g) * cn + slot(2 * _N_QSLOT + DIFF_KVH + g) * sn
        v = slot(2 * _N_QSLOT + 2 * DIFF_KVH + g)
        rows_ref[:, g * LANES:(g + 1) * LANES] = k
        rows_ref[:, (DIFF_KVH + g) * LANES:(DIFF_KVH + g + 1) * LANES] = v
        k_ref[g] = k.astype(BF16)
        v_ref[g, 0:LANES, :] = v.T.astype(BF16)
        v_ref[g, LANES:LANES + SUM_ROW_PAD, :] = sum_rows


def _prep_layer1(w_in_c):
    nq = DIFF_HEADS * 2 * DIFF_D
    nk = DIFF_KVH * 2 * DIFF_D
    wq, wk, wv = w_in_c[:, :nq], w_in_c[:, nq:nq + nk], w_in_c[:, nq + nk:]
    return jnp.concatenate([wq, _rot_cols(wq, DIFF_D), wk, _rot_cols(wk, DIFF_D), wv], axis=1).astype(BF16)


def _proj1(x, gain, w1, cn, sn, prompt):
    t, d = x.shape
    tm = min(t, 256)
    row = lambda w: pl.BlockSpec((tm, w), lambda i: (i, 0))
    full = lambda a: pl.BlockSpec(a.shape, lambda i: (0,) * a.ndim)
    ng = 2 * DIFF_KVH
    return pl.pallas_call(
        functools.partial(_proj1_kernel, tm=tm, prompt=prompt),
        grid=(t // tm,),
        in_specs=[row(d), full(gain), full(w1), row(LANES), row(LANES)],
        out_specs=[row(2 * DIFF_KVH * LANES),
                   pl.BlockSpec((ng, DIFF_GROUP, tm, LANES), lambda i: (0, 0, i, 0)),
                   pl.BlockSpec((DIFF_KVH, tm, LANES), lambda i: (0, i, 0)),
                   pl.BlockSpec((DIFF_KVH, LANES + SUM_ROW_PAD, tm), lambda i: (0, 0, i))],
        out_shape=[jax.ShapeDtypeStruct((t, 2 * DIFF_KVH * LANES), F32),
                   jax.ShapeDtypeStruct((ng, DIFF_GROUP, t, LANES), BF16),
                   jax.ShapeDtypeStruct((DIFF_KVH, t, LANES), BF16),
                   jax.ShapeDtypeStruct((DIFF_KVH, LANES + SUM_ROW_PAD, t), BF16)],
        compiler_params=_cparams(("parallel",)), name="proj1",
    )(x, gain, w1, cn, sn)


def _mix1_kernel(x_ref, o_ref_in, lam_ref, sub_ref, w_ref, o_ref):
    lv = lam_ref[...]
    lam = (jnp.exp(jnp.sum(lv[0:1] * lv[1:2], axis=-1, keepdims=True))
           - jnp.exp(jnp.sum(lv[2:3] * lv[3:4], axis=-1, keepdims=True)) + DIFF_LAMBDA_INIT)
    acc = x_ref[...]
    for g in range(DIFF_KVH):
        for r in range(DIFF_GROUP):
            o = o_ref_in[2 * g, r] - lam * o_ref_in[2 * g + 1, r]
            o = _rms(o, sub_ref[...]) * (1.0 - DIFF_LAMBDA_INIT)
            acc = acc + _dot(o.astype(BF16), w_ref[g * DIFF_GROUP + r])
    o_ref[...] = acc


def _mix1(x, o, lam_vecs, subln, w_out):
    t, d = x.shape
    tm = min(t, 512)
    full = lambda a: pl.BlockSpec(a.shape, lambda i: (0,) * a.ndim)
    return pl.pallas_call(
        _mix1_kernel, grid=(t // tm,),
        in_specs=[pl.BlockSpec((tm, d), lambda i: (i, 0)),
                  pl.BlockSpec((2 * DIFF_KVH, DIFF_GROUP, tm, LANES), lambda i: (0, 0, i, 0)),
                  full(lam_vecs), full(subln), full(w_out)],
        out_specs=pl.BlockSpec((tm, d), lambda i: (i, 0)), out_shape=jax.ShapeDtypeStruct((t, d), F32),
        compiler_params=_cparams(("parallel",)), name="mix1",
    )(x, o, lam_vecs, subln, w_out)


def kernel(x_prompt, x_sample, cache_mla, cache_nsa_cmp, cache_nsa_slc, state_nsa_win, cache_diff, page_table,
           attn_norm, ffn_norm, final_norm,
           w_in_a, mla_q_norm, mla_kv_norm, w_mla_uq, w_mla_uk, w_mla_uv,
           nsa_pe_k, nsa_pe_v, nsa_cmp_w1_k, nsa_cmp_w2_k, nsa_cmp_w1_v, nsa_cmp_w2_v, w_out_a,
           w_in_c, diff_lambda_q1, diff_lambda_k1, diff_lambda_q2, diff_lambda_k2, diff_subln, w_out_c,
           w_ffn_gate, w_ffn_up, w_ffn_down):
    bp, t, d = x_prompt.shape
    b, ts, _ = x_sample.shape
    assert bp == 1 and ts == 1 and t % NSA_BLOCK == 0
    n_pool = cache_mla.shape[0]
    past = page_table.shape[1] * PAGE
    assert past % NSA_BLOCK == 0
    nh = MLA_HEADS

    w0, uq, ukv_prompt, ukt, uv_pad = _prep_layer0(w_in_a, w_mla_uq, w_mla_uk, w_mla_uv)
    pe, w1c, w2c, pe_t, w1c_t = _prep_compress(nsa_pe_k, nsa_pe_v, nsa_cmp_w1_k, nsa_cmp_w1_v, nsa_cmp_w2_k,
                                               nsa_cmp_w2_v)
    w1 = _prep_layer1(w_in_c)
    wo_a = w_out_a.reshape(2 * nh, NSA_D, d)
    wo_a = jnp.concatenate([wo_a, jnp.zeros_like(wo_a)], axis=1).astype(BF16)
    wo_c = w_out_c.reshape(DIFF_HEADS, 2 * DIFF_D, d).astype(BF16)
    wg, wu, wd = (w.astype(BF16) for w in (w_ffn_gate, w_ffn_up, w_ffn_down))
    lam_vecs = jnp.stack([diff_lambda_q1, diff_lambda_k1, diff_lambda_q2, diff_lambda_k2])
    row2 = lambda v: v.reshape(1, -1)

    xp = x_prompt[0]
    xs = x_sample[:, 0]
    tab_p = _rope_tables(jnp.arange(t, dtype=jnp.int32))
    tab_s = _rope_tables(jnp.full((b,), past, jnp.int32))

    (mla_p, cmp_p, slc_p, win_p, qm_p, qc_p, qr_p, gate_p, km_p, vm_p, ksa_p, vs_p, kw_p, vw_p) = _proj0(
        xp, row2(attn_norm[0]), w0, row2(mla_q_norm), row2(mla_kv_norm), uq, ukv_prompt, tab_p, True)
    o_mla_p = _flash(qm_p.reshape(nh, 1, t, LANES), km_p, vm_p, tq=2048, tk=1024, groups=nh,
                     q_index=lambda g, k0: g, kv_index=lambda g: g, sum_row=MLA_V, out_dtype=BF16,
                     name="flash_mla")[:, 0]
    nb_p = t // NSA_BLOCK
    nbp_p = -(-nb_p // LANES) * LANES
    kcvc_p = _compress_prompt(cmp_p, pe, w1c, w2c)
    o_c_p, imp_p = _cmp_attn_prompt(kcvc_p, qc_p, nbp_p)
    q_aug = _select_prompt(imp_p, qr_p)
    keys_per_half = LANES * NSA_BLOCK
    o_s_p = _flash(q_aug, ksa_p[None], vs_p[None], tq=512, tk=1024, groups=1,
                   q_index=lambda g, k0: k0 // keys_per_half, kv_index=lambda g: 0, sum_row=NSA_D,
                   name="flash_slc")[0]
    o_w_p = _flash(qr_p[None], kw_p[None], vw_p[None], tq=256, tk=256, groups=1, window=NSA_WINDOW,
                   q_index=lambda g, k0: 0, kv_index=lambda g: 0, sum_row=NSA_D, name="flash_win")[0]
    hp = _mix0(xp, o_mla_p, o_c_p, o_s_p, o_w_p, gate_p, wo_a)
    hp = _ffn(hp, row2(ffn_norm[0]), wg[0], wu[0], wd[0], row2(final_norm), False)

    (mla_s, cmp_s, slc_s, win_s, qm_s, qc_s, qr_s, gate_s, qa_s) = _proj0(
        xs, row2(attn_norm[0]), w0, row2(mla_q_norm), row2(mla_kv_norm), uq, ukt, tab_s, False)
    q_abs = jnp.transpose(qa_s[:, :, :MLA_KV_LORA + MLA_ROPE], (1, 0, 2))
    o_lat = _decode_mla(q_abs, mla_s[:, None, :], cache_mla, page_table, 64)
    n_pages = page_table.shape[1]
    nb_sel_s = -(-(past + ts) // NSA_BLOCK)
    nbp_s = -(-nb_sel_s // LANES) * LANES
    kcvc_s = _compress_paged(cache_nsa_cmp, page_table, pe_t, w1c_t, w2c, 64)
    o_c_s, imp_s = _cmp_attn_sample(kcvc_s, jnp.transpose(qc_s, (1, 0, 2)), past, nbp_s, n_pages)
    idx_s, val_s = _select_sample(imp_s[:, 0], past, nb_sel_s, n_pages)
    o_s_s, o_w_s = _nsa_decode(idx_s, val_s, page_table, jnp.transpose(qr_s, (1, 0, 2)), slc_s[:, None, :],
                               win_s[:, None, :], state_nsa_win, cache_nsa_slc, past)
    tr = lambda a: jnp.transpose(a, (1, 0, 2))
    hs = _mix0(xs, tr(o_lat), tr(o_c_s), tr(o_s_s), tr(o_w_s), gate_s, wo_a, uv_pad)
    hs = _ffn(hs, row2(ffn_norm[0]), wg[0], wu[0], wd[0], row2(final_norm), False)

    rows_p, qd_p, kd_p, vd_p = _proj1(hp, row2(attn_norm[1]), w1, tab_p[2], tab_p[3], True)
    o_d_p = _flash(qd_p, kd_p, vd_p, tq=1024, tk=1024, groups=2 * DIFF_KVH,
                   q_index=lambda g, k0: g, kv_index=lambda g: g // 2, sum_row=2 * DIFF_D, name="flash_diff")
    hp = _mix1(hp, o_d_p, lam_vecs, row2(diff_subln), wo_c)
    y_p = _ffn(hp, row2(ffn_norm[1]), wg[1], wu[1], wd[1], row2(final_norm), True)

    rows_s, qd_s, kd_s, vd_s = _proj1(hs, row2(attn_norm[1]), w1, tab_s[2], tab_s[3], False)
    qd = jnp.transpose(qd_s.reshape(_DIFF_ROWS, b, LANES), (1, 0, 2))
    o_d_s = _decode_diff(qd, rows_s[:, None, :], cache_diff, page_table, 32)
    o_d_s = jnp.transpose(o_d_s, (1, 0, 2)).reshape(2 * DIFF_KVH, DIFF_GROUP, b, LANES)
    hs = _mix1(hs, o_d_s, lam_vecs, row2(diff_subln), wo_c)
    y_s = _ffn(hs, row2(ffn_norm[1]), wg[1], wu[1], wd[1], row2(final_norm), True)

    wb_p = min(NSA_WINDOW, t)
    pair = lambda a: a.reshape(a.shape[0], 2, 1, NSA_D)
    new_win_s = jnp.concatenate([state_nsa_win, win_s.reshape(b, 1, 2, 1, NSA_D)], axis=1)[:, ts:]
    return (y_p[None], y_s[:, None], mla_p[None], pair(cmp_p)[None], pair(slc_p)[None], pair(win_p[t - wb_p:])[None],
            rows_p.reshape(1, t, 2, DIFF_KVH, 2 * DIFF_D),
            mla_s[:, None], pair(cmp_s)[:, None], pair(slc_s)[:, None], new_win_s,
            rows_s.reshape(b, 1, 2, DIFF_KVH, 2 * DIFF_D))
```

```python
import functools
import math

import numpy as np
import jax
import jax.numpy as jnp
from jax import lax
from jax.experimental import pallas as pl
from jax.experimental.pallas import tpu as pltpu

F32 = jnp.float32
BF16 = jnp.bfloat16

LANES = 128
VMEM_LIMIT = 56 * 1024 * 1024

ROPE_THETA = 10000.0
NORM_EPS = 1e-6
PAGE = 128
MLA_HEADS = 8
MLA_Q_LORA = 256
MLA_KV_LORA = 128
MLA_NOPE = 64
MLA_ROPE = 32
MLA_V = 64
MLA_SCALE = (MLA_NOPE + MLA_ROPE) ** -0.5
NSA_HEADS = 8
NSA_D = 64
NSA_BLOCK = 64
NSA_TOP_N = 16
NSA_WINDOW = 512
NSA_SCALE = NSA_D ** -0.5
NSA_FORCED = 1.0e4
DIFF_HEADS = 8
DIFF_KVH = 2
DIFF_D = 64
DIFF_GROUP = DIFF_HEADS // DIFF_KVH
DIFF_SCALE = DIFF_D ** -0.5
DIFF_LAMBDA_INIT = 0.8 - 0.6 * math.exp(-0.3 * 1)

LOG2E = math.log2(math.e)
SUM_ROW_PAD = 16
NEG_BIG = -1.0e30
SEL_NEG = -float(2 ** 40)


def _cparams(sem):
    return pltpu.CompilerParams(dimension_semantics=sem, vmem_limit_bytes=VMEM_LIMIT)


def _rms(x, g):
    return x * lax.rsqrt(jnp.mean(x * x, axis=-1, keepdims=True) + NORM_EPS) * g


def _dot(a, b):
    return jnp.dot(a, b, preferred_element_type=F32)


def _dot_nt(a, b):
    return lax.dot_general(a, b, (((1,), (1,)), ((), ())), preferred_element_type=F32)


def _lane(shape):
    return lax.broadcasted_iota(jnp.int32, shape, len(shape) - 1)


_S_CQ, _S_CKV, _S_KR, _S_KR_ROT, _S_QN, _S_QN_ROT = 0, 2, 3, 4, 5, 9
_S_CMP, _S_SLC, _S_SLC_ROT, _S_WIN, _S_WIN_ROT, _S_GATE, _N_SLOTS0 = 13, 14, 15, 16, 17, 18, 19
_VT_ROWS = NSA_D + SUM_ROW_PAD


def _proj0_kernel(x_ref, g_ref, w_ref, qn_ref, kvn_ref, uq_ref, ukv_ref, c32_ref, s32_ref, cn_ref, sn_ref,
                  *out_refs, prompt, tm):
    if prompt:
        (mla_ref, cmp_ref, slc_ref, win_ref, qm_ref, qc_ref, qr_ref, gate_ref,
         km_ref, vm_ref, ksa_ref, vs_ref, kw_ref, vw_ref) = out_refs
    else:
        (mla_ref, cmp_ref, slc_ref, win_ref, qm_ref, qc_ref, qr_ref, gate_ref, qa_ref) = out_refs

    hn = _rms(x_ref[...], g_ref[...]).astype(BF16)
    z = _dot(hn, w_ref[...])

    def slot(s, n=1):
        return z[:, s * LANES:(s + n) * LANES]

    c32, s32, cn, sn = c32_ref[...], s32_ref[...], cn_ref[...], sn_ref[...]
    lane = _lane((tm, LANES))
    low = lane < NSA_D

    cqn = _rms(slot(_S_CQ, 2), qn_ref[...]).astype(BF16)
    zq = _dot(cqn, uq_ref[...])
    ckvn = _rms(slot(_S_CKV), kvn_ref[...])
    kpe_hi = slot(_S_KR) * c32 + slot(_S_KR_ROT) * s32
    mla_ref[:, 0:MLA_KV_LORA] = ckvn
    mla_ref[:, MLA_KV_LORA:MLA_KV_LORA + MLA_ROPE] = pltpu.roll(kpe_hi, 64, axis=1)[:, 0:MLA_ROPE]
    ckvb = ckvn.astype(BF16)
    q_cos = jnp.where(low, 1.0, c32)
    nh = MLA_HEADS
    flash_scale = LOG2E if prompt else 1.0
    sum_row = jnp.where(lane == NSA_D, 1.0, 0.0)
    for h in range(nh):
        qh = (zq[:, h * LANES:(h + 1) * LANES] * q_cos
              + zq[:, (nh + h) * LANES:(nh + h + 1) * LANES] * s32) * (MLA_SCALE * flash_scale)
        qm_ref[h] = qh.astype(BF16)
        if not prompt:
            qlat = _dot(qh.astype(BF16), ukv_ref[h])
            qa_ref[h, :, 0:LANES] = qlat.astype(BF16)
            qa_ref[h, :, LANES:2 * LANES] = jnp.where(lane < MLA_ROPE, pltpu.roll(qh, 64, axis=1), 0.0).astype(BF16)
    if prompt:
        zk = _dot(ckvb, ukv_ref[0])
        zv = _dot(ckvb, ukv_ref[1])
        for h in range(nh):
            km_ref[h] = (zk[:, h * LANES:(h + 1) * LANES] + kpe_hi).astype(BF16)
            vm_ref[h] = (zv[:, h * LANES:(h + 1) * LANES] + sum_row).T[0:_VT_ROWS].astype(BF16)

    for p in range(NSA_HEADS // 2):
        plain = slot(_S_QN + p) * NSA_SCALE
        roped = (slot(_S_QN + p) * cn + slot(_S_QN_ROT + p) * sn) * (NSA_SCALE * flash_scale)
        qc_ref[2 * p] = jnp.where(low, plain, 0.0).astype(BF16)
        qc_ref[2 * p + 1] = jnp.where(low, 0.0, plain).astype(BF16)
        qr_ref[2 * p] = jnp.where(low, roped, 0.0).astype(BF16)
        qr_ref[2 * p + 1] = jnp.where(low, 0.0, roped).astype(BF16)

    ck = jnp.where(low, cn, 1.0)
    cmp_ref[...] = slot(_S_CMP)
    slc = slot(_S_SLC) * ck + slot(_S_SLC_ROT) * sn
    win = slot(_S_WIN) * ck + slot(_S_WIN_ROT) * sn
    slc_ref[...] = slc
    win_ref[...] = win
    gate_ref[...] = jax.nn.sigmoid(slot(_S_GATE))
    if prompt:
        slc_sw = pltpu.roll(slc, 64, axis=1)
        win_sw = pltpu.roll(win, 64, axis=1)
        row = pl.program_id(0) * tm + lax.broadcasted_iota(jnp.int32, (tm, LANES), 0)
        onehot = ((row >> 6) & (LANES - 1)) == lane
        ksa_ref[:, 0:LANES] = jnp.where(onehot, 1.0, 0.0).astype(BF16)
        ksa_ref[:, LANES:2 * LANES] = jnp.where(low, slc, slc_sw).astype(BF16)
        is_sum = lane == NSA_D
        keep = lane <= NSA_D
        vs_ref[...] = jnp.where(is_sum, 1.0, jnp.where(keep, slc_sw, 0.0)).T[0:_VT_ROWS].astype(BF16)
        kw_ref[...] = jnp.where(low, win, win_sw).astype(BF16)
        vw_ref[...] = jnp.where(is_sum, 1.0, jnp.where(keep, win_sw, 0.0)).T[0:_VT_ROWS].astype(BF16)


def _rot_cols(w, d):
    k = w.shape[0]
    w3 = w.reshape(k, -1, d)
    return jnp.concatenate([-w3[..., d // 2:], w3[..., :d // 2]], axis=-1).reshape(k, -1)


def _place(w, left, width=LANES):
    return jnp.pad(w, ((0, 0), (left, width - left - w.shape[1])))


def _prep_layer0(w_in_a, w_mla_uq, w_mla_uk, w_mla_uv):
    o = np.cumsum([0, 256, 128, 32, 512, 64, 64, 64, 64, 64, 64, 24]).tolist()
    c_q, c_kv, k_r, q_n = (w_in_a[:, o[i]:o[i + 1]] for i in range(4))
    kv_c = w_in_a[:, o[4]:o[6]]
    kv_s = w_in_a[:, o[6]:o[8]]
    kv_w = w_in_a[:, o[8]:o[10]]
    g = w_in_a[:, o[10]:o[11]]
    z64 = jnp.zeros((w_in_a.shape[0], 64), F32)
    cols = [c_q, c_kv, _place(k_r, 64), _place(_rot_cols(k_r, MLA_ROPE), 64), q_n, _rot_cols(q_n, NSA_D),
            kv_c,
            kv_s, jnp.concatenate([_rot_cols(kv_s[:, :64], 64), z64], axis=1),
            kv_w, jnp.concatenate([_rot_cols(kv_w[:, :64], 64), z64], axis=1),
            _place(g, 0)]
    w0 = jnp.concatenate(cols, axis=1).astype(BF16)
    uq3 = w_mla_uq.reshape(MLA_Q_LORA, MLA_HEADS, MLA_NOPE + MLA_ROPE)
    nope, pe = uq3[..., :MLA_NOPE], uq3[..., MLA_NOPE:]
    pe_rot = jnp.concatenate([-pe[..., MLA_ROPE // 2:], pe[..., :MLA_ROPE // 2]], axis=-1)
    pad32 = jnp.zeros(pe.shape, F32)
    plain = jnp.concatenate([nope, pe, pad32], axis=-1).reshape(MLA_Q_LORA, -1)
    rot = jnp.concatenate([jnp.zeros(nope.shape, F32), pe_rot, pad32], axis=-1).reshape(MLA_Q_LORA, -1)
    uq = jnp.concatenate([plain, rot], axis=1).astype(BF16)
    pad_k = jnp.zeros((MLA_KV_LORA, MLA_HEADS, LANES - MLA_NOPE), F32)
    uk_slots = jnp.concatenate([w_mla_uk, pad_k], axis=-1).reshape(MLA_KV_LORA, -1)
    uv_slots = jnp.concatenate([w_mla_uv, pad_k], axis=-1).reshape(MLA_KV_LORA, -1)
    ukv_prompt = jnp.stack([uk_slots, uv_slots]).astype(BF16)
    ukt = jnp.transpose(w_mla_uk, (1, 2, 0))
    ukt = jnp.concatenate([ukt, jnp.zeros_like(ukt)], axis=1).astype(BF16)
    uv_pad = jnp.transpose(jnp.concatenate([w_mla_uv, pad_k], axis=-1), (1, 0, 2)).astype(BF16)
    return w0, uq, ukv_prompt, ukt, uv_pad


def _rope_tables(pos):
    def cs(dim):
        inv = ROPE_THETA ** (-jnp.arange(0, dim, 2, dtype=F32) / dim)
        ang = pos.astype(F32)[:, None] * inv[None, :]
        ang = jnp.concatenate([ang, ang], axis=-1)
        return jnp.cos(ang), jnp.sin(ang)
    c32, s32 = cs(MLA_ROPE)
    cn, sn = cs(NSA_D)
    return (_place(c32, 64), _place(s32, 64), jnp.concatenate([cn, cn], axis=1), jnp.concatenate([sn, sn], axis=1))


def _proj0(x, gain, w0, qnorm, kvnorm, uq, ukv, tables, prompt):
    t, d = x.shape
    tm = min(t, 256)
    nh = MLA_HEADS
    row = lambda w: pl.BlockSpec((tm, w), lambda i: (i, 0))
    head = lambda w: pl.BlockSpec((nh, tm, w), lambda i: (0, i, 0))
    full = lambda a: pl.BlockSpec(a.shape, lambda i: (0,) * a.ndim)
    out_shape = [jax.ShapeDtypeStruct((t, MLA_KV_LORA + MLA_ROPE), F32)] + [jax.ShapeDtypeStruct((t, LANES), F32)] * 3
    out_specs = [row(MLA_KV_LORA + MLA_ROPE), row(LANES), row(LANES), row(LANES)]
    out_shape += [jax.ShapeDtypeStruct((nh, t, LANES), BF16)] * 3 + [jax.ShapeDtypeStruct((t, LANES), F32)]
    out_specs += [head(LANES)] * 3 + [row(LANES)]
    if prompt:
        col = pl.BlockSpec((_VT_ROWS, tm), lambda i: (0, i))
        out_shape += [jax.ShapeDtypeStruct((nh, t, LANES), BF16), jax.ShapeDtypeStruct((nh, _VT_ROWS, t), BF16)]
        out_specs += [head(LANES), pl.BlockSpec((nh, _VT_ROWS, tm), lambda i: (0, 0, i))]
        out_shape += [jax.ShapeDtypeStruct((t, 2 * LANES), BF16), jax.ShapeDtypeStruct((_VT_ROWS, t), BF16),
                      jax.ShapeDtypeStruct((t, LANES), BF16), jax.ShapeDtypeStruct((_VT_ROWS, t), BF16)]
        out_specs += [row(2 * LANES), col, row(LANES), col]
    else:
        out_shape += [jax.ShapeDtypeStruct((nh, t, 2 * LANES), BF16)]
        out_specs += [head(2 * LANES)]
    return pl.pallas_call(
        functools.partial(_proj0_kernel, prompt=prompt, tm=tm),
        grid=(t // tm,),
        in_specs=[row(d), full(gain), full(w0), full(qnorm), full(kvnorm), full(uq), full(ukv)] + [row(LANES)] * 4,
        out_specs=out_specs, out_shape=out_shape,
        compiler_params=_cparams(("parallel",)),
        name="proj0_prompt" if prompt else "proj0_sample",
    )(x, gain, w0, qnorm, kvnorm, uq, ukv, *tables)


_CMP_PAIR = 2
_STAGE_PITCH = PAGE + 8


def _compress_rows(src_ref, n_blocks, pe_ref, w1_ref):
    acc = None
    for r0 in range(0, NSA_BLOCK, _CMP_PAIR):
        xs = [src_ref[pl.ds(r0 + u, n_blocks, stride=NSA_BLOCK), :] + pe_ref[r0 + u:r0 + u + 1, :]
              for u in range(_CMP_PAIR)]
        part = _dot(jnp.concatenate(xs, axis=1).astype(BF16), w1_ref[r0 // _CMP_PAIR])
        acc = part if acc is None else acc + part
    return acc


def _compress_prompt_kernel(f_ref, pe_ref, w1_ref, w2_ref, o_ref, *, n_blocks):
    pre = _compress_rows(f_ref, n_blocks, pe_ref, w1_ref)
    o_ref[...] = _dot(jax.nn.silu(pre).astype(BF16), w2_ref[...])


def _compress_paged_kernel(pt_ref, *refs, pages):
    page_refs = refs[:pages]
    pe_ref, w1_ref, w2_ref, o_ref, stage, pre_k, pre_v = refs[pages:]
    ch = pl.program_id(1)
    for p in range(pages):
        stage[_STAGE_PITCH * p:_STAGE_PITCH * p + PAGE, :] = page_refs[p][0]
    for kv, pre in enumerate((pre_k, pre_v)):
        acc = None
        for f0 in range(0, NSA_D, _CMP_PAIR):
            xs = [stage[pl.ds(kv * NSA_D + f0 + u, pages, stride=_STAGE_PITCH), :]
                  + pe_ref[kv * NSA_D + f0 + u:kv * NSA_D + f0 + u + 1, :] for u in range(_CMP_PAIR)]
            part = _dot(jnp.concatenate(xs, axis=1).astype(BF16), w1_ref[kv, f0 // _CMP_PAIR])
            acc = part if acc is None else acc + part
        pre[pl.ds(pl.multiple_of(ch * pages, pages), pages), :] = acc

    @pl.when(ch == pl.num_programs(1) - 1)
    def _():
        pk, pv = pre_k[...], pre_v[...]
        low = _lane(pk.shape) < NSA_D
        even = jnp.where(low, pk, pltpu.roll(pv, 64, axis=1))
        odd = jnp.where(low, pltpu.roll(pk, 64, axis=1), pv)
        both = jnp.concatenate([even, odd], axis=0)
        o_ref[0] = _dot(jax.nn.silu(both).astype(BF16), w2_ref[...])


def _prep_compress(pe_k, pe_v, w1_k, w1_v, w2_k, w2_v):
    pe = jnp.concatenate([pe_k, pe_v], axis=1)
    w1k3 = w1_k.reshape(NSA_BLOCK, NSA_D, NSA_D)
    w1v3 = w1_v.reshape(NSA_BLOCK, NSA_D, NSA_D)
    w1 = jnp.zeros((NSA_BLOCK, 2, NSA_D, 2 * NSA_D), F32)
    w1 = w1.at[:, 0, :, :NSA_D].set(w1k3).at[:, 1, :, NSA_D:].set(w1v3)
    w1 = w1.reshape(NSA_BLOCK // _CMP_PAIR, _CMP_PAIR * LANES, 2 * NSA_D).astype(BF16)
    w2 = jnp.zeros((2 * NSA_D, 2 * NSA_D), F32)
    w2 = w2.at[:NSA_D, :NSA_D].set(w2_k).at[NSA_D:, NSA_D:].set(w2_v).astype(BF16)
    pe_t = jnp.tile(jnp.concatenate([pe_k.T, pe_v.T], axis=0), (1, 2))
    def per_dim(w3):
        wd = jnp.transpose(w3, (1, 0, 2))
        z = jnp.zeros_like(wd)
        top = jnp.concatenate([wd, z], axis=2)
        bot = jnp.concatenate([z, wd], axis=2)
        return jnp.concatenate([top, bot], axis=1)
    w1_t = jnp.stack([per_dim(w1k3), per_dim(w1v3)])
    w1_t = w1_t.reshape(2, NSA_D // _CMP_PAIR, _CMP_PAIR * LANES, LANES).astype(BF16)
    return pe, w1, w2, pe_t, w1_t


def _compress_prompt(cmp_rows, pe, w1, w2):
    t = cmp_rows.shape[0]
    nb = t // NSA_BLOCK
    full = lambda a: pl.BlockSpec(a.shape, lambda i: (0,) * a.ndim)
    return pl.pallas_call(
        functools.partial(_compress_prompt_kernel, n_blocks=nb),
        grid=(1,), in_specs=[full(cmp_rows), full(pe), full(w1), full(w2)],
        out_specs=pl.BlockSpec((nb, LANES), lambda i: (0, 0)),
        out_shape=jax.ShapeDtypeStruct((nb, LANES), F32),
        compiler_params=_cparams(("arbitrary",)), name="compress_prompt",
    )(cmp_rows, pe, w1, w2)


def _feature_major(cache):
    n, rows = cache.shape[:2]
    return jnp.transpose(cache, (0, 2, 3, 4, 1)).reshape(n, 2 * NSA_D, rows)


def _compress_paged(cache_cmp, page_table, pe_t, w1_t, w2, pages):
    b, n_pages = page_table.shape
    nb = n_pages * PAGE // NSA_BLOCK
    view = _feature_major(cache_cmp)
    pages = min(pages, n_pages)
    page_specs = [pl.BlockSpec((1, 2 * NSA_D, PAGE),
                               functools.partial(lambda i, c, pt, p: (pt[i, c * pages + p], 0, 0), p=p))
                  for p in range(pages)]
    full = lambda a: pl.BlockSpec(a.shape, lambda i, c, pt: (0,) * a.ndim)
    return pl.pallas_call(
        functools.partial(_compress_paged_kernel, pages=pages),
        grid_spec=pltpu.PrefetchScalarGridSpec(
            num_scalar_prefetch=1, grid=(b, n_pages // pages),
            in_specs=page_specs + [full(pe_t), full(w1_t), full(w2)],
            out_specs=pl.BlockSpec((1, nb, LANES), lambda i, c, pt: (i, 0, 0)),
            scratch_shapes=[pltpu.VMEM((_STAGE_PITCH * pages, LANES), F32), pltpu.VMEM((n_pages, LANES), F32),
                            pltpu.VMEM((n_pages, LANES), F32)]),
        out_shape=jax.ShapeDtypeStruct((b, nb, LANES), F32),
        compiler_params=_cparams(("parallel", "arbitrary")), name="compress_paged",
    )(page_table, *([view] * pages), pe_t, w1_t, w2)


def _block_id(lane, perm_pages):
    if not perm_pages:
        return lane
    assert perm_pages & (perm_pages - 1) == 0
    shift = perm_pages.bit_length() - 1
    return jnp.where(lane < 2 * perm_pages, 2 * (lane & (perm_pages - 1)) + (lane >> shift), lane)


def _cmp_attn_kernel(kv_ref, q_ref, o_ref, imp_ref, *, rows_per_query, tq, pos0, pos_step, nbp, perm_pages):
    kv = kv_ref[...] if len(kv_ref.shape) == 2 else kv_ref[0]
    nb = kv.shape[0]
    sw = pltpu.roll(kv, 64, axis=1)
    kdup = jnp.where(_lane(kv.shape) < NSA_D, kv, sw).astype(BF16)
    vk = sw.astype(BF16)
    if rows_per_query:
        q = q_ref[0]
        qpos = jnp.full((q.shape[0], 1), pos0, jnp.int32)
    else:
        q = q_ref[...].reshape(NSA_HEADS * tq, LANES)
        r = lax.broadcasted_iota(jnp.int32, (NSA_HEADS * tq, 1), 0)
        qpos = pos0 + pos_step * (pl.program_id(0) * tq + (r & (tq - 1)))
    s = _dot_nt(q, kdup)
    last = (_block_id(_lane(s.shape), perm_pages) + 1) * NSA_BLOCK - 1
    s = jnp.where(last <= qpos, s, -jnp.inf)
    m = jnp.max(s, axis=-1, keepdims=True)
    m = jnp.where(m == -jnp.inf, 0.0, m)
    e = jnp.exp(s - m)
    den = jnp.sum(e, axis=-1, keepdims=True)
    p = e / jnp.where(den > 0.0, den, 1.0)
    o = _dot(p.astype(BF16), vk)
    if rows_per_query:
        o_ref[0] = o
        imp = jnp.sum(p, axis=0, keepdims=True)
        if nbp > nb:
            imp = jnp.concatenate([imp, jnp.zeros((1, nbp - nb), F32)], axis=1)
        imp_ref[0] = imp
    else:
        o_ref[...] = o.reshape(NSA_HEADS, tq, LANES)
        imp = jnp.sum(p.reshape(NSA_HEADS, tq, nb), axis=0)
        if nbp > nb:
            imp = jnp.concatenate([imp, jnp.zeros((tq, nbp - nb), F32)], axis=1)
        imp_ref[...] = imp


def _cmp_attn_prompt(kcvc, q_cmp, nbp):
    nh, t, _ = q_cmp.shape
    nb = kcvc.shape[0]
    tq = min(t, 128)
    return pl.pallas_call(
        functools.partial(_cmp_attn_kernel, rows_per_query=False, tq=tq, pos0=0, pos_step=1, nbp=nbp, perm_pages=0),
        grid=(t // tq,),
        in_specs=[pl.BlockSpec((nb, LANES), lambda i: (0, 0)), pl.BlockSpec((nh, tq, LANES), lambda i: (0, i, 0))],
        out_specs=[pl.BlockSpec((nh, tq, LANES), lambda i: (0, i, 0)), pl.BlockSpec((tq, nbp), lambda i: (i, 0))],
        out_shape=[jax.ShapeDtypeStruct((nh, t, LANES), F32), jax.ShapeDtypeStruct((t, nbp), F32)],
        compiler_params=_cparams(("parallel",)), name="cmp_attn_prompt",
    )(kcvc, q_cmp)


def _cmp_attn_sample(kcvc, q_cmp, pos, nbp, perm_pages):
    b, nh, _ = q_cmp.shape
    nb = kcvc.shape[1]
    return pl.pallas_call(
        functools.partial(_cmp_attn_kernel, rows_per_query=True, tq=1, pos0=pos, pos_step=0, nbp=nbp,
                          perm_pages=perm_pages),
        grid=(b,),
        in_specs=[pl.BlockSpec((1, nb, LANES), lambda i: (i, 0, 0)), pl.BlockSpec((1, nh, LANES), lambda i: (i, 0, 0))],
        out_specs=[pl.BlockSpec((1, nh, LANES), lambda i: (i, 0, 0)), pl.BlockSpec((1, 1, nbp), lambda i: (i, 0, 0))],
        out_shape=[jax.ShapeDtypeStruct((b, nh, LANES), F32), jax.ShapeDtypeStruct((b, 1, nbp), F32)],
        compiler_params=_cparams(("parallel",)), name="cmp_attn_sample",
    )(kcvc, q_cmp)


def _select_kernel(imp_ref, *refs, tq, pos0, pos_step, nb_sel, n_halves, with_query, perm_pages):
    if with_query:
        q_ref, qa_ref = refs
    else:
        idx_ref, val_ref = refs
    imp = imp_ref[...]
    nbp = imp.shape[1]
    lane = _block_id(_lane(imp.shape), perm_pages)
    qpos = pos0 + pos_step * (pl.program_id(0) * tq + lax.broadcasted_iota(jnp.int32, (tq, 1), 0))
    cur = qpos >> 6
    forced = (lane == 0) | (lane == cur) | (lane == cur - 1)
    score = jnp.where(forced, NSA_FORCED, imp)
    work = jnp.where((lane <= cur) & (lane < nb_sel), score, -jnp.inf)
    lane_f = lane.astype(F32)
    neg = jnp.full(imp.shape, SEL_NEG, F32)
    lane16 = _lane((tq, LANES))
    idx_acc = jnp.zeros((tq, LANES), jnp.int32)
    val_acc = jnp.zeros((tq, LANES), jnp.int32)
    for n in range(min(NSA_TOP_N, nb_sel)):
        mx = jnp.max(work, axis=-1, keepdims=True)
        first = jnp.min(jnp.where(work == mx, lane_f, float(nbp)), axis=-1, keepdims=True)
        pick = lane_f == first
        valid_neg = jnp.where(mx > -jnp.inf, 0.0, SEL_NEG)
        neg = jnp.where(pick, jnp.maximum(neg, valid_neg), neg)
        work = jnp.where(pick, -jnp.inf, work)
        if not with_query:
            idx_acc = jnp.where(lane16 == n, first.astype(jnp.int32), idx_acc)
            val_acc = jnp.where(lane16 == n, jnp.where(mx > -jnp.inf, 1, 0), val_acc)
    if with_query:
        neg = neg.astype(BF16)
        for hf in range(n_halves):
            part = neg[:, hf * LANES:(hf + 1) * LANES]
            for h in range(NSA_HEADS):
                qa_ref[hf, h, :, 0:LANES] = part
                qa_ref[hf, h, :, LANES:2 * LANES] = q_ref[h]
    else:
        idx_ref[...] = idx_acc
        val_ref[...] = val_acc


def _select_prompt(imp, q_rot):
    t, nbp = imp.shape
    nh = q_rot.shape[0]
    tq = min(t, 512)
    n_halves = nbp // LANES
    return pl.pallas_call(
        functools.partial(_select_kernel, tq=tq, pos0=0, pos_step=1, nb_sel=t // NSA_BLOCK, n_halves=n_halves,
                          with_query=True, perm_pages=0),
        grid=(t // tq,),
        in_specs=[pl.BlockSpec((tq, nbp), lambda i: (i, 0)), pl.BlockSpec((nh, tq, LANES), lambda i: (0, i, 0))],
        out_specs=pl.BlockSpec((n_halves, nh, tq, 2 * LANES), lambda i: (0, 0, i, 0)),
        out_shape=jax.ShapeDtypeStruct((n_halves, nh, t, 2 * LANES), BF16),
        compiler_params=_cparams(("parallel",)), name="select_prompt",
    )(imp, q_rot)


def _select_sample(imp, pos, nb_sel, perm_pages):
    b, nbp = imp.shape
    return pl.pallas_call(
        functools.partial(_select_kernel, tq=b, pos0=pos, pos_step=0, nb_sel=nb_sel, n_halves=0, with_query=False,
                          perm_pages=perm_pages),
        grid=(1,),
        in_specs=[pl.BlockSpec((b, nbp), lambda i: (0, 0))],
        out_specs=[pl.BlockSpec((b, LANES), lambda i: (0, 0))] * 2,
        out_shape=[jax.ShapeDtypeStruct((b, LANES), jnp.int32)] * 2,
        compiler_params=_cparams(("arbitrary",)), name="select_sample",
    )(imp)


_FLASH_CQ = 256


def _flash_kernel(qi_ref, kj_ref, flag_ref, q_ref, k_ref, vt_ref, o_ref, m_scr, acc_scr, *, r, tq, tk, window,
                  sum_row):
    step_id = pl.program_id(1)
    i, kv_blk, flags = qi_ref[step_id], kj_ref[step_id], flag_ref[step_id]
    m_rows = r * tq
    cq = min(_FLASH_CQ, m_rows)

    @pl.when((flags & 1) != 0)
    def _():
        m_scr[...] = jnp.full(m_scr.shape, NEG_BIG, F32)
        acc_scr[...] = jnp.zeros(acc_scr.shape, F32)

    need_mask = None if window else kv_blk * tk + (tk - 1) > i * tq

    def q_chunk(c):
        if tq >= cq:
            per_head = tq // cq
            return q_ref[0, c // per_head, (c % per_head) * cq:(c % per_head + 1) * cq, :]
        heads = cq // tq
        return q_ref[0, c * heads:(c + 1) * heads].reshape(cq, q_ref.shape[-1])

    def scores(c, masked):
        s = _dot_nt(k_ref[0], q_chunk(c))
        if masked:
            qpos = i * tq + ((c * cq + _lane((tk, cq))) & (tq - 1))
            kpos = kv_blk * tk + lax.broadcasted_iota(jnp.int32, (tk, cq), 0)
            ok = kpos <= qpos
            if window:
                ok = ok & (qpos - kpos < window)
            s = jnp.where(ok, s, NEG_BIG)
        return s

    def step(masked):
        m_all = m_scr[...]
        n_chunks = m_rows // cq
        def accumulate(cols, alpha, p):
            acc_scr[:, cols] = alpha * acc_scr[:, cols] + _dot(vt_ref[0], p)

        s_next = scores(0, masked)
        pending = None
        for c in range(n_chunks):
            cols = slice(c * cq, (c + 1) * cq)
            s = s_next
            if c + 1 < n_chunks:
                s_next = scores(c + 1, masked)
            m_prev = m_all[:, cols]
            m_new = jnp.maximum(m_prev, jnp.max(s, axis=0, keepdims=True))
            alpha = jnp.exp2(m_prev - m_new)
            p = jnp.exp2(s - m_new).astype(BF16)
            m_scr[:, cols] = m_new
            if pending is not None:
                accumulate(*pending)
            pending = (cols, alpha, p)
        accumulate(*pending)

    if window:
        step(True)
    else:
        pl.when(need_mask)(lambda: step(True))
        pl.when(jnp.logical_not(need_mask))(lambda: step(False))

    @pl.when((flags & 2) != 0)
    def _():
        vals = acc_scr[0:sum_row, :] / acc_scr[sum_row:sum_row + 1, :]
        if sum_row < LANES:
            vals = jnp.concatenate([vals, jnp.zeros((LANES - sum_row, m_rows), F32)], axis=0)
        o_ref[0] = vals.T.reshape(r, tq, LANES).astype(o_ref.dtype)


def _flash(q, k, vt, *, tq, tk, groups, q_index, kv_index, sum_row, window=0, out_dtype=F32, name):
    r, t, dq = q.shape[1:]
    dv = vt.shape[1]
    tq, tk = min(tq, t), min(tk, t)
    qi, kj, flags = [], [], []
    for i in range(t // tq):
        if window:
            assert tq == tk and window % tk == 0
            blocks = [b for b in range(i - window // tk, i + 1) if b >= 0]
        else:
            blocks = list(range((i * tq + tq - 1) // tk + 1))
        for n, b in enumerate(blocks):
            qi.append(i)
            kj.append(b)
            flags.append((1 if n == 0 else 0) | (2 if n == len(blocks) - 1 else 0))
    qi, kj, flags = (jnp.asarray(np.array(a, np.int32)) for a in (qi, kj, flags))
    return pl.pallas_call(
        functools.partial(_flash_kernel, r=r, tq=tq, tk=tk, window=window, sum_row=sum_row),
        grid_spec=pltpu.PrefetchScalarGridSpec(
            num_scalar_prefetch=3, grid=(groups, int(qi.shape[0])),
            in_specs=[pl.BlockSpec((1, r, tq, dq), lambda g, s, qi, kj, fl: (q_index(g, kj[s] * tk), 0, qi[s], 0)),
                      pl.BlockSpec((1, tk, dq), lambda g, s, qi, kj, fl: (kv_index(g), kj[s], 0)),
                      pl.BlockSpec((1, dv, tk), lambda g, s, qi, kj, fl: (kv_index(g), 0, kj[s]))],
            out_specs=pl.BlockSpec((1, r, tq, LANES), lambda g, s, qi, kj, fl: (g, 0, qi[s], 0)),
            scratch_shapes=[pltpu.VMEM((1, r * tq), F32), pltpu.VMEM((dv, r * tq), F32)]),
        out_shape=jax.ShapeDtypeStruct((groups, r, t, LANES), out_dtype),
        compiler_params=_cparams(("parallel", "arbitrary")), name=name,
    )(qi, kj, flags, q, k, vt)


def _online_softmax_step(s, m_scr, l_scr):
    m_prev = m_scr[...]
    m_new = jnp.maximum(m_prev, jnp.max(s, axis=-1, keepdims=True))
    alpha = jnp.exp(m_prev - m_new)
    p = jnp.exp(s - m_new)
    l_scr[...] = alpha * l_scr[...] + jnp.sum(p, axis=-1, keepdims=True)
    m_scr[...] = m_new
    return alpha, p


def _decode_mla_kernel(pt_ref, q_ref, new_ref, *refs, pages):
    page_refs = refs[:pages]
    o_ref, m_scr, l_scr, acc_scr = refs[pages:]
    c = pl.program_id(1)
    q = q_ref[0]

    @pl.when(c == 0)
    def _():
        new = new_ref[0].astype(BF16).astype(F32)
        m_scr[...] = jnp.sum(q.astype(F32) * new, axis=-1, keepdims=True)
        l_scr[...] = jnp.ones(l_scr.shape, F32)
        acc_scr[...] = jnp.broadcast_to(new[:, 0:MLA_KV_LORA], acc_scr.shape)

    kt = jnp.concatenate([pr[0] for pr in page_refs], axis=1).astype(BF16)
    alpha, p = _online_softmax_step(_dot(q, kt), m_scr, l_scr)
    acc_scr[...] = alpha * acc_scr[...] + _dot_nt(p.astype(BF16), kt[0:MLA_KV_LORA])

    @pl.when(c == pl.num_programs(1) - 1)
    def _():
        o_ref[0] = acc_scr[...] / l_scr[...]


def _decode_mla(q, new_rows, cache_mla, page_table, pages):
    b, rows, kw = q.shape
    n_pages = page_table.shape[1]
    pages = min(pages, n_pages)
    view = jnp.transpose(cache_mla, (0, 2, 1))
    page_specs = [pl.BlockSpec((1, kw, PAGE), functools.partial(lambda i, c, pt, p: (pt[i, c * pages + p], 0, 0), p=p))
                  for p in range(pages)]
    return pl.pallas_call(
        functools.partial(_decode_mla_kernel, pages=pages),
        grid_spec=pltpu.PrefetchScalarGridSpec(
            num_scalar_prefetch=1, grid=(b, n_pages // pages),
            in_specs=[pl.BlockSpec((1, rows, kw), lambda i, c, pt: (i, 0, 0)),
                      pl.BlockSpec((1, 1, kw), lambda i, c, pt: (i, 0, 0))] + page_specs,
            out_specs=pl.BlockSpec((1, rows, MLA_KV_LORA), lambda i, c, pt: (i, 0, 0)),
            scratch_shapes=[pltpu.VMEM((rows, 1), F32), pltpu.VMEM((rows, 1), F32),
                            pltpu.VMEM((rows, MLA_KV_LORA), F32)]),
        out_shape=jax.ShapeDtypeStruct((b, rows, MLA_KV_LORA), F32),
        compiler_params=_cparams(("parallel", "arbitrary")), name="decode_mla",
    )(page_table, q, new_rows, *([view] * pages))


_DIFF_ROWS = 2 * DIFF_KVH * DIFF_GROUP
_DIFF_COLS = 2 * DIFF_KVH


def _decode_diff_kernel(pt_ref, q_ref, new_ref, *refs, pages):
    page_refs = refs[:pages]
    o_ref, m_scr, l_scr, acc_scr = refs[pages:]
    c = pl.program_id(1)
    q = q_ref[0]
    head = lax.broadcasted_iota(jnp.int32, (_DIFF_ROWS, 1), 0) >> 3

    @pl.when(c == 0)
    def _():
        new = new_ref[0].astype(BF16).astype(F32)
        knew = jnp.where(head == 0, new[:, 0:LANES], new[:, LANES:2 * LANES])
        vnew = jnp.where(head == 0, new[:, 2 * LANES:3 * LANES], new[:, 3 * LANES:4 * LANES])
        m_scr[...] = jnp.sum(q.astype(F32) * knew, axis=-1, keepdims=True)
        l_scr[...] = jnp.ones(l_scr.shape, F32)
        acc_scr[...] = vnew

    def cache_rows(c):
        return jnp.concatenate([pr[0, pl.ds(c, PAGE, stride=_DIFF_COLS), :] for pr in page_refs],
                               axis=0).astype(BF16)

    rows_q = _DIFF_ROWS // DIFF_KVH
    qf = q.astype(F32)
    s = jnp.concatenate([_dot_nt(qf[g * rows_q:(g + 1) * rows_q].astype(BF16), cache_rows(g))
                         for g in range(DIFF_KVH)], axis=0)
    alpha, p = _online_softmax_step(s, m_scr, l_scr)
    pv = jnp.concatenate([_dot(p[g * rows_q:(g + 1) * rows_q].astype(BF16), cache_rows(DIFF_KVH + g))
                          for g in range(DIFF_KVH)], axis=0)
    acc_scr[...] = alpha * acc_scr[...] + pv

    @pl.when(c == pl.num_programs(1) - 1)
    def _():
        o_ref[0] = acc_scr[...] / l_scr[...]


def _decode_diff(q, new_rows, cache_diff, page_table, pages):
    b, rows, _ = q.shape
    n_pool = cache_diff.shape[0]
    n_pages = page_table.shape[1]
    pages = min(pages, n_pages)
    view = cache_diff.reshape(n_pool, PAGE * _DIFF_COLS, LANES)
    page_specs = [pl.BlockSpec((1, PAGE * _DIFF_COLS, LANES),
                               functools.partial(lambda i, c, pt, p: (pt[i, c * pages + p], 0, 0), p=p))
                  for p in range(pages)]
    return pl.pallas_call(
        functools.partial(_decode_diff_kernel, pages=pages),
        grid_spec=pltpu.PrefetchScalarGridSpec(
            num_scalar_prefetch=1, grid=(b, n_pages // pages),
            in_specs=[pl.BlockSpec((1, rows, LANES), lambda i, c, pt: (i, 0, 0)),
                      pl.BlockSpec((1, 1, _DIFF_COLS * LANES), lambda i, c, pt: (i, 0, 0))] + page_specs,
            out_specs=pl.BlockSpec((1, rows, LANES), lambda i, c, pt: (i, 0, 0)),
            scratch_shapes=[pltpu.VMEM((rows, 1), F32), pltpu.VMEM((rows, 1), F32), pltpu.VMEM((rows, LANES), F32)]),
        out_shape=jax.ShapeDtypeStruct((b, rows, LANES), F32),
        compiler_params=_cparams(("parallel", "arbitrary")), name="decode_diff",
    )(page_table, q, new_rows, *([view] * pages))


def _nsa_decode_kernel(idx_ref, val_ref, pt_ref, q_ref, news_ref, neww_ref, win_ref, *refs, n_sel, nb_cached, pos,
                       win_pos0):
    blk_refs = refs[:n_sel]
    os_ref, ow_ref = refs[n_sel:]
    b = pl.program_id(0)
    q = q_ref[0]
    qf = q.astype(F32)

    def new_key_score(new_ref):
        row = new_ref[0].astype(BF16).astype(F32)
        kd = jnp.where(_lane(row.shape) < NSA_D, row, pltpu.roll(row, 64, axis=1))
        vnew = jnp.where(_lane(row.shape) < NSA_D, pltpu.roll(row, 64, axis=1), 0.0)
        return jnp.sum(qf * kd, axis=-1, keepdims=True), vnew

    def attend(ktv, ok, s_new, new_ok, vnew):
        ktv = ktv.astype(BF16)
        s = _dot(q, jnp.concatenate([ktv[0:NSA_D], ktv[0:NSA_D]], axis=0))
        s = jnp.where(ok, s, -jnp.inf)
        if new_ok is not True:
            s_new = jnp.where(new_ok, s_new, -jnp.inf)
        m = jnp.maximum(jnp.max(s, axis=-1, keepdims=True), s_new)
        m = jnp.where(m == -jnp.inf, 0.0, m)
        e = jnp.exp(s - m)
        e_new = jnp.exp(s_new - m)
        den = jnp.sum(e, axis=-1, keepdims=True) + e_new
        den = jnp.where(den > 0.0, den, 1.0)
        o = _dot_nt(e.astype(BF16), ktv)
        return (pltpu.roll(o, 64, axis=1) + e_new * vnew) / den

    ktv = jnp.concatenate([r[0] for r in blk_refs], axis=1)
    lane = _lane((NSA_HEADS, n_sel * PAGE))
    page_of, half_of = lane >> 7, (lane >> 6) & 1
    ok = jnp.zeros(lane.shape, F32)
    new_ok = jnp.int32(0)
    for n in range(n_sel):
        valid = val_ref[b * n_sel + n] > 0
        blk = idx_ref[b * n_sel + n]
        use = jnp.where(valid & (blk < nb_cached), 1.0, 0.0)
        ok = jnp.where(page_of == n, jnp.where(half_of == (blk & 1), use, 0.0), ok)
        new_ok = new_ok | jnp.where(valid & (blk == nb_cached), 1, 0)
    s_new, vnew = new_key_score(news_ref)
    os_ref[0] = attend(ktv, ok > 0.0, s_new, new_ok > 0, vnew)

    wpos = win_pos0 + _lane((NSA_HEADS, win_ref.shape[-1]))
    d = pos - wpos
    okw = (d >= 0) & (d < NSA_WINDOW) & (wpos >= 0)
    sw_new, vwnew = new_key_score(neww_ref)
    ow_ref[0] = attend(win_ref[0], okw, sw_new, True, vwnew)


def _nsa_decode(idx, val, page_table, q_rot, new_slc, new_win, state_win, cache_slc, pos):
    b = q_rot.shape[0]
    n_pages = page_table.shape[1]
    nb_cached = n_pages * PAGE // NSA_BLOCK
    n_sel = min(NSA_TOP_N, nb_cached + 1)
    wb = state_win.shape[1]
    blocks = _feature_major(cache_slc)
    state_win = _feature_major(state_win)

    def blk_index(i, idx_r, val_r, pt_r, n):
        blk = jnp.minimum(idx_r[i * n_sel + n], nb_cached - 1)
        return (pt_r[i, blk >> 1], 0, 0)

    blk_specs = [pl.BlockSpec((1, 2 * NSA_D, PAGE), functools.partial(blk_index, n=n)) for n in range(n_sel)]
    per_seq = lambda shp: pl.BlockSpec((1,) + shp, lambda i, a, c, d: (i, 0, 0))
    return pl.pallas_call(
        functools.partial(_nsa_decode_kernel, n_sel=n_sel, nb_cached=nb_cached, pos=pos, win_pos0=pos - wb),
        grid_spec=pltpu.PrefetchScalarGridSpec(
            num_scalar_prefetch=3, grid=(b,),
            in_specs=[per_seq((NSA_HEADS, LANES)), per_seq((1, LANES)), per_seq((1, LANES)), per_seq((2 * NSA_D, wb))]
            + blk_specs,
            out_specs=[per_seq((NSA_HEADS, LANES))] * 2),
        out_shape=[jax.ShapeDtypeStruct((b, NSA_HEADS, LANES), F32)] * 2,
        compiler_params=_cparams(("parallel",)), name="nsa_decode",
    )(idx[:, :n_sel].reshape(-1), val[:, :n_sel].reshape(-1), page_table, q_rot, new_slc, new_win, state_win, *([blocks] * n_sel))


def _mix0_kernel(x_ref, om_ref, oc_ref, os_ref, ow_ref, g_ref, w_ref, *refs, absorbed):
    if absorbed:
        uv_ref, o_ref = refs
    else:
        (o_ref,) = refs
    gates = g_ref[...]
    acc = x_ref[...]
    for h in range(MLA_HEADS):
        om = om_ref[h]
        if absorbed:
            om = _dot(om.astype(BF16), uv_ref[h])
        acc = acc + _dot(om.astype(BF16), w_ref[h])
    for h in range(NSA_HEADS):
        gc, gs, gw = (gates[:, k * NSA_HEADS + h:k * NSA_HEADS + h + 1] for k in range(3))
        on = gc * oc_ref[h] + gs * os_ref[h] + gw * ow_ref[h]
        acc = acc + _dot(on.astype(BF16), w_ref[MLA_HEADS + h])
    o_ref[...] = acc


def _mix0(x, o_mla, o_c, o_s, o_w, gates, w_out, uv=None):
    t, d = x.shape
    tm = min(t, 512)
    row = lambda w: pl.BlockSpec((tm, w), lambda i: (i, 0))
    head = pl.BlockSpec((NSA_HEADS, tm, LANES), lambda i: (0, i, 0))
    full = lambda a: pl.BlockSpec(a.shape, lambda i: (0,) * a.ndim)
    ins = [x, o_mla, o_c, o_s, o_w, gates, w_out] + ([uv] if uv is not None else [])
    specs = [row(d), head, head, head, head, row(LANES), full(w_out)] + ([full(uv)] if uv is not None else [])
    return pl.pallas_call(
        functools.partial(_mix0_kernel, absorbed=uv is not None),
        grid=(t // tm,), in_specs=specs, out_specs=row(d), out_shape=jax.ShapeDtypeStruct((t, d), F32),
        compiler_params=_cparams(("parallel",)), name="mix0",
    )(*ins)


def _ffn_kernel(x_ref, g_ref, wg_ref, wu_ref, wd_ref, fg_ref, o_ref, hn_scr, acc_scr, *, final_norm):
    f = pl.program_id(1)

    @pl.when(f == 0)
    def _():
        hn_scr[...] = _rms(x_ref[...], g_ref[...]).astype(BF16)
        acc_scr[...] = x_ref[...]

    hn = hn_scr[...]
    act = jax.nn.silu(_dot(hn, wg_ref[...])) * _dot(hn, wu_ref[...])
    acc_scr[...] += _dot(act.astype(BF16), wd_ref[...])

    @pl.when(f == pl.num_programs(1) - 1)
    def _():
        y = acc_scr[...]
        o_ref[...] = _rms(y, fg_ref[...]) if final_norm else y


def _ffn(x, gain, wg, wu, wd, final_gain, final_norm):
    t, d = x.shape
    dff = wg.shape[1]
    tm = min(t, 512)
    tf = dff // 2 if (dff // 2) % LANES == 0 else dff
    return pl.pallas_call(
        functools.partial(_ffn_kernel, final_norm=final_norm),
        grid=(t // tm, dff // tf),
        in_specs=[pl.BlockSpec((tm, d), lambda i, f: (i, 0)), pl.BlockSpec((1, d), lambda i, f: (0, 0)),
                  pl.BlockSpec((d, tf), lambda i, f: (0, f)), pl.BlockSpec((d, tf), lambda i, f: (0, f)),
                  pl.BlockSpec((tf, d), lambda i, f: (f, 0)), pl.BlockSpec((1, d), lambda i, f: (0, 0))],
        out_specs=pl.BlockSpec((tm, d), lambda i, f: (i, 0)),
        out_shape=jax.ShapeDtypeStruct((t, d), F32),
        scratch_shapes=[pltpu.VMEM((tm, d), BF16), pltpu.VMEM((tm, d), F32)],
        compiler_params=_cparams(("parallel", "arbitrary")), name="ffn",
    )(x, gain, wg, wu, wd, final_gain)


_N_QSLOT = DIFF_HEADS


def _proj1_kernel(x_ref, g_ref, w_ref, cn_ref, sn_ref, rows_ref, q_ref, k_ref, v_ref, *, tm, prompt):
    hn = _rms(x_ref[...], g_ref[...]).astype(BF16)
    z = _dot(hn, w_ref[...])
    cn, sn = cn_ref[...], sn_ref[...]
    low = _lane((tm, LANES)) < DIFF_D
    q_scale = DIFF_SCALE * (LOG2E if prompt else 1.0)
    sum_rows = jnp.where(lax.broadcasted_iota(jnp.int32, (SUM_ROW_PAD, tm), 0) == 0, 1.0, 0.0).astype(BF16)

    def slot(s):
        return z[:, s * LANES:(s + 1) * LANES]

    for g in range(DIFF_KVH):
        for r in range(DIFF_GROUP):
            h = g * DIFF_GROUP + r
            qp = (slot(h) * cn + slot(_N_QSLOT + h) * sn) * q_scale
            q_ref[2 * g, r] = jnp.where(low, qp, 0.0).astype(BF16)
            q_ref[2 * g + 1, r] = jnp.where(low, 0.0, qp).astype(BF16)
        k = slot(2 * _N_QSLOT + g) * cn + slot(2 * _N_QSLOT + DIFF_KVH + g) * sn
        v = slot(2 * _N_QSLOT + 2 * DIFF_KVH + g)
        rows_ref[:, g * LANES:(g + 1) * LANES] = k
        rows_ref[:, (DIFF_KVH + g) * LANES:(DIFF_KVH + g + 1) * LANES] = v
        k_ref[g] = k.astype(BF16)
        v_ref[g, 0:LANES, :] = v.T.astype(BF16)
        v_ref[g, LANES:LANES + SUM_ROW_PAD, :] = sum_rows


def _prep_layer1(w_in_c):
    nq = DIFF_HEADS * 2 * DIFF_D
    nk = DIFF_KVH * 2 * DIFF_D
    wq, wk, wv = w_in_c[:, :nq], w_in_c[:, nq:nq + nk], w_in_c[:, nq + nk:]
    return jnp.concatenate([wq, _rot_cols(wq, DIFF_D), wk, _rot_cols(wk, DIFF_D), wv], axis=1).astype(BF16)


def _proj1(x, gain, w1, cn, sn, prompt):
    t, d = x.shape
    tm = min(t, 256)
    row = lambda w: pl.BlockSpec((tm, w), lambda i: (i, 0))
    full = lambda a: pl.BlockSpec(a.shape, lambda i: (0,) * a.ndim)
    ng = 2 * DIFF_KVH
    return pl.pallas_call(
        functools.partial(_proj1_kernel, tm=tm, prompt=prompt),
        grid=(t // tm,),
        in_specs=[row(d), full(gain), full(w1), row(LANES), row(LANES)],
        out_specs=[row(2 * DIFF_KVH * LANES),
                   pl.BlockSpec((ng, DIFF_GROUP, tm, LANES), lambda i: (0, 0, i, 0)),
                   pl.BlockSpec((DIFF_KVH, tm, LANES), lambda i: (0, i, 0)),
                   pl.BlockSpec((DIFF_KVH, LANES + SUM_ROW_PAD, tm), lambda i: (0, 0, i))],
        out_shape=[jax.ShapeDtypeStruct((t, 2 * DIFF_KVH * LANES), F32),
                   jax.ShapeDtypeStruct((ng, DIFF_GROUP, t, LANES), BF16),
                   jax.ShapeDtypeStruct((DIFF_KVH, t, LANES), BF16),
                   jax.ShapeDtypeStruct((DIFF_KVH, LANES + SUM_ROW_PAD, t), BF16)],
        compiler_params=_cparams(("parallel",)), name="proj1",
    )(x, gain, w1, cn, sn)


def _mix1_kernel(x_ref, o_ref_in, lam_ref, sub_ref, w_ref, o_ref):
    lv = lam_ref[...]
    lam = (jnp.exp(jnp.sum(lv[0:1] * lv[1:2], axis=-1, keepdims=True))
           - jnp.exp(jnp.sum(lv[2:3] * lv[3:4], axis=-1, keepdims=True)) + DIFF_LAMBDA_INIT)
    acc = x_ref[...]
    for g in range(DIFF_KVH):
        for r in range(DIFF_GROUP):
            o = o_ref_in[2 * g, r] - lam * o_ref_in[2 * g + 1, r]
            o = _rms(o, sub_ref[...]) * (1.0 - DIFF_LAMBDA_INIT)
            acc = acc + _dot(o.astype(BF16), w_ref[g * DIFF_GROUP + r])
    o_ref[...] = acc


def _mix1(x, o, lam_vecs, subln, w_out):
    t, d = x.shape
    tm = min(t, 512)
    full = lambda a: pl.BlockSpec(a.shape, lambda i: (0,) * a.ndim)
    return pl.pallas_call(
        _mix1_kernel, grid=(t // tm,),
        in_specs=[pl.BlockSpec((tm, d), lambda i: (i, 0)),
                  pl.BlockSpec((2 * DIFF_KVH, DIFF_GROUP, tm, LANES), lambda i: (0, 0, i, 0)),
                  full(lam_vecs), full(subln), full(w_out)],
        out_specs=pl.BlockSpec((tm, d), lambda i: (i, 0)), out_shape=jax.ShapeDtypeStruct((t, d), F32),
        compiler_params=_cparams(("parallel",)), name="mix1",
    )(x, o, lam_vecs, subln, w_out)


def kernel(x_prompt, x_sample, cache_mla, cache_nsa_cmp, cache_nsa_slc, state_nsa_win, cache_diff, page_table,
           attn_norm, ffn_norm, final_norm,
           w_in_a, mla_q_norm, mla_kv_norm, w_mla_uq, w_mla_uk, w_mla_uv,
           nsa_pe_k, nsa_pe_v, nsa_cmp_w1_k, nsa_cmp_w2_k, nsa_cmp_w1_v, nsa_cmp_w2_v, w_out_a,
           w_in_c, diff_lambda_q1, diff_lambda_k1, diff_lambda_q2, diff_lambda_k2, diff_subln, w_out_c,
           w_ffn_gate, w_ffn_up, w_ffn_down):
    bp, t, d = x_prompt.shape
    b, ts, _ = x_sample.shape
    assert bp == 1 and ts == 1 and t % NSA_BLOCK == 0
    n_pool = cache_mla.shape[0]
    past = page_table.shape[1] * PAGE
    assert past % NSA_BLOCK == 0
    nh = MLA_HEADS

    w0, uq, ukv_prompt, ukt, uv_pad = _prep_layer0(w_in_a, w_mla_uq, w_mla_uk, w_mla_uv)
    pe, w1c, w2c, pe_t, w1c_t = _prep_compress(nsa_pe_k, nsa_pe_v, nsa_cmp_w1_k, nsa_cmp_w1_v, nsa_cmp_w2_k,
                                               nsa_cmp_w2_v)
    w1 = _prep_layer1(w_in_c)
    wo_a = w_out_a.reshape(2 * nh, NSA_D, d)
    wo_a = jnp.concatenate([wo_a, jnp.zeros_like(wo_a)], axis=1).astype(BF16)
    wo_c = w_out_c.reshape(DIFF_HEADS, 2 * DIFF_D, d).astype(BF16)
    wg, wu, wd = (w.astype(BF16) for w in (w_ffn_gate, w_ffn_up, w_ffn_down))
    lam_vecs = jnp.stack([diff_lambda_q1, diff_lambda_k1, diff_lambda_q2, diff_lambda_k2])
    row2 = lambda v: v.reshape(1, -1)

    xp = x_prompt[0]
    xs = x_sample[:, 0]
    tab_p = _rope_tables(jnp.arange(t, dtype=jnp.int32))
    tab_s = _rope_tables(jnp.full((b,), past, jnp.int32))

    (mla_p, cmp_p, slc_p, win_p, qm_p, qc_p, qr_p, gate_p, km_p, vm_p, ksa_p, vs_p, kw_p, vw_p) = _proj0(
        xp, row2(attn_norm[0]), w0, row2(mla_q_norm), row2(mla_kv_norm), uq, ukv_prompt, tab_p, True)
    o_mla_p = _flash(qm_p.reshape(nh, 1, t, LANES), km_p, vm_p, tq=2048, tk=2048, groups=nh,
                     q_index=lambda g, k0: g, kv_index=lambda g: g, sum_row=MLA_V, out_dtype=BF16,
                     name="flash_mla")[:, 0]
    nb_p = t // NSA_BLOCK
    nbp_p = -(-nb_p // LANES) * LANES
    kcvc_p = _compress_prompt(cmp_p, pe, w1c, w2c)
    o_c_p, imp_p = _cmp_attn_prompt(kcvc_p, qc_p, nbp_p)
    q_aug = _select_prompt(imp_p, qr_p)
    keys_per_half = LANES * NSA_BLOCK
    o_s_p = _flash(q_aug, ksa_p[None], vs_p[None], tq=512, tk=1024, groups=1,
                   q_index=lambda g, k0: k0 // keys_per_half, kv_index=lambda g: 0, sum_row=NSA_D,
                   name="flash_slc")[0]
    o_w_p = _flash(qr_p[None], kw_p[None], vw_p[None], tq=256, tk=256, groups=1, window=NSA_WINDOW,
                   q_index=lambda g, k0: 0, kv_index=lambda g: 0, sum_row=NSA_D, name="flash_win")[0]
    hp = _mix0(xp, o_mla_p, o_c_p, o_s_p, o_w_p, gate_p, wo_a)
    hp = _ffn(hp, row2(ffn_norm[0]), wg[0], wu[0], wd[0], row2(final_norm), False)

    (mla_s, cmp_s, slc_s, win_s, qm_s, qc_s, qr_s, gate_s, qa_s) = _proj0(
        xs, row2(attn_norm[0]), w0, row2(mla_q_norm), row2(mla_kv_norm), uq, ukt, tab_s, False)
    q_abs = jnp.transpose(qa_s[:, :, :MLA_KV_LORA + MLA_ROPE], (1, 0, 2))
    o_lat = _decode_mla(q_abs, mla_s[:, None, :], cache_mla, page_table, 64)
    n_pages = page_table.shape[1]
    nb_sel_s = -(-(past + ts) // NSA_BLOCK)
    nbp_s = -(-nb_sel_s // LANES) * LANES
    kcvc_s = _compress_paged(cache_nsa_cmp, page_table, pe_t, w1c_t, w2c, 64)
    o_c_s, imp_s = _cmp_attn_sample(kcvc_s, jnp.transpose(qc_s, (1, 0, 2)), past, nbp_s, n_pages)
    idx_s, val_s = _select_sample(imp_s[:, 0], past, nb_sel_s, n_pages)
    o_s_s, o_w_s = _nsa_decode(idx_s, val_s, page_table, jnp.transpose(qr_s, (1, 0, 2)), slc_s[:, None, :],
                               win_s[:, None, :], state_nsa_win, cache_nsa_slc, past)
    tr = lambda a: jnp.transpose(a, (1, 0, 2))
    hs = _mix0(xs, tr(o_lat), tr(o_c_s), tr(o_s_s), tr(o_w_s), gate_s, wo_a, uv_pad)
    hs = _ffn(hs, row2(ffn_norm[0]), wg[0], wu[0], wd[0], row2(final_norm), False)

    rows_p, qd_p, kd_p, vd_p = _proj1(hp, row2(attn_norm[1]), w1, tab_p[2], tab_p[3], True)
    o_d_p = _flash(qd_p, kd_p, vd_p, tq=1024, tk=1024, groups=2 * DIFF_KVH,
                   q_index=lambda g, k0: g, kv_index=lambda g: g // 2, sum_row=2 * DIFF_D, name="flash_diff")
    hp = _mix1(hp, o_d_p, lam_vecs, row2(diff_subln), wo_c)
    y_p = _ffn(hp, row2(ffn_norm[1]), wg[1], wu[1], wd[1], row2(final_norm), True)

    rows_s, qd_s, kd_s, vd_s = _proj1(hs, row2(attn_norm[1]), w1, tab_s[2], tab_s[3], False)
    qd = jnp.transpose(qd_s.reshape(_DIFF_ROWS, b, LANES), (1, 0, 2))
    o_d_s = _decode_diff(qd, rows_s[:, None, :], cache_diff, page_table, 32)
    o_d_s = jnp.transpose(o_d_s, (1, 0, 2)).reshape(2 * DIFF_KVH, DIFF_GROUP, b, LANES)
    hs = _mix1(hs, o_d_s, lam_vecs, row2(diff_subln), wo_c)
    y_s = _ffn(hs, row2(ffn_norm[1]), wg[1], wu[1], wd[1], row2(final_norm), True)

    wb_p = min(NSA_WINDOW, t)
    pair = lambda a: a.reshape(a.shape[0], 2, 1, NSA_D)
    new_win_s = jnp.concatenate([state_nsa_win, win_s.reshape(b, 1, 2, 1, NSA_D)], axis=1)[:, ts:]
    return (y_p[None], y_s[:, None], mla_p[None], pair(cmp_p)[None], pair(slc_p)[None], pair(win_p[t - wb_p:])[None],
            rows_p.reshape(1, t, 2, DIFF_KVH, 2 * DIFF_D),
            mla_s[:, None], pair(cmp_s)[:, None], pair(slc_s)[:, None], new_win_s,
            rows_s.reshape(b, 1, 2, DIFF_KVH, 2 * DIFF_D))
```
